```python
import jax
import jax.numpy as jnp
from jax import lax
import numpy as np

D_MODEL = 1024
BATCH = 16
SEQ = 256
DEPTH = 4
DEC_BATCH = 8
DEC_SEQ = 2048
PAST_LEN = 512

GRID_W = 64

RWKV_HEADS = 8
RWKV_HEAD_DIM = 64
RWKV_WIDTH = RWKV_HEADS * RWKV_HEAD_DIM
RWKV_DECAY_RANK = 64
RWKV_ICLR_RANK = 64
RWKV_GATE_RANK = 128
RWKV_GN_EPS = 64e-5
CONV_WIDTH = 256
CONV_KERNEL = 31
SGU_GROUPS = 4
SGU_WIDTH = 256
SGU_CHUNK = 128
POOL_GROUPS = 4
POOL_WIDTH = 256
POOL_WINDOWS = (2, 4, 8, 16)
N_BRANCHES = 4
N_EXPERTS = 16
N_EXPERT_GROUPS = 4
EXPERTS_PER_GROUP = N_EXPERTS // N_EXPERT_GROUPS
TOP_K = 2
D_EXPERT = 512

NORM_EPS = 1e-6
LN_EPS = 1e-5

IN_SPLITS = (RWKV_WIDTH, RWKV_WIDTH, RWKV_WIDTH, RWKV_DECAY_RANK, RWKV_ICLR_RANK, RWKV_GATE_RANK,
             2 * CONV_WIDTH, SGU_WIDTH, SGU_WIDTH, POOL_WIDTH, N_BRANCHES * D_MODEL)
D_IN = sum(IN_SPLITS)

kernel_name = 'hybrid_flow_rwkv7_conv_sgu_pool_moe_step'


def rms_norm(x, g):
    xf = x.astype(jnp.float32)
    y = xf * lax.rsqrt(jnp.mean(xf * xf, axis=-1, keepdims=True) + NORM_EPS)
    return (y * g.astype(jnp.float32)).astype(x.dtype)


def layer_norm(x, g, b):
    xf = x.astype(jnp.float32)
    mu = jnp.mean(xf, axis=-1, keepdims=True)
    var = jnp.mean(jnp.square(xf - mu), axis=-1, keepdims=True)
    y = (xf - mu) * lax.rsqrt(var + LN_EPS)
    return (y * g.astype(jnp.float32) + b.astype(jnp.float32)).astype(x.dtype)


def split_cols(z, sizes):
    out, off = [], 0
    for s in sizes:
        out.append(z[..., off:off + s])
        off += s
    return out


def grid_pos_embed(n_tokens, dtype):
    rows = n_tokens // GRID_W
    quarter = D_MODEL // 4
    half = D_MODEL // 2
    omega = 1.0 / (10000.0 ** (jnp.arange(quarter, dtype=jnp.float32) / quarter))
    ang_r = jnp.arange(rows, dtype=jnp.float32)[:, None] * omega
    ang_c = jnp.arange(GRID_W, dtype=jnp.float32)[:, None] * omega
    emb_r = jnp.concatenate([jnp.sin(ang_r), jnp.cos(ang_r)], axis=-1)
    emb_c = jnp.concatenate([jnp.sin(ang_c), jnp.cos(ang_c)], axis=-1)
    emb = jnp.concatenate([jnp.broadcast_to(emb_r[:, None, :], (rows, GRID_W, half)),
                           jnp.broadcast_to(emb_c[None, :, :], (rows, GRID_W, half))], axis=-1)
    return emb.reshape(rows * GRID_W, D_MODEL).astype(dtype)


def wkv_scan(s0, r, w, k, v, kk, a, reverse):
    def step(s, inp):
        r_t, w_t, k_t, v_t, kk_t, a_t = inp
        s_kk = jnp.einsum('bhvk,bhk->bhv', s, kk_t)
        s = (s * w_t[:, :, None, :] - s_kk[..., None] * (kk_t * a_t)[:, :, None, :]
             + v_t[..., None] * k_t[:, :, None, :])
        return s, jnp.einsum('bhvk,bhk->bhv', s, r_t)
    xs = tuple(jnp.swapaxes(t.astype(jnp.float32), 0, 1) for t in (r, w, k, v, kk, a))
    s_fin, ys = lax.scan(step, s0.astype(jnp.float32), xs, reverse=reverse)
    return jnp.swapaxes(ys, 0, 1), s_fin


def rwkv7_mixer(r, k, v, wd, ad, gd, s0, lp):
    bsz, n = r.shape[:2]
    heads = lambda t: t.reshape(bsz, n, RWKV_HEADS, RWKV_HEAD_DIM).astype(jnp.float32)
    g = jax.nn.sigmoid(gd) @ lp['rwkv_g_up']
    kk = heads(k * lp['rwkv_k_k'])
    kk = kk / jnp.maximum(jnp.sqrt(jnp.sum(kk * kk, axis=-1, keepdims=True)), 1e-12)
    rh, vh = heads(r), heads(v)
    r_k = lp['rwkv_r_k'].astype(jnp.float32)
    y_dirs, s_dirs = [], []
    for d, rev in enumerate((False, True)):
        w_pre = (lp['rwkv_w0'][d] + jnp.tanh(wd) @ lp['rwkv_w_up'][d]).astype(jnp.float32)
        decay = jnp.exp(-jnp.exp(-jax.nn.softplus(-w_pre) - 0.5))
        a = jax.nn.sigmoid(lp['rwkv_a0'][d] + ad @ lp['rwkv_a_up'][d])
        k_d = k * (1 + (a - 1) * lp['rwkv_k_a'])
        kdh = heads(k_d)
        y, s_fin = wkv_scan(s0[:, d], rh, heads(decay), kdh, vh, kk, heads(a), rev)
        bonus = jnp.sum(rh * kdh * r_k, axis=-1, keepdims=True) * vh
        y_dirs.append(y + bonus)
        s_dirs.append(s_fin)
    y = y_dirs[0] + y_dirs[1]
    mu = jnp.mean(y, axis=-1, keepdims=True)
    var = jnp.mean(jnp.square(y - mu), axis=-1, keepdims=True)
    yn = ((y - mu) * lax.rsqrt(var + RWKV_GN_EPS)).reshape(bsz, n, RWKV_WIDTH)
    yn = yn * lp['rwkv_gn_g'].astype(jnp.float32) + lp['rwkv_gn_b'].astype(jnp.float32)
    out = (yn.astype(r.dtype) * g) @ lp['rwkv_w_o']
    return out, jnp.stack(s_dirs, axis=1)


def conv_module(z, lp):
    val, gate = jnp.split(z, 2, axis=-1)
    u = val * jax.nn.sigmoid(gate)
    u = lax.conv_general_dilated(u, lp['conv_dw'][:, None, :], window_strides=(1,), padding='SAME',
                                 dimension_numbers=('NWC', 'WIO', 'NWC'),
                                 feature_group_count=CONV_WIDTH) + lp['conv_dw_b']
    u = jax.nn.silu(layer_norm(u, lp['conv_ln_g'], lp['conv_ln_b']))
    return u @ lp['conv_w_o']


def chunk_sgu(u, v, lp):
    bsz, n, _ = v.shape
    n_chunks = n // SGU_CHUNK
    vn = layer_norm(v, lp['sgu_ln_g'], lp['sgu_ln_b'])
    vn = vn.reshape(bsz, n_chunks, SGU_CHUNK, SGU_GROUPS, SGU_WIDTH // SGU_GROUPS)
    s = jnp.einsum('gpq,bcqgd->bcpgd', lp['sgu_w_s'], vn) + lp['sgu_b_s'].T[:, :, None]
    return (u * s.reshape(bsz, n, SGU_WIDTH)) @ lp['sgu_w_o']


def multiscale_pool(z, lp):
    bsz, n, _ = z.shape
    zf = z.reshape(bsz, n, POOL_GROUPS, POOL_WIDTH // POOL_GROUPS).astype(jnp.float32)
    cs = jnp.concatenate([jnp.zeros_like(zf[:, :1]), jnp.cumsum(zf, axis=1)], axis=1)
    t = jnp.arange(n)
    pooled = []
    for gi, win in enumerate(POOL_WINDOWS):
        lo = jnp.clip(t - win // 2, 0, n)
        hi = jnp.clip(t + win - win // 2, 0, n)
        csg = cs[:, :, gi]
        cnt = (hi - lo).astype(jnp.float32)[None, :, None]
        pooled.append((csg[:, hi] - csg[:, lo]) / cnt - zf[:, :, gi])
    p = jnp.stack(pooled, axis=2).astype(z.dtype)
    p = jnp.einsum('btgc,gcd->btgd', p, lp['pool_w']).reshape(bsz, n, POOL_WIDTH) * lp['pool_scale']
    return p @ lp['pool_w_o']


def token_mix(h, s0, lp):
    r, k, v, wd, ad, gd, zb, su, sv, zp, zg = split_cols(h @ lp['w_in'] + lp['b_in'], IN_SPLITS)
    y_a, s_fin = rwkv7_mixer(r, k, v, wd, ad, gd, s0, lp)
    y_b = conv_module(zb, lp)
    y_c = chunk_sgu(su, sv, lp)
    y_d = multiscale_pool(zp, lp)
    gates = jax.nn.sigmoid(zg).reshape(zg.shape[:-1] + (N_BRANCHES, D_MODEL))
    merged = (gates[..., 0, :] * y_a + gates[..., 1, :] * y_b
              + gates[..., 2, :] * y_c + gates[..., 3, :] * y_d)
    return merged @ lp['w_out'], s_fin


def moe_ffn(h, router_w, router_b, w_gate, w_up, w_down):
    logits = h.astype(jnp.float32) @ router_w.astype(jnp.float32) + router_b.astype(jnp.float32)
    probs = jax.nn.softmax(logits, axis=-1)
    grouped = probs.reshape(probs.shape[:-1] + (N_EXPERT_GROUPS, EXPERTS_PER_GROUP))
    group_score = jnp.sum(lax.top_k(grouped, TOP_K)[0], axis=-1)
    best = jnp.argmax(group_score, axis=-1)
    in_group = (jnp.arange(N_EXPERTS) // EXPERTS_PER_GROUP) == best[..., None]
    vals, idx = lax.top_k(jnp.where(in_group, probs, -jnp.inf), TOP_K)
    wts = vals / jnp.sum(vals, axis=-1, keepdims=True)
    gates = jnp.sum(jax.nn.one_hot(idx, N_EXPERTS, dtype=jnp.float32) * wts[..., None], axis=-2).astype(h.dtype)
    out = jnp.zeros_like(h)
    for e in range(N_EXPERTS):
        ff = (jax.nn.silu(h @ w_gate[e]) * (h @ w_up[e])) @ w_down[e]
        out = out + gates[..., e:e + 1] * ff
    return out


def trunk_layer(x, cond, s0, lp, router_w, router_b):
    mod = jax.nn.silu(cond) @ lp['w_mod'] + lp['b_mod']
    sh1, sc1, g1, sh2, sc2, g2 = jnp.split(mod[:, None, :], 6, axis=-1)
    h = rms_norm(x, lp['norm1_g']) * (1 + sc1) + sh1
    mixed, s_fin = token_mix(h, s0, lp)
    x = x + g1 * mixed
    h = rms_norm(x, lp['norm2_g']) * (1 + sc2) + sh2
    x = x + g2 * moe_ffn(h, router_w, router_b, lp['moe_w_gate'], lp['moe_w_up'], lp['moe_w_down'])
    return x, s_fin


def setup_inputs(seed: int = 0) -> dict:
    key = jax.random.key(seed)
    keys = iter(jax.random.split(key, 64))
    nrm = lambda shape, scale: scale * jax.random.normal(next(keys), shape, jnp.float32)
    L, D = DEPTH, D_MODEL
    return {
        'x_prompt': nrm((BATCH, SEQ, D), 1.0),
        'x_sample': nrm((DEC_BATCH, DEC_SEQ, D), 1.0),
        'state_rwkv': nrm((DEC_BATCH, DEPTH, 2, RWKV_HEADS, RWKV_HEAD_DIM, RWKV_HEAD_DIM), 1.0),
        'c': nrm((DEC_BATCH, D), 1.0),
        'c_ctx': nrm((D,), 1.0),
        'norm1_g': 1.0 + nrm((L, D), 0.05),
        'norm2_g': 1.0 + nrm((L, D), 0.05),
        'w_mod': nrm((L, D, 6 * D), 0.5 * D ** -0.5),
        'b_mod': nrm((L, 6 * D), 0.01),
        'w_in': nrm((L, D, D_IN), D ** -0.5),
        'b_in': nrm((L, D_IN), 0.01),
        'rwkv_w0': nrm((L, 2, RWKV_WIDTH), 0.5),
        'rwkv_w_up': nrm((L, 2, RWKV_DECAY_RANK, RWKV_WIDTH), 0.1),
        'rwkv_a0': nrm((L, 2, RWKV_WIDTH), 0.5),
        'rwkv_a_up': nrm((L, 2, RWKV_ICLR_RANK, RWKV_WIDTH), 0.1),
        'rwkv_g_up': nrm((L, RWKV_GATE_RANK, RWKV_WIDTH), RWKV_GATE_RANK ** -0.5),
        'rwkv_k_k': 0.85 + nrm((L, RWKV_WIDTH), 0.05),
        'rwkv_k_a': 1.0 + nrm((L, RWKV_WIDTH), 0.05),
        'rwkv_r_k': nrm((L, RWKV_HEADS, RWKV_HEAD_DIM), 0.1),
        'rwkv_gn_g': 1.0 + nrm((L, RWKV_WIDTH), 0.05),
        'rwkv_gn_b': nrm((L, RWKV_WIDTH), 0.01),
        'rwkv_w_o': nrm((L, RWKV_WIDTH, D), RWKV_WIDTH ** -0.5),
        'conv_dw': nrm((L, CONV_KERNEL, CONV_WIDTH), CONV_KERNEL ** -0.5),
        'conv_dw_b': nrm((L, CONV_WIDTH), 0.01),
        'conv_ln_g': 1.0 + nrm((L, CONV_WIDTH), 0.05),
        'conv_ln_b': nrm((L, CONV_WIDTH), 0.01),
        'conv_w_o': nrm((L, CONV_WIDTH, D), CONV_WIDTH ** -0.5),
        'sgu_ln_g': 1.0 + nrm((L, SGU_WIDTH), 0.05),
        'sgu_ln_b': nrm((L, SGU_WIDTH), 0.01),
        'sgu_w_s': nrm((L, SGU_GROUPS, SGU_CHUNK, SGU_CHUNK), SGU_CHUNK ** -0.5),
        'sgu_b_s': 1.0 + nrm((L, SGU_GROUPS, SGU_CHUNK), 0.01),
        'sgu_w_o': nrm((L, SGU_WIDTH, D), SGU_WIDTH ** -0.5),
        'pool_w': nrm((L, POOL_GROUPS, POOL_WIDTH // POOL_GROUPS, POOL_WIDTH // POOL_GROUPS), (POOL_WIDTH // POOL_GROUPS) ** -0.5),
        'pool_scale': 1.0 + nrm((L, POOL_WIDTH), 0.05),
        'pool_w_o': nrm((L, POOL_WIDTH, D), POOL_WIDTH ** -0.5),
        'w_out': nrm((L, D, D), D ** -0.5),
        'moe_w_gate': nrm((L, N_EXPERTS, D, D_EXPERT), D ** -0.5),
        'moe_w_up': nrm((L, N_EXPERTS, D, D_EXPERT), D ** -0.5),
        'moe_w_down': nrm((L, N_EXPERTS, D_EXPERT, D), D_EXPERT ** -0.5),
        'router_w': nrm((D, N_EXPERTS), D ** -0.5),
        'router_b': nrm((N_EXPERTS,), 0.01),
        'final_norm_g': 1.0 + nrm((D,), 0.05),
    }


def reference(x_prompt, x_sample, state_rwkv, c, c_ctx,
              norm1_g, norm2_g, w_mod, b_mod, w_in, b_in,
              rwkv_w0, rwkv_w_up, rwkv_a0, rwkv_a_up, rwkv_g_up, rwkv_k_k, rwkv_k_a, rwkv_r_k,
              rwkv_gn_g, rwkv_gn_b, rwkv_w_o,
              conv_dw, conv_dw_b, conv_ln_g, conv_ln_b, conv_w_o,
              sgu_ln_g, sgu_ln_b, sgu_w_s, sgu_b_s, sgu_w_o,
              pool_w, pool_scale, pool_w_o,
              w_out, moe_w_gate, moe_w_up, moe_w_down, router_w, router_b, final_norm_g):
    ctx = x_prompt
    lat = x_sample + grid_pos_embed(x_sample.shape[1], x_sample.dtype)[None]
    ctx_cond = c_ctx[None, :]
    ctx_s0 = jnp.zeros((x_prompt.shape[0], 2, RWKV_HEADS, RWKV_HEAD_DIM, RWKV_HEAD_DIM), jnp.float32)
    ctx_states = []
    for l in range(DEPTH):
        lp = {
            'norm1_g': norm1_g[l], 'norm2_g': norm2_g[l], 'w_mod': w_mod[l], 'b_mod': b_mod[l],
            'w_in': w_in[l], 'b_in': b_in[l],
            'rwkv_w0': rwkv_w0[l], 'rwkv_w_up': rwkv_w_up[l], 'rwkv_a0': rwkv_a0[l], 'rwkv_a_up': rwkv_a_up[l],
            'rwkv_g_up': rwkv_g_up[l], 'rwkv_k_k': rwkv_k_k[l], 'rwkv_k_a': rwkv_k_a[l], 'rwkv_r_k': rwkv_r_k[l],
            'rwkv_gn_g': rwkv_gn_g[l], 'rwkv_gn_b': rwkv_gn_b[l], 'rwkv_w_o': rwkv_w_o[l],
            'conv_dw': conv_dw[l], 'conv_dw_b': conv_dw_b[l], 'conv_ln_g': conv_ln_g[l], 'conv_ln_b': conv_ln_b[l],
            'conv_w_o': conv_w_o[l],
            'sgu_ln_g': sgu_ln_g[l], 'sgu_ln_b': sgu_ln_b[l], 'sgu_w_s': sgu_w_s[l], 'sgu_b_s': sgu_b_s[l],
            'sgu_w_o': sgu_w_o[l],
            'pool_w': pool_w[l], 'pool_scale': pool_scale[l], 'pool_w_o': pool_w_o[l],
            'w_out': w_out[l],
            'moe_w_gate': moe_w_gate[l], 'moe_w_up': moe_w_up[l], 'moe_w_down': moe_w_down[l],
        }
        ctx, ctx_s_fin = trunk_layer(ctx, ctx_cond, ctx_s0, lp, router_w, router_b)
        ctx_states.append(ctx_s_fin)
        lat, _ = trunk_layer(lat, c, state_rwkv[:, l], lp, router_w, router_b)
    y_prompt = rms_norm(ctx, final_norm_g)
    y_sample = rms_norm(lat, final_norm_g)
    new_state_rwkv = jnp.stack(ctx_states, axis=1).astype(x_prompt.dtype)
    return (y_prompt, y_sample, new_state_rwkv)
```

```python
import functools
import math

import jax
import jax.numpy as jnp
from jax import lax
from jax.experimental import pallas as pl
from jax.experimental.pallas import tpu as pltpu

F32 = jnp.float32
BF16 = jnp.bfloat16

D_MODEL = 1024
GRID_W = 64
RWKV_HEADS = 8
RWKV_HEAD_DIM = 64
RWKV_WIDTH = RWKV_HEADS * RWKV_HEAD_DIM
RWKV_DECAY_RANK = 64
RWKV_ICLR_RANK = 64
RWKV_GATE_RANK = 128
RWKV_GN_EPS = 64e-5
CONV_WIDTH = 256
CONV_KERNEL = 31
SGU_GROUPS = 4
SGU_WIDTH = 256
SGU_CHUNK = 128
POOL_GROUPS = 4
POOL_WIDTH = 256
POOL_WINDOWS = (2, 4, 8, 16)
N_BRANCHES = 4
N_EXPERTS = 16
N_EXPERT_GROUPS = 4
EXPERTS_PER_GROUP = N_EXPERTS // N_EXPERT_GROUPS
D_EXPERT = 512
NORM_EPS = 1e-6
LN_EPS = 1e-5

RW_COLS = 3 * RWKV_WIDTH + RWKV_DECAY_RANK + RWKV_ICLR_RANK + RWKV_GATE_RANK
MX_COLS = 2 * CONV_WIDTH + 2 * SGU_WIDTH + POOL_WIDTH
MX_OUT = CONV_WIDTH + SGU_WIDTH + POOL_WIDTH

TILE = 256
HALO = 16
HG_LANES = 256
HEADS_PER_HG = HG_LANES // RWKV_HEAD_DIM
N_HG = RWKV_WIDTH // HG_LANES
SCAN_CHUNK = 64
DECAY_SCALE = math.exp(-0.5)
MOD_ROWS = 16

VMEM_LIMIT = 48 * 1024 * 1024


def _cparams(sem):
    return pltpu.CompilerParams(dimension_semantics=sem, vmem_limit_bytes=VMEM_LIMIT)


def _dot(a, b):
    return jnp.dot(a, b, preferred_element_type=F32)


def _dot_nt(a, b):
    return lax.dot_general(a, b, (((1,), (1,)), ((), ())), preferred_element_type=F32)


def _dot_tn(a, b):
    return lax.dot_general(a, b, (((0,), (0,)), ((), ())), preferred_element_type=F32)


def _split2(x):
    hi = x.astype(BF16)
    lo = (x - hi.astype(F32)).astype(BF16)
    return hi, lo


def _dot3(a, b):
    a_hi, a_lo = _split2(a)
    b_hi, b_lo = _split2(b)
    return _dot(a_hi, b_hi) + (_dot(a_lo, b_hi) + _dot(a_hi, b_lo))


def _dot_exact_rhs(x, rhs):
    hi, lo = _split2(x)
    return _dot(hi, rhs) + _dot(lo, rhs)


def _dot_exact_lhs(lhs, x):
    hi, lo = _split2(x)
    return _dot(lhs, hi) + _dot(lhs, lo)


def _sigmoid(x):
    return 1.0 / (1.0 + jnp.exp(-x))


def _silu(x):
    return x * _sigmoid(x)


def _norm_mod(x, g, shift, scale):
    y = x * lax.rsqrt(jnp.mean(x * x, axis=-1, keepdims=True) + NORM_EPS) * g
    return y * (1.0 + scale) + shift


def _layer_norm(x, g, b):
    mu = jnp.mean(x, axis=-1, keepdims=True)
    xc = x - mu
    var = jnp.mean(xc * xc, axis=-1, keepdims=True)
    return xc * lax.rsqrt(var + LN_EPS) * g + b


def _mod_kernel(c_ref, w_ref, b_ref, o_ref):
    o_ref[...] = _dot3(_silu(c_ref[...]), w_ref[...]) + b_ref[...]


def _modulation(cond, w_mod, b_mod):
    n_layers = w_mod.shape[0]
    tn = 1536
    return pl.pallas_call(
        _mod_kernel,
        grid=(n_layers, 6 * D_MODEL // tn),
        in_specs=[
            pl.BlockSpec((MOD_ROWS, D_MODEL), lambda l, j: (0, 0)),
            pl.BlockSpec((None, D_MODEL, tn), lambda l, j: (l, 0, j)),
            pl.BlockSpec((None, 1, tn), lambda l, j: (l, 0, j)),
        ],
        out_specs=pl.BlockSpec((None, MOD_ROWS, tn), lambda l, j: (l, 0, j)),
        out_shape=jax.ShapeDtypeStruct((n_layers, MOD_ROWS, 6 * D_MODEL), F32),
        compiler_params=_cparams(("parallel", "parallel")),
        name="modulation",
    )(cond, w_mod, b_mod.reshape(n_layers, 1, 6 * D_MODEL))


def _embed_kernel(n_ctx_tiles, xp_ref, xs_ref, pos_ref, o_ref):
    i = pl.program_id(0)

    @pl.when(i < n_ctx_tiles)
    def _():
        o_ref[...] = xp_ref[...]

    @pl.when(i >= n_ctx_tiles)
    def _():
        o_ref[...] = xs_ref[...] + pos_ref[...]


def _embed(xp, xs, pos, lat_tiles):
    n_ctx_tiles = xp.shape[0] // TILE
    n_lat_tiles = xs.shape[0] // TILE
    n_tiles = n_ctx_tiles + n_lat_tiles
    return pl.pallas_call(
        functools.partial(_embed_kernel, n_ctx_tiles),
        grid=(n_tiles,),
        in_specs=[
            pl.BlockSpec((TILE, D_MODEL), lambda i: (jnp.minimum(i, n_ctx_tiles - 1), 0)),
            pl.BlockSpec((TILE, D_MODEL), lambda i: (jnp.maximum(i - n_ctx_tiles, 0), 0)),
            pl.BlockSpec((TILE, D_MODEL), lambda i: (jnp.maximum(i - n_ctx_tiles, 0) % lat_tiles, 0)),
        ],
        out_specs=pl.BlockSpec((TILE, D_MODEL), lambda i: (i, 0)),
        out_shape=jax.ShapeDtypeStruct((n_tiles * TILE, D_MODEL), F32),
        compiler_params=_cparams(("parallel",)),
        name="embed",
    )(xp, xs, pos)


def _inproj_kernel(x_ref, mod_ref, g_ref, wa_ref, ba_ref, wm_ref, bm_ref, zr_ref, zm_ref):
    h = _norm_mod(x_ref[...], g_ref[...], mod_ref[:, 0:D_MODEL], mod_ref[:, D_MODEL:2 * D_MODEL])
    hb = h.astype(BF16)
    zr_ref[...] = _dot(hb, wa_ref[...]) + ba_ref[...]
    zm_ref[...] = _dot(hb, wm_ref[...]) + bm_ref[...]


def _const_spec(shape):
    nd = len(shape)
    return pl.BlockSpec(shape, lambda *_: (0,) * nd)


def _in_projection(x, mod_l, mod_row, norm_g, wa, ba, wm, bm):
    n = x.shape[0]
    return pl.pallas_call(
        _inproj_kernel,
        grid=(n // TILE,),
        in_specs=[
            pl.BlockSpec((TILE, D_MODEL), lambda i: (i, 0)),
            pl.BlockSpec((None, 1, 6 * D_MODEL), lambda i: (mod_row(i), 0, 0)),
            _const_spec((1, D_MODEL)),
            _const_spec((D_MODEL, RW_COLS)),
            _const_spec((1, RW_COLS)),
            _const_spec((D_MODEL, MX_COLS)),
            _const_spec((1, MX_COLS)),
        ],
        out_specs=[
            pl.BlockSpec((TILE, RW_COLS), lambda i: (i, 0)),
            pl.BlockSpec((TILE, MX_COLS), lambda i: (i, 0)),
        ],
        out_shape=[
            jax.ShapeDtypeStruct((n, RW_COLS), F32),
            jax.ShapeDtypeStruct((n, MX_COLS), F32),
        ],
        compiler_params=_cparams(("parallel",)),
        name="in_projection",
    )(x, mod_l, norm_g, wa, ba, wm, bm)


P_W0, P_A0, P_KK, P_KA, P_RK, P_GNG, P_GNB = 0, 2, 4, 5, 6, 7, 8
P_ROWS = 16


def _rep(xb, reps, mask):
    return jnp.concatenate([xb] * reps, axis=0) * mask


def _rwkv_kernel(t_len, r_ref, k_ref, v_ref, lr_ref, s0_ref, par_ref, wup_ref, aup_ref, gup_ref,
                 bd_ref, cm_ref, tri_ref, eye_ref, out_ref, sfin_ref, y_scr, st_scr):
    c = SCAN_CHUNK
    n_chunks = t_len // c
    bd = bd_ref[...]
    eye = eye_ref[...]
    eye_c = eye[0:c, :] + eye[c:2 * c, :] + eye[2 * c:3 * c, :] + eye[3 * c:4 * c, :]

    st_scr[...] = s0_ref[...]

    k_k = par_ref[P_KK:P_KK + 1, :]
    k_a = par_ref[P_KA:P_KA + 1, :]
    r_k = par_ref[P_RK:P_RK + 1, :]

    def seg_sum(x):
        return _dot_exact_rhs(x, bd)

    def chunk_step(d, row0):
        rows = pl.ds(row0, c)
        r = r_ref[rows, :]
        k = k_ref[rows, :]
        v = v_ref[rows, :]
        lr = lr_ref[rows, :]
        w0 = par_ref[P_W0 + d:P_W0 + d + 1, :]
        a0 = par_ref[P_A0 + d:P_A0 + d + 1, :]

        kkr = k * k_k
        kk = kkr / jnp.maximum(jnp.sqrt(seg_sum(kkr * kkr)), 1e-12)
        w_pre = w0 + _dot(jnp.tanh(lr).astype(BF16), wup_ref[d])
        lw = -DECAY_SCALE * _sigmoid(w_pre)
        a = _sigmoid(a0 + _dot(lr.astype(BF16), aup_ref[d]))
        kd = k * (1.0 + (a - 1.0) * k_a)
        b = kk * a

        cum = _dot_exact_lhs(tri_ref[d], lw)
        e_incl = jnp.exp(cum)
        e_excl = jnp.exp(cum - lw)
        e_neg = jnp.exp(-cum)
        gam = e_incl[c - 1:c, :] if d == 0 else e_incl[0:1, :]

        at = (-kk * e_excl).astype(BF16)
        rt = r * e_incl
        bt = b * e_neg
        kt = kd * e_neg
        vb = v.astype(BF16)

        lhs2 = jnp.concatenate([at, rt.astype(BF16)], axis=0)
        m_b = _dot_nt(lhs2, _rep(bt.astype(BF16), HEADS_PER_HG, bd))
        m_k = _dot_nt(lhs2, _rep(kt.astype(BF16), HEADS_PER_HG, bd))
        strict = cm_ref[d, 0]
        incl = cm_ref[d, 1]
        a_ab = m_b[0:c] * strict
        a_rb = (m_b[c:2 * c] * incl).astype(BF16)
        a_ak = (m_k[0:c] * strict).astype(BF16)
        a_rk = (m_k[c:2 * c] * incl).astype(BF16)

        x = a_ab
        t_acc = eye_c + x
        n_sq = int(math.log2(c))
        for step in range(1, n_sq):
            x_bd = _rep(x.astype(BF16), HEADS_PER_HG, bd)
            if step == 1:
                x = _dot(x.astype(BF16), x_bd)
            else:
                both = _dot(jnp.concatenate([x, t_acc], axis=0).astype(BF16), x_bd)
                x, t_acc = both[0:c], t_acc + both[c:2 * c]
        t_acc = t_acc + _dot(t_acc.astype(BF16), _rep(x.astype(BF16), HEADS_PER_HG, bd))
        t_b = t_acc.astype(BF16)

        v_bd = _rep(vb, HEADS_PER_HG, bd)
        g1 = _dot(a_ak, v_bd)
        w = _dot(t_b, _rep(at, HEADS_PER_HG, bd))
        u0 = _dot(t_b, _rep(g1.astype(BF16), HEADS_PER_HG, bd))
        wb = w.astype(BF16)
        u0b = u0.astype(BF16)
        q = rt + _dot(a_rb, _rep(wb, HEADS_PER_HG, bd))
        y0 = _dot(a_rb, _rep(u0b, HEADS_PER_HG, bd)) + _dot(a_rk, v_bd)
        bg = (bt * gam).astype(BF16)
        kg = (kt * gam).astype(BF16)
        bdf = bd.astype(F32)
        p_bd = (eye * gam + _dot_tn(bg, wb)) * bdf
        z_bd = _dot_tn(jnp.concatenate([bg, kg], axis=0), jnp.concatenate([u0b, vb], axis=0)) * bdf

        st = st_scr[d]
        stb = st.astype(BF16)
        y = _dot(q.astype(BF16), stb) + y0
        st_scr[d] = _dot(p_bd.astype(BF16), stb) + z_bd
        bonus = seg_sum(r * kd * r_k) * v
        y_scr[d, rows, :] = y + bonus

    def body(j, carry):
        chunk_step(0, pl.multiple_of(j * c, c))
        chunk_step(1, pl.multiple_of((n_chunks - 1 - j) * c, c))
        return carry

    lax.fori_loop(0, n_chunks, body, 0)
    sfin_ref[...] = st_scr[...]

    gn_g = par_ref[P_GNG:P_GNG + 1, :]
    gn_b = par_ref[P_GNB:P_GNB + 1, :]
    inv_n = 1.0 / RWKV_HEAD_DIM

    def finish(j, carry):
        rows = pl.ds(pl.multiple_of(j * TILE, TILE), TILE)
        y = y_scr[0, rows, :] + y_scr[1, rows, :]
        mu = seg_sum(y) * inv_n
        yc = y - mu
        var = seg_sum(yc * yc) * inv_n
        yn = yc * lax.rsqrt(var + RWKV_GN_EPS) * gn_g + gn_b
        g = _dot(_sigmoid(lr_ref[rows, :]).astype(BF16), gup_ref[...])
        out_ref[rows, :] = (yn * g).astype(BF16)
        return carry

    lax.fori_loop(0, t_len // TILE, finish, 0)


def _rwkv_mixer(zr, s0_bd, n_seq, t_len, row_off, par, wup, aup, gup, consts):
    bd, cm, tri, eye = consts
    col = lambda cb: (lambda b, g: (b + row_off, cb + g))
    lr_block = 3 * RWKV_WIDTH // HG_LANES
    return pl.pallas_call(
        functools.partial(_rwkv_kernel, t_len),
        grid=(n_seq, N_HG),
        in_specs=[
            pl.BlockSpec((t_len, HG_LANES), col(0)),
            pl.BlockSpec((t_len, HG_LANES), col(N_HG)),
            pl.BlockSpec((t_len, HG_LANES), col(2 * N_HG)),
            pl.BlockSpec((t_len, HG_LANES), lambda b, g: (b + row_off, lr_block)),
            pl.BlockSpec((None, None, 2, HG_LANES, HG_LANES), lambda b, g: (b, g, 0, 0, 0)),
            pl.BlockSpec((None, P_ROWS, HG_LANES), lambda b, g: (g, 0, 0)),
            pl.BlockSpec((2, None, HG_LANES, HG_LANES), lambda b, g: (0, g, 0, 0)),
            pl.BlockSpec((2, None, HG_LANES, HG_LANES), lambda b, g: (0, g, 0, 0)),
            pl.BlockSpec((None, HG_LANES, HG_LANES), lambda b, g: (g, 0, 0)),
            _const_spec(bd.shape),
            _const_spec(cm.shape),
            _const_spec(tri.shape),
            _const_spec(eye.shape),
        ],
        out_specs=[
            pl.BlockSpec((t_len, HG_LANES), lambda b, g: (b, g)),
            pl.BlockSpec((None, None, 2, HG_LANES, HG_LANES), lambda b, g: (b, g, 0, 0, 0)),
        ],
        out_shape=[
            jax.ShapeDtypeStruct((n_seq * t_len, RWKV_WIDTH), BF16),
            jax.ShapeDtypeStruct((n_seq, N_HG, 2, HG_LANES, HG_LANES), F32),
        ],
        scratch_shapes=[
            pltpu.VMEM((2, t_len, HG_LANES), F32),
            pltpu.VMEM((2, HG_LANES, HG_LANES), F32),
        ],
        compiler_params=_cparams(("parallel", "parallel")),
        name="rwkv_mixer",
    )(zr, zr, zr, zr, s0_bd, par, wup, aup, gup, bd, cm, tri, eye)


def _rwkv_consts():
    c = SCAN_CHUNK
    ri = jnp.arange(HG_LANES)[:, None]
    ci = jnp.arange(HG_LANES)[None, :]
    bd = (ri // RWKV_HEAD_DIM == ci // RWKV_HEAD_DIM).astype(BF16)
    t = jnp.arange(c)[:, None]
    i = (jnp.arange(HEADS_PER_HG * c) % c)[None, :]
    cm = jnp.stack([jnp.stack([i < t, i <= t]), jnp.stack([i > t, i >= t])]).astype(F32)
    ti = jnp.arange(c)[None, :]
    tri = jnp.stack([ti <= t, ti >= t]).astype(BF16)
    eye = jnp.eye(HG_LANES, dtype=F32)
    return bd, cm, tri, eye


def _states_to_bd(s):
    b = s.shape[0]
    st = jnp.swapaxes(s, -1, -2).reshape(b, 2, N_HG, HEADS_PER_HG, RWKV_HEAD_DIM, RWKV_HEAD_DIM)
    eye = jnp.eye(HEADS_PER_HG, dtype=s.dtype)
    full = st[:, :, :, :, :, None, :] * eye[None, None, None, :, None, :, None]
    full = full.reshape(b, 2, N_HG, HG_LANES, HG_LANES)
    return jnp.swapaxes(full, 1, 2)


def _bd_to_states(sbd):
    b = sbd.shape[0]
    x = jnp.swapaxes(sbd, 1, 2).reshape(b, 2, N_HG, HEADS_PER_HG, RWKV_HEAD_DIM, HEADS_PER_HG, RWKV_HEAD_DIM)
    diag = jnp.stack([x[:, :, :, h, :, h, :] for h in range(HEADS_PER_HG)], axis=3)
    return jnp.swapaxes(diag, -1, -2).reshape(b, 2, RWKV_HEADS, RWKV_HEAD_DIM, RWKV_HEAD_DIM)


def _mixers_kernel(ctx_tiles, seq_tiles_ctx, seq_tiles_lat, zc_ref, zp_ref, zn_ref, dw_ref, vec_ref, wcat_ref,
                   bs_ref, pw_ref, sgm_ref, plm_ref, out_ref, win_scr):
    i = pl.program_id(0)
    j_ctx = i % seq_tiles_ctx
    j_lat = jnp.maximum(i - ctx_tiles, 0) % seq_tiles_lat
    is_ctx = i < ctx_tiles
    first = jnp.where(is_ctx, j_ctx == 0, j_lat == 0)
    last = jnp.where(is_ctx, j_ctx == seq_tiles_ctx - 1, j_lat == seq_tiles_lat - 1)
    keep_prev = jnp.where(first, 0.0, 1.0)
    keep_next = jnp.where(last, 0.0, 1.0)

    cw = CONV_WIDTH
    conv_b, cln_g, cln_b = vec_ref[0:1, :], vec_ref[1:2, :], vec_ref[2:3, :]
    sln_g, sln_b, pool_scale = vec_ref[3:4, :], vec_ref[4:5, :], vec_ref[5:6, :]

    glu = lambda z: z[:, 0:cw] * _sigmoid(z[:, cw:2 * cw])
    win_scr[0:HALO, :] = glu(zp_ref[...]) * keep_prev
    win_scr[HALO:HALO + TILE, :] = glu(zc_ref[...])
    win_scr[HALO + TILE:2 * HALO + TILE, :] = glu(zn_ref[...]) * keep_next
    acc = jnp.zeros((TILE, cw), F32) + conv_b
    pad = CONV_KERNEL // 2
    for j in range(CONV_KERNEL):
        acc = acc + win_scr[pl.ds(HALO - pad + j, TILE), :] * dw_ref[j:j + 1, :]
    out_ref[:, 0:cw] = _silu(_layer_norm(acc, cln_g, cln_b)).astype(BF16)

    su = zc_ref[:, 2 * cw:2 * cw + SGU_WIDTH]
    vn = _layer_norm(zc_ref[:, 2 * cw + SGU_WIDTH:2 * cw + 2 * SGU_WIDTH], sln_g, sln_b).astype(BF16)
    sgm = sgm_ref[...]
    for ch in range(TILE // SGU_CHUNK):
        rows = slice(ch * SGU_CHUNK, (ch + 1) * SGU_CHUNK)
        s = _dot(wcat_ref[...], _rep(vn[rows], SGU_GROUPS, sgm)) + bs_ref[...]
        out_ref[rows, cw:cw + SGU_WIDTH] = (su[rows] * s).astype(BF16)

    zoff = 2 * cw + 2 * SGU_WIDTH
    zcur = zc_ref[:, zoff:zoff + POOL_WIDTH]
    zw = jnp.concatenate([zp_ref[:, zoff:zoff + POOL_WIDTH], zcur, zn_ref[:, zoff:zoff + POOL_WIDTH]], axis=0)
    wlen = TILE + 2 * HALO
    t_loc = lax.broadcasted_iota(jnp.int32, (TILE, POOL_GROUPS * wlen), 0)
    col = lax.broadcasted_iota(jnp.int32, (TILE, POOL_GROUPS * wlen), 1)
    lo_seq = jnp.where(first, 0, -HALO)
    hi_seq = jnp.where(last, TILE, TILE + HALO)
    band = jnp.zeros((TILE, POOL_GROUPS * wlen), jnp.bool_)
    for gi, win in enumerate(POOL_WINDOWS):
        s_rel = col - gi * wlen - HALO
        lo = jnp.maximum(t_loc - win // 2, lo_seq)
        hi = jnp.minimum(t_loc + win - win // 2, hi_seq)
        band = band | ((col >= gi * wlen) & (col < (gi + 1) * wlen) & (s_rel >= lo) & (s_rel < hi))
    t_row = lax.broadcasted_iota(jnp.int32, (TILE, POOL_WIDTH), 0)
    lane_g = lax.broadcasted_iota(jnp.int32, (TILE, POOL_WIDTH), 1) // (POOL_WIDTH // POOL_GROUPS)
    cnt = jnp.zeros((TILE, POOL_WIDTH), F32)
    for gi, win in enumerate(POOL_WINDOWS):
        lo = jnp.maximum(t_row - win // 2, lo_seq)
        hi = jnp.minimum(t_row + win - win // 2, hi_seq)
        cnt = jnp.where(lane_g == gi, (hi - lo).astype(F32), cnt)
    plm = plm_ref[...]
    z_hi, z_lo = _split2(zw)
    band_b = band.astype(F32).astype(BF16)
    psum = _dot(band_b, _rep(z_hi, POOL_GROUPS, plm)) + _dot(band_b, _rep(z_lo, POOL_GROUPS, plm))
    p = psum / cnt - zcur
    pooled = _dot(p.astype(BF16), pw_ref[...]) * pool_scale
    out_ref[:, cw + SGU_WIDTH:cw + SGU_WIDTH + POOL_WIDTH] = pooled.astype(BF16)


def _mixers(zm, ctx_tiles, seq_tiles_ctx, seq_tiles_lat, dw, vecs, wcat, bs, pw_bd, sgm, plm):
    n = zm.shape[0]
    n_tiles = n // TILE
    per = TILE // HALO
    n_halo = n // HALO
    return pl.pallas_call(
        functools.partial(_mixers_kernel, ctx_tiles, seq_tiles_ctx, seq_tiles_lat),
        grid=(n_tiles,),
        in_specs=[
            pl.BlockSpec((TILE, MX_COLS), lambda i: (i, 0)),
            pl.BlockSpec((HALO, MX_COLS), lambda i: (jnp.maximum(i * per - 1, 0), 0)),
            pl.BlockSpec((HALO, MX_COLS), lambda i: (jnp.minimum((i + 1) * per, n_halo - 1), 0)),
            _const_spec(dw.shape),
            _const_spec(vecs.shape),
            _const_spec(wcat.shape),
            _const_spec(bs.shape),
            _const_spec(pw_bd.shape),
            _const_spec(sgm.shape),
            _const_spec(plm.shape),
        ],
        out_specs=pl.BlockSpec((TILE, MX_OUT), lambda i: (i, 0)),
        out_shape=jax.ShapeDtypeStruct((n, MX_OUT), BF16),
        scratch_shapes=[pltpu.VMEM((TILE + 2 * HALO, CONV_WIDTH), F32)],
        compiler_params=_cparams(("parallel",)),
        name="mixers",
    )(zm, zm, zm, dw, vecs, wcat, bs, pw_bd, sgm, plm)


def _route(logits_t):
    m = jnp.max(logits_t, axis=0, keepdims=True)
    e = jnp.exp(logits_t - m)
    p = e / jnp.sum(e, axis=0, keepdims=True)
    best_score = None
    best = None
    for g in range(N_EXPERT_GROUPS):
        rows = [p[g * EXPERTS_PER_GROUP + q:g * EXPERTS_PER_GROUP + q + 1] for q in range(EXPERTS_PER_GROUP)]
        score = None
        for a in range(EXPERTS_PER_GROUP):
            for b in range(a + 1, EXPERTS_PER_GROUP):
                pair = rows[a] + rows[b]
                score = pair if score is None else jnp.maximum(score, pair)
        if g == 0:
            best_score, best = score, jnp.zeros(score.shape, jnp.int32)
        else:
            upd = score > best_score
            best = jnp.where(upd, g, best)
            best_score = jnp.where(upd, score, best_score)
    eidx = lax.broadcasted_iota(jnp.int32, p.shape, 0)
    neg = -jnp.inf
    masked = jnp.where(eidx // EXPERTS_PER_GROUP == best, p, neg)
    m1 = jnp.max(masked, axis=0, keepdims=True)
    i1 = jnp.min(jnp.where(masked == m1, eidx, N_EXPERTS), axis=0, keepdims=True)
    masked2 = jnp.where(eidx == i1, neg, masked)
    m2 = jnp.max(masked2, axis=0, keepdims=True)
    i2 = jnp.min(jnp.where(masked2 == m2, eidx, N_EXPERTS), axis=0, keepdims=True)
    tot = m1 + m2
    return jnp.where(eidx == i1, m1 / tot, 0.0) + jnp.where(eidx == i2, m2 / tot, 0.0)


def _merge_kernel(x_ref, mod_ref, n1_ref, n2_ref, ya_ref, mx_ref, wg_ref, bg_ref, wro_ref, wco_ref, wso_ref,
                  wpo_ref, wout_ref, rw_ref, rb_ref, x1_ref, h2_ref, gates_ref):
    d = D_MODEL
    x = x_ref[...]
    hb = _norm_mod(x, n1_ref[...], mod_ref[:, 0:d], mod_ref[:, d:2 * d]).astype(BF16)
    cw = CONV_WIDTH
    branches = (
        (ya_ref[...], wro_ref),
        (mx_ref[:, 0:cw], wco_ref),
        (mx_ref[:, cw:cw + SGU_WIDTH], wso_ref),
        (mx_ref[:, cw + SGU_WIDTH:MX_OUT], wpo_ref),
    )
    merged = None
    for bi, (y_in, w_ref) in enumerate(branches):
        zg = _dot(hb, wg_ref[:, bi * d:(bi + 1) * d]) + bg_ref[:, bi * d:(bi + 1) * d]
        term = _sigmoid(zg) * _dot(y_in, w_ref[...])
        merged = term if merged is None else merged + term
    x1 = x + mod_ref[:, 2 * d:3 * d] * _dot(merged.astype(BF16), wout_ref[...])
    x1_ref[...] = x1
    h2 = _norm_mod(x1, n2_ref[...], mod_ref[:, 3 * d:4 * d], mod_ref[:, 4 * d:5 * d])
    h2_ref[...] = h2.astype(BF16)
    h_hi, h_lo = _split2(h2)
    r_hi, r_lo = _split2(rw_ref[...])
    logits_t = _dot_nt(r_hi, h_hi) + (_dot_nt(r_lo, h_hi) + _dot_nt(r_hi, h_lo)) + rb_ref[...]
    gates_ref[...] = _route(logits_t)


def _merge(x, mod_l, mod_row, n1, n2, ya, mx, wg, bg, wro, wco, wso, wpo, wout, rw_t, rb):
    n = x.shape[0]
    consts = (n1, n2)
    weights = (wg, bg, wro, wco, wso, wpo, wout, rw_t, rb)
    return pl.pallas_call(
        _merge_kernel,
        grid=(n // TILE,),
        in_specs=[
            pl.BlockSpec((TILE, D_MODEL), lambda i: (i, 0)),
            pl.BlockSpec((None, 1, 6 * D_MODEL), lambda i: (mod_row(i), 0, 0)),
            *[_const_spec(a.shape) for a in consts],
            pl.BlockSpec((TILE, RWKV_WIDTH), lambda i: (i, 0)),
            pl.BlockSpec((TILE, MX_OUT), lambda i: (i, 0)),
            *[_const_spec(a.shape) for a in weights],
        ],
        out_specs=[
            pl.BlockSpec((TILE, D_MODEL), lambda i: (i, 0)),
            pl.BlockSpec((TILE, D_MODEL), lambda i: (i, 0)),
            pl.BlockSpec((N_EXPERTS, TILE), lambda i: (0, i)),
        ],
        out_shape=[
            jax.ShapeDtypeStruct((n, D_MODEL), F32),
            jax.ShapeDtypeStruct((n, D_MODEL), BF16),
            jax.ShapeDtypeStruct((N_EXPERTS, n), F32),
        ],
        compiler_params=_cparams(("parallel",)),
        name="merge_router",
    )(x, mod_l, n1, n2, ya, mx, *weights)


MOE_TILE = 1024


def _moe_kernel(h_ref, gates_ref, x_ref, mod_ref, wg_ref, wu_ref, wd_ref, o_ref, acc_ref):
    e = pl.program_id(1)

    @pl.when(e == 0)
    def _():
        acc_ref[...] = jnp.zeros_like(acc_ref)

    h = h_ref[...]
    act = _silu(_dot(h, wg_ref[...])) * _dot(h, wu_ref[...])
    ff = _dot(act.astype(BF16), wd_ref[...])
    lane = lax.broadcasted_iota(jnp.int32, gates_ref.shape, 1)
    gate = jnp.sum(jnp.where(lane == e, gates_ref[...], 0.0), axis=1, keepdims=True)
    acc_ref[...] += gate * ff

    @pl.when(e == N_EXPERTS - 1)
    def _():
        o_ref[...] = x_ref[...] + mod_ref[:, 5 * D_MODEL:6 * D_MODEL] * acc_ref[...]


def _moe(h2, gates, x1, mod_l, mod_row_moe, wg, wu, wd):
    n = h2.shape[0]
    tm = MOE_TILE
    return pl.pallas_call(
        _moe_kernel,
        grid=(n // tm, N_EXPERTS),
        in_specs=[
            pl.BlockSpec((tm, D_MODEL), lambda i, e: (i, 0)),
            pl.BlockSpec((tm, N_EXPERTS), lambda i, e: (i, 0)),
            pl.BlockSpec((tm, D_MODEL), lambda i, e: (i, 0)),
            pl.BlockSpec((None, 1, 6 * D_MODEL), lambda i, e: (mod_row_moe(i), 0, 0)),
            pl.BlockSpec((None, D_MODEL, D_EXPERT), lambda i, e: (e, 0, 0)),
            pl.BlockSpec((None, D_MODEL, D_EXPERT), lambda i, e: (e, 0, 0)),
            pl.BlockSpec((None, D_EXPERT, D_MODEL), lambda i, e: (e, 0, 0)),
        ],
        out_specs=pl.BlockSpec((tm, D_MODEL), lambda i, e: (i, 0)),
        out_shape=jax.ShapeDtypeStruct((n, D_MODEL), F32),
        scratch_shapes=[pltpu.VMEM((tm, D_MODEL), F32)],
        compiler_params=_cparams(("parallel", "arbitrary")),
        name="moe",
    )(h2, gates, x1, mod_l, wg, wu, wd)


def _final_norm_kernel(x_ref, g_ref, o_ref):
    x = x_ref[...]
    o_ref[...] = x * lax.rsqrt(jnp.mean(x * x, axis=-1, keepdims=True) + NORM_EPS) * g_ref[...]


def _final_norm(x, g, tile0, n_tiles):
    return pl.pallas_call(
        _final_norm_kernel,
        grid=(n_tiles,),
        in_specs=[pl.BlockSpec((TILE, D_MODEL), lambda i: (i + tile0, 0)), _const_spec((1, D_MODEL))],
        out_specs=pl.BlockSpec((TILE, D_MODEL), lambda i: (i, 0)),
        out_shape=jax.ShapeDtypeStruct((n_tiles * TILE, D_MODEL), F32),
        compiler_params=_cparams(("parallel",)),
        name="final_norm",
    )(x, g)


def _grid_pos_embed(n_tokens):
    rows = n_tokens // GRID_W
    quarter = D_MODEL // 4
    half = D_MODEL // 2
    omega = 1.0 / (10000.0 ** (jnp.arange(quarter, dtype=F32) / quarter))
    ang_r = jnp.arange(rows, dtype=F32)[:, None] * omega
    ang_c = jnp.arange(GRID_W, dtype=F32)[:, None] * omega
    emb_r = jnp.concatenate([jnp.sin(ang_r), jnp.cos(ang_r)], axis=-1)
    emb_c = jnp.concatenate([jnp.sin(ang_c), jnp.cos(ang_c)], axis=-1)
    emb = jnp.concatenate([jnp.broadcast_to(emb_r[:, None, :], (rows, GRID_W, half)),
                           jnp.broadcast_to(emb_c[None, :, :], (rows, GRID_W, half))], axis=-1)
    return emb.reshape(rows * GRID_W, D_MODEL)


def _pad_rows(w, rows, offset):
    out = jnp.zeros(w.shape[:-2] + (rows, w.shape[-1]), w.dtype)
    return lax.dynamic_update_slice_in_dim(out, w, offset, axis=-2)


def _hg_cols(w):
    return jnp.moveaxis(w.reshape(w.shape[:-1] + (N_HG, HG_LANES)), -2, 0)


def kernel(x_prompt, x_sample, state_rwkv, c, c_ctx, norm1_g, norm2_g, w_mod, b_mod, w_in, b_in, rwkv_w0, rwkv_w_up,
           rwkv_a0, rwkv_a_up, rwkv_g_up, rwkv_k_k, rwkv_k_a, rwkv_r_k, rwkv_gn_g, rwkv_gn_b, rwkv_w_o, conv_dw,
           conv_dw_b, conv_ln_g, conv_ln_b, conv_w_o, sgu_ln_g, sgu_ln_b, sgu_w_s, sgu_b_s, sgu_w_o, pool_w,
           pool_scale, pool_w_o, w_out, moe_w_gate, moe_w_up, moe_w_down, router_w, router_b, final_norm_g):
    n_ctx, t_ctx, d = x_prompt.shape
    n_lat, t_lat, _ = x_sample.shape
    n_layers = w_in.shape[0]
    assert d == D_MODEL and t_ctx % TILE == 0 and t_lat % TILE == 0 and n_lat < MOD_ROWS
    ctx_rows, lat_rows = n_ctx * t_ctx, n_lat * t_lat
    assert ctx_rows % t_lat == 0 and ctx_rows % MOE_TILE == 0 and lat_rows % MOE_TILE == 0
    ctx_tiles = ctx_rows // TILE
    seq_tiles_ctx, seq_tiles_lat = t_ctx // TILE, t_lat // TILE

    def mod_row_for(tile_rows):
        ctx_t = ctx_rows // tile_rows
        per_lat = t_lat // tile_rows
        return lambda i: jnp.where(i < ctx_t, n_lat, jnp.maximum(i - ctx_t, 0) // per_lat)

    mod_row = mod_row_for(TILE)
    mod_row_moe = mod_row_for(MOE_TILE)

    cond = jnp.zeros((MOD_ROWS, d), F32).at[:n_lat].set(c).at[n_lat].set(c_ctx)
    mod = _modulation(cond, w_mod, b_mod).reshape(n_layers, MOD_ROWS, 1, 6 * d)

    x = _embed(x_prompt.reshape(ctx_rows, d), x_sample.reshape(lat_rows, d), _grid_pos_embed(t_lat), seq_tiles_lat)

    n_rw = RW_COLS
    n_mx = MX_COLS
    wa = w_in[:, :, :n_rw].astype(BF16)
    ba = b_in[:, None, :n_rw]
    wm = w_in[:, :, n_rw:n_rw + n_mx].astype(BF16)
    bm = b_in[:, None, n_rw:n_rw + n_mx]
    wgt = w_in[:, :, n_rw + n_mx:].astype(BF16)
    bgt = b_in[:, None, n_rw + n_mx:]

    par = jnp.zeros((n_layers, P_ROWS, RWKV_WIDTH), F32)
    par = par.at[:, P_W0:P_W0 + 2].set(rwkv_w0).at[:, P_A0:P_A0 + 2].set(rwkv_a0)
    par = par.at[:, P_KK].set(rwkv_k_k).at[:, P_KA].set(rwkv_k_a)
    par = par.at[:, P_RK].set(rwkv_r_k.reshape(n_layers, RWKV_WIDTH))
    par = par.at[:, P_GNG].set(rwkv_gn_g).at[:, P_GNB].set(rwkv_gn_b)
    par = jnp.moveaxis(_hg_cols(par), 0, 1)
    wup = jnp.moveaxis(_hg_cols(_pad_rows(rwkv_w_up, HG_LANES, 0)), 0, 2).astype(BF16)
    aup = jnp.moveaxis(_hg_cols(_pad_rows(rwkv_a_up, HG_LANES, RWKV_DECAY_RANK)), 0, 2).astype(BF16)
    gup = jnp.moveaxis(_hg_cols(_pad_rows(rwkv_g_up, HG_LANES, RWKV_DECAY_RANK + RWKV_ICLR_RANK)), 0, 1).astype(BF16)
    rw_consts = _rwkv_consts()

    mix_vecs = jnp.zeros((n_layers, 8, CONV_WIDTH), F32)
    for row, vec in enumerate((conv_dw_b, conv_ln_g, conv_ln_b, sgu_ln_g, sgu_ln_b, pool_scale)):
        mix_vecs = mix_vecs.at[:, row].set(vec)
    sgu_wcat = jnp.transpose(sgu_w_s, (0, 2, 1, 3)).reshape(n_layers, SGU_CHUNK, SGU_GROUPS * SGU_CHUNK).astype(BF16)
    sgu_bs = jnp.repeat(jnp.swapaxes(sgu_b_s, 1, 2), SGU_WIDTH // SGU_GROUPS, axis=2)
    pc = POOL_WIDTH // POOL_GROUPS
    pool_bd = (pool_w[:, :, :, None, :] * jnp.eye(POOL_GROUPS, dtype=F32)[None, :, None, :, None])
    pool_bd = pool_bd.reshape(n_layers, POOL_WIDTH, POOL_WIDTH).astype(BF16)
    lane_grp = jnp.arange(SGU_WIDTH)[None, :] // (SGU_WIDTH // SGU_GROUPS)
    sgm = (jnp.arange(SGU_GROUPS * SGU_CHUNK)[:, None] // SGU_CHUNK == lane_grp).astype(BF16)
    wlen = TILE + 2 * HALO
    plm = (jnp.arange(POOL_GROUPS * wlen)[:, None] // wlen == jnp.arange(POOL_WIDTH)[None, :] // pc).astype(BF16)

    wro, wco, wso, wpo, wout = (w.astype(BF16) for w in (rwkv_w_o, conv_w_o, sgu_w_o, pool_w_o, w_out))
    rw_t = router_w.T
    rb = router_b[:, None]
    moe_g, moe_u, moe_d = moe_w_gate.astype(BF16), moe_w_up.astype(BF16), moe_w_down.astype(BF16)

    ctx_s0 = jnp.zeros((n_ctx, N_HG, 2, HG_LANES, HG_LANES), F32)
    ctx_states = []
    for l in range(n_layers):
        zr, zm = _in_projection(x, mod[l], mod_row, norm1_g[l][None], wa[l], ba[l], wm[l], bm[l])
        rw_args = (par[l], wup[l], aup[l], gup[l], rw_consts)
        ya_ctx, s_fin = _rwkv_mixer(zr, ctx_s0, n_ctx, t_ctx, 0, *rw_args)
        ya_lat, _ = _rwkv_mixer(zr, _states_to_bd(state_rwkv[:, l]), n_lat, t_lat, ctx_rows // t_lat, *rw_args)
        ctx_states.append(_bd_to_states(s_fin))
        ya = jnp.concatenate([ya_ctx, ya_lat], axis=0)
        mx = _mixers(zm, ctx_tiles, seq_tiles_ctx, seq_tiles_lat, conv_dw[l], mix_vecs[l], sgu_wcat[l], sgu_bs[l],
                     pool_bd[l], sgm, plm)
        x1, h2, gates_t = _merge(x, mod[l], mod_row, norm1_g[l][None], norm2_g[l][None], ya, mx, wgt[l], bgt[l],
                                 wro[l], wco[l], wso[l], wpo[l], wout[l], rw_t, rb)
        x = _moe(h2, gates_t.T, x1, mod[l], mod_row_moe, moe_g[l], moe_u[l], moe_d[l])

    g_fin = final_norm_g[None]
    y_prompt = _final_norm(x, g_fin, 0, ctx_tiles).reshape(n_ctx, t_ctx, d)
    y_sample = _final_norm(x, g_fin, ctx_tiles, lat_rows // TILE).reshape(n_lat, t_lat, d)
    new_state = jnp.stack(ctx_states, axis=1).astype(x_prompt.dtype)
    return (y_prompt, y_sample, new_state)
```

```python
import functools
import math

import jax
import jax.numpy as jnp
from jax import lax
from jax.experimental import pallas as pl
from jax.experimental.pallas import tpu as pltpu

F32 = jnp.float32
BF16 = jnp.bfloat16

D_MODEL = 1024
GRID_W = 64
RWKV_HEADS = 8
RWKV_HEAD_DIM = 64
RWKV_WIDTH = RWKV_HEADS * RWKV_HEAD_DIM
RWKV_DECAY_RANK = 64
RWKV_ICLR_RANK = 64
RWKV_GATE_RANK = 128
RWKV_GN_EPS = 64e-5
CONV_WIDTH = 256
CONV_KERNEL = 31
SGU_GROUPS = 4
SGU_WIDTH = 256
SGU_CHUNK = 128
POOL_GROUPS = 4
POOL_WIDTH = 256
POOL_WINDOWS = (2, 4, 8, 16)
N_BRANCHES = 4
N_EXPERTS = 16
N_EXPERT_GROUPS = 4
EXPERTS_PER_GROUP = N_EXPERTS // N_EXPERT_GROUPS
D_EXPERT = 512
NORM_EPS = 1e-6
LN_EPS = 1e-5

RW_COLS = 3 * RWKV_WIDTH + RWKV_DECAY_RANK + RWKV_ICLR_RANK + RWKV_GATE_RANK
MX_COLS = 2 * CONV_WIDTH + 2 * SGU_WIDTH + POOL_WIDTH
MX_OUT = CONV_WIDTH + SGU_WIDTH + POOL_WIDTH

TILE = 256
HALO = 16
HG_LANES = 256
HEADS_PER_HG = HG_LANES // RWKV_HEAD_DIM
N_HG = RWKV_WIDTH // HG_LANES
SCAN_CHUNK = 64
assert SCAN_CHUNK == RWKV_HEAD_DIM
CHUNKS_PER_TRIP = 2
DECAY_SCALE = math.exp(-0.5)
MOD_ROWS = 16
ROUTER_LANES = 128

VMEM_LIMIT = 48 * 1024 * 1024
RWKV_VMEM_LIMIT = 56 * 1024 * 1024


def _cparams(sem, vmem_limit=VMEM_LIMIT):
    return pltpu.CompilerParams(dimension_semantics=sem, vmem_limit_bytes=vmem_limit)


def _dot(a, b):
    return jnp.dot(a, b, preferred_element_type=F32)


def _split2(x):
    hi = x.astype(BF16)
    lo = (x - hi.astype(F32)).astype(BF16)
    return hi, lo


def _dot3(a, b):
    a_hi, a_lo = _split2(a)
    b_hi, b_lo = _split2(b)
    return _dot(a_hi, b_hi) + (_dot(a_lo, b_hi) + _dot(a_hi, b_lo))


def _dot_exact_rhs(x, rhs):
    hi, lo = _split2(x)
    return _dot(hi, rhs) + _dot(lo, rhs)


def _dot_exact_lhs(lhs, x):
    hi, lo = _split2(x)
    return _dot(lhs, hi) + _dot(lhs, lo)


def _sigmoid(x):
    return 1.0 / (1.0 + jnp.exp(-x))


def _silu(x):
    return x * _sigmoid(x)


def _norm_mod(x, g, shift, scale):
    y = x * lax.rsqrt(jnp.mean(x * x, axis=-1, keepdims=True) + NORM_EPS) * g
    return y * (1.0 + scale) + shift


def _layer_norm(x, g, b):
    mu = jnp.mean(x, axis=-1, keepdims=True)
    xc = x - mu
    var = jnp.mean(xc * xc, axis=-1, keepdims=True)
    return xc * lax.rsqrt(var + LN_EPS) * g + b


def _mod_kernel(c_ref, w_ref, b_ref, o_ref):
    o_ref[...] = _dot3(_silu(c_ref[...]), w_ref[...]) + b_ref[...]


def _modulation(cond, w_mod, b_mod):
    n_layers = w_mod.shape[0]
    tn = 1536
    return pl.pallas_call(
        _mod_kernel,
        grid=(n_layers, 6 * D_MODEL // tn),
        in_specs=[
            pl.BlockSpec((MOD_ROWS, D_MODEL), lambda l, j: (0, 0)),
            pl.BlockSpec((None, D_MODEL, tn), lambda l, j: (l, 0, j)),
            pl.BlockSpec((None, 1, tn), lambda l, j: (l, 0, j)),
        ],
        out_specs=pl.BlockSpec((None, MOD_ROWS, tn), lambda l, j: (l, 0, j)),
        out_shape=jax.ShapeDtypeStruct((n_layers, MOD_ROWS, 6 * D_MODEL), F32),
        compiler_params=_cparams(("parallel", "parallel")),
        name="modulation",
    )(cond, w_mod, b_mod.reshape(n_layers, 1, 6 * D_MODEL))


def _embed_kernel(n_ctx_tiles, xp_ref, xs_ref, pos_ref, o_ref):
    i = pl.program_id(0)

    @pl.when(i < n_ctx_tiles)
    def _():
        o_ref[...] = xp_ref[...]

    @pl.when(i >= n_ctx_tiles)
    def _():
        o_ref[...] = xs_ref[...] + pos_ref[...]


def _embed(xp, xs, pos, lat_tiles):
    n_ctx_tiles = xp.shape[0] // TILE
    n_lat_tiles = xs.shape[0] // TILE
    n_tiles = n_ctx_tiles + n_lat_tiles
    return pl.pallas_call(
        functools.partial(_embed_kernel, n_ctx_tiles),
        grid=(n_tiles,),
        in_specs=[
            pl.BlockSpec((TILE, D_MODEL), lambda i: (jnp.minimum(i, n_ctx_tiles - 1), 0)),
            pl.BlockSpec((TILE, D_MODEL), lambda i: (jnp.maximum(i - n_ctx_tiles, 0), 0)),
            pl.BlockSpec((TILE, D_MODEL), lambda i: (jnp.maximum(i - n_ctx_tiles, 0) % lat_tiles, 0)),
        ],
        out_specs=pl.BlockSpec((TILE, D_MODEL), lambda i: (i, 0)),
        out_shape=jax.ShapeDtypeStruct((n_tiles * TILE, D_MODEL), F32),
        compiler_params=_cparams(("parallel",)),
        name="embed",
    )(xp, xs, pos)


def _inproj_kernel(x_ref, mod_ref, g_ref, wa_ref, ba_ref, wm_ref, bm_ref, zr_ref, zm_ref):
    h = _norm_mod(x_ref[...], g_ref[...], mod_ref[:, 0:D_MODEL], mod_ref[:, D_MODEL:2 * D_MODEL])
    hb = h.astype(BF16)
    zr_ref[...] = _dot(hb, wa_ref[...]) + ba_ref[...]
    zm_ref[...] = _dot(hb, wm_ref[...]) + bm_ref[...]


def _const_spec(shape):
    nd = len(shape)
    return pl.BlockSpec(shape, lambda *_: (0,) * nd)


def _in_projection(x, mod_l, mod_row, norm_g, wa, ba, wm, bm):
    n = x.shape[0]
    return pl.pallas_call(
        _inproj_kernel,
        grid=(n // TILE,),
        in_specs=[
            pl.BlockSpec((TILE, D_MODEL), lambda i: (i, 0)),
            pl.BlockSpec((None, 1, 6 * D_MODEL), lambda i: (mod_row(i), 0, 0)),
            _const_spec((1, D_MODEL)),
            _const_spec((D_MODEL, RW_COLS)),
            _const_spec((1, RW_COLS)),
            _const_spec((D_MODEL, MX_COLS)),
            _const_spec((1, MX_COLS)),
        ],
        out_specs=[
            pl.BlockSpec((TILE, RW_COLS), lambda i: (i, 0)),
            pl.BlockSpec((TILE, MX_COLS), lambda i: (i, 0)),
        ],
        out_shape=[
            jax.ShapeDtypeStruct((n, RW_COLS), F32),
            jax.ShapeDtypeStruct((n, MX_COLS), F32),
        ],
        compiler_params=_cparams(("parallel",)),
        name="in_projection",
    )(x, mod_l, norm_g, wa, ba, wm, bm)


P_W0, P_A0, P_KK, P_KA, P_RK, P_GNG, P_GNB = 0, 2, 4, 5, 6, 7, 8
P_ROWS = 16


def _rep(xb, reps, mask):
    return jnp.concatenate([xb] * reps, axis=0) * mask


def _rwkv_kernel(t_len, r_ref, k_ref, v_ref, lr_ref, s0_ref, par_ref, wup_ref, aup_ref, gup_ref,
                 bd_ref, cm_ref, tri_ref, eyec_ref, out_ref, sfin_ref, y_scr, st_scr):
    c = SCAN_CHUNK
    n_chunks = t_len // c
    n_trips = n_chunks // CHUNKS_PER_TRIP
    lr_k = RWKV_DECAY_RANK + RWKV_ICLR_RANK
    probs = [(d, sl, g) for sl in range(CHUNKS_PER_TRIP) for d in (0, 1) for g in range(N_HG)]
    n_sq = int(math.log2(c))

    def lanes(g):
        return slice(g * HG_LANES, (g + 1) * HG_LANES)

    def rep(xb):
        return _rep(xb, HEADS_PER_HG, bd_ref[...])

    def seg_sums(xs, passes):
        n = xs[0].shape[0]
        if passes == 1:
            out = _dot(jnp.concatenate([x.astype(BF16) for x in xs], axis=0), bd_ref[...])
            return [out[i * n:(i + 1) * n] for i in range(len(xs))]
        parts = []
        for x in xs:
            parts.extend(_split2(x))
        out = _dot(jnp.concatenate(parts, axis=0), bd_ref[...])
        return [out[2 * i * n:(2 * i + 1) * n] + out[(2 * i + 1) * n:(2 * i + 2) * n] for i in range(len(xs))]

    def fold(x_bd):
        n = x_bd.shape[0] // HEADS_PER_HG
        return (x_bd[0:n] + x_bd[n:2 * n]) + (x_bd[2 * n:3 * n] + x_bd[3 * n:4 * n])

    def par(g, row):
        return par_ref[g, row:row + 1, :]

    st_scr[...] = s0_ref[...]

    def trip(j, first_touch):
        rows = {}
        for sl in range(CHUNKS_PER_TRIP):
            rows[0, sl] = pl.ds(pl.multiple_of((CHUNKS_PER_TRIP * j + sl) * c, c), c)
            rows[1, sl] = pl.ds(pl.multiple_of((n_chunks - 1 - CHUNKS_PER_TRIP * j - sl) * c, c), c)
        lr = {key: lr_ref[rw, 0:lr_k] for key, rw in rows.items()}
        th = {key: jnp.tanh(x).astype(BF16) for key, x in lr.items()}
        lrb = {key: x.astype(BF16) for key, x in lr.items()}
        ss = [dict(r=r_ref[rows[d, sl], lanes(g)], k=k_ref[rows[d, sl], lanes(g)], v=v_ref[rows[d, sl], lanes(g)])
              for d, sl, g in probs]

        for (d, sl, g), s in zip(probs, ss):
            s["w_pre"] = par(g, P_W0 + d) + _dot(th[d, sl], wup_ref[d, g])
            s["a_pre"] = par(g, P_A0 + d) + _dot(lrb[d, sl], aup_ref[d, g])
            s["kkr"] = s["k"] * par(g, P_KK)
        for s, ssq in zip(ss, seg_sums([s["kkr"] * s["kkr"] for s in ss], 1)):
            s["ssq"] = ssq
        for (d, sl, g), s in zip(probs, ss):
            s["lw"] = -DECAY_SCALE * _sigmoid(s["w_pre"])
            s["cum"] = _dot_exact_lhs(tri_ref[d], s["lw"])
        for (d, sl, g), s in zip(probs, ss):
            a = _sigmoid(s["a_pre"])
            kk = s["kkr"] / jnp.maximum(jnp.sqrt(s["ssq"]), 1e-12)
            kd = s["k"] * (1.0 + (a - 1.0) * par(g, P_KA))
            cum = s["cum"]
            e_incl = jnp.exp(cum)
            e_neg = jnp.exp(-cum)
            gam = e_incl[c - 1:c, :] if d == 0 else e_incl[0:1, :]
            s["at"] = (-kk * jnp.exp(cum - s["lw"])).astype(BF16)
            s["rt"] = s["r"] * e_incl
            bt = kk * a * e_neg
            kt = kd * e_neg
            s["vb"] = s["v"].astype(BF16)
            s["bonus"] = s["r"] * kd * par(g, P_RK)
            s["gam_rows"] = _dot_exact_rhs(eyec_ref[...] * gam, bd_ref[...])
            tr = jnp.concatenate([bt, kt], axis=0).T
            swapped = pltpu.roll(tr, c, axis=1)
            first_half = lax.broadcasted_iota(jnp.int32, tr.shape, 1) < c
            bt_t = jnp.where(first_half, tr, swapped).astype(BF16)
            kt_t = jnp.where(first_half, swapped, tr).astype(BF16)
            s["bt_bd"] = jnp.concatenate([bt_t, bt_t], axis=1) * bd_ref[...]
            s["kt_bd"] = jnp.concatenate([kt_t, kt_t], axis=1) * bd_ref[...]
            lhs2 = jnp.concatenate([s["at"], s["rt"].astype(BF16)], axis=0)
            s["m_b"] = _dot(lhs2, s["bt_bd"])
            s["m_k"] = _dot(lhs2, s["kt_bd"])
        for s, bsum in zip(ss, seg_sums([s["bonus"] for s in ss], 1)):
            s["bsum"] = bsum
        for (d, sl, g), s in zip(probs, ss):
            causal = cm_ref[d]
            m_b = s["m_b"] * causal
            s["x"] = m_b[0:c]
            s["a_rb"] = m_b[c:2 * c].astype(BF16)
            s["t"] = eyec_ref[...] + s["x"]
            s["bt_c"] = fold(s["bt_bd"])
            av = _dot(jnp.concatenate([(s["m_k"] * causal).astype(BF16), fold(s["kt_bd"])], axis=0), rep(s["vb"]))
            s["g1"] = av[0:c].astype(BF16)
            s["y0"] = av[c:2 * c] + s["bsum"] * s["v"]
            s["z"] = av[2 * c:2 * c + RWKV_HEAD_DIM]

        for step in range(1, n_sq):
            for s in ss:
                x_bd = rep(s["x"].astype(BF16))
                if step == 1:
                    s["x"] = _dot(s["x"].astype(BF16), x_bd)
                else:
                    both = _dot(jnp.concatenate([s["x"], s["t"]], axis=0).astype(BF16), x_bd)
                    s["x"], s["t"] = both[0:c], s["t"] + both[c:2 * c]
        for s in ss:
            s["t"] = (s["t"] + _dot(s["t"].astype(BF16), rep(s["x"].astype(BF16)))).astype(BF16)

        for s in ss:
            s["w"] = _dot(s["t"], rep(s["at"])).astype(BF16)
            s["u0"] = _dot(s["t"], rep(s["g1"])).astype(BF16)
        for s in ss:
            lhs = jnp.concatenate([s["a_rb"], s["bt_c"]], axis=0)
            with_w = _dot(lhs, rep(s["w"]))
            with_u0 = _dot(lhs, rep(s["u0"]))
            s["q"] = (s["rt"] + with_w[0:c]).astype(BF16)
            s["p"] = (eyec_ref[...] + with_w[c:c + RWKV_HEAD_DIM]).astype(BF16)
            s["y0"] = s["y0"] + with_u0[0:c]
            s["z"] = s["z"] + with_u0[c:c + RWKV_HEAD_DIM]

        for (d, sl, g), s in zip(probs, ss):
            both = _dot(jnp.concatenate([s["p"], s["q"]], axis=0), rep(st_scr[g, d].astype(BF16)))
            st_scr[g, d] = s["gam_rows"] * (both[0:RWKV_HEAD_DIM] + s["z"])
            y = both[RWKV_HEAD_DIM:RWKV_HEAD_DIM + c] + s["y0"]
            if first_touch:
                y_scr[rows[d, sl], lanes(g)] = y
            else:
                y_scr[rows[d, sl], lanes(g)] += y

    half = n_trips // 2
    lax.fori_loop(0, half, lambda j, carry: (trip(j, True), carry)[1], 0)
    lax.fori_loop(half, n_trips, lambda j, carry: (trip(j, False), carry)[1], 0)
    sfin_ref[...] = st_scr[...]

    inv_n = 1.0 / RWKV_HEAD_DIM

    def finish(j, carry):
        rows = pl.ds(pl.multiple_of(j * TILE, TILE), TILE)
        sg = _sigmoid(lr_ref[rows, lr_k:HG_LANES]).astype(BF16)
        ys = [y_scr[rows, lanes(g)] for g in range(N_HG)]
        mus = [m * inv_n for m in seg_sums(ys, 2)]
        ycs = [y - mu for y, mu in zip(ys, mus)]
        variances = [v * inv_n for v in seg_sums([yc * yc for yc in ycs], 2)]
        for g in range(N_HG):
            yn = ycs[g] * lax.rsqrt(variances[g] + RWKV_GN_EPS) * par(g, P_GNG) + par(g, P_GNB)
            out_ref[rows, lanes(g)] = (yn * _dot(sg, gup_ref[g])).astype(BF16)
        return carry

    lax.fori_loop(0, t_len // TILE, finish, 0)


def _rwkv_mixer(zr, s0, n_seq, t_len, row_off, par, wup, aup, gup, consts):
    lr_block = 3 * RWKV_WIDTH // HG_LANES
    state_spec = pl.BlockSpec((None, N_HG, 2, RWKV_HEAD_DIM, HG_LANES), lambda b: (b, 0, 0, 0, 0))
    weights = (par, wup, aup, gup)
    return pl.pallas_call(
        functools.partial(_rwkv_kernel, t_len),
        grid=(n_seq,),
        in_specs=[
            pl.BlockSpec((t_len, RWKV_WIDTH), lambda b: (b + row_off, 0)),
            pl.BlockSpec((t_len, RWKV_WIDTH), lambda b: (b + row_off, 1)),
            pl.BlockSpec((t_len, RWKV_WIDTH), lambda b: (b + row_off, 2)),
            pl.BlockSpec((t_len, HG_LANES), lambda b: (b + row_off, lr_block)),
            state_spec,
            *[_const_spec(a.shape) for a in weights],
            *[_const_spec(a.shape) for a in consts],
        ],
        out_specs=[pl.BlockSpec((t_len, RWKV_WIDTH), lambda b: (b, 0)), state_spec],
        out_shape=[
            jax.ShapeDtypeStruct((n_seq * t_len, RWKV_WIDTH), BF16),
            jax.ShapeDtypeStruct((n_seq, N_HG, 2, RWKV_HEAD_DIM, HG_LANES), F32),
        ],
        scratch_shapes=[
            pltpu.VMEM((t_len, RWKV_WIDTH), F32),
            pltpu.VMEM((N_HG, 2, RWKV_HEAD_DIM, HG_LANES), F32),
        ],
        compiler_params=_cparams(("parallel",), RWKV_VMEM_LIMIT),
        name="rwkv_mixer",
    )(zr, zr, zr, zr, s0, *weights, *consts)


def _rwkv_consts():
    c = SCAN_CHUNK
    ri = jnp.arange(HG_LANES)[:, None]
    ci = jnp.arange(HG_LANES)[None, :]
    bd = (ri // RWKV_HEAD_DIM == ci // RWKV_HEAD_DIM).astype(BF16)
    t = jnp.arange(c)[:, None]
    i = (jnp.arange(HEADS_PER_HG * c) % c)[None, :]
    cm = jnp.stack([jnp.concatenate([i < t, i <= t]), jnp.concatenate([i > t, i >= t])]).astype(F32)
    ti = jnp.arange(c)[None, :]
    tri = jnp.stack([ti <= t, ti >= t]).astype(BF16)
    eyec = (i == t).astype(F32)
    return bd, cm, tri, eyec


def _states_to_kernel(s):
    b = s.shape[0]
    st = s.reshape(b, 2, N_HG, HEADS_PER_HG, RWKV_HEAD_DIM, RWKV_HEAD_DIM)
    return jnp.transpose(st, (0, 2, 1, 5, 3, 4)).reshape(b, N_HG, 2, RWKV_HEAD_DIM, HG_LANES)


def _states_from_kernel(sk):
    b = sk.shape[0]
    st = sk.reshape(b, N_HG, 2, RWKV_HEAD_DIM, HEADS_PER_HG, RWKV_HEAD_DIM)
    return jnp.transpose(st, (0, 2, 1, 4, 5, 3)).reshape(b, 2, RWKV_HEADS, RWKV_HEAD_DIM, RWKV_HEAD_DIM)


def _mixers_kernel(ctx_tiles, seq_tiles_ctx, seq_tiles_lat, zc_ref, zp_ref, zn_ref, dw_ref, vec_ref, wcat_ref,
                   bs_ref, pw_ref, sgm_ref, plm_ref, out_ref, win_scr):
    i = pl.program_id(0)
    j_ctx = i % seq_tiles_ctx
    j_lat = jnp.maximum(i - ctx_tiles, 0) % seq_tiles_lat
    is_ctx = i < ctx_tiles
    first = jnp.where(is_ctx, j_ctx == 0, j_lat == 0)
    last = jnp.where(is_ctx, j_ctx == seq_tiles_ctx - 1, j_lat == seq_tiles_lat - 1)
    keep_prev = jnp.where(first, 0.0, 1.0)
    keep_next = jnp.where(last, 0.0, 1.0)

    cw = CONV_WIDTH
    conv_b, cln_g, cln_b = vec_ref[0:1, :], vec_ref[1:2, :], vec_ref[2:3, :]
    sln_g, sln_b, pool_scale = vec_ref[3:4, :], vec_ref[4:5, :], vec_ref[5:6, :]

    glu = lambda z: z[:, 0:cw] * _sigmoid(z[:, cw:2 * cw])
    win_scr[0:HALO, :] = glu(zp_ref[...]) * keep_prev
    win_scr[HALO:HALO + TILE, :] = glu(zc_ref[...])
    win_scr[HALO + TILE:2 * HALO + TILE, :] = glu(zn_ref[...]) * keep_next
    acc = jnp.zeros((TILE, cw), F32) + conv_b
    pad = CONV_KERNEL // 2
    for j in range(CONV_KERNEL):
        acc = acc + win_scr[pl.ds(HALO - pad + j, TILE), :] * dw_ref[j:j + 1, :]
    out_ref[:, 0:cw] = _silu(_layer_norm(acc, cln_g, cln_b)).astype(BF16)

    su = zc_ref[:, 2 * cw:2 * cw + SGU_WIDTH]
    vn = _layer_norm(zc_ref[:, 2 * cw + SGU_WIDTH:2 * cw + 2 * SGU_WIDTH], sln_g, sln_b).astype(BF16)
    sgm = sgm_ref[...]
    for ch in range(TILE // SGU_CHUNK):
        rows = slice(ch * SGU_CHUNK, (ch + 1) * SGU_CHUNK)
        s = _dot(wcat_ref[...], _rep(vn[rows], SGU_GROUPS, sgm)) + bs_ref[...]
        out_ref[rows, cw:cw + SGU_WIDTH] = (su[rows] * s).astype(BF16)

    zoff = 2 * cw + 2 * SGU_WIDTH
    zcur = zc_ref[:, zoff:zoff + POOL_WIDTH]
    zw = jnp.concatenate([zp_ref[:, zoff:zoff + POOL_WIDTH], zcur, zn_ref[:, zoff:zoff + POOL_WIDTH]], axis=0)
    wlen = TILE + 2 * HALO
    t_loc = lax.broadcasted_iota(jnp.int32, (TILE, POOL_GROUPS * wlen), 0)
    col = lax.broadcasted_iota(jnp.int32, (TILE, POOL_GROUPS * wlen), 1)
    lo_seq = jnp.where(first, 0, -HALO)
    hi_seq = jnp.where(last, TILE, TILE + HALO)
    band = jnp.zeros((TILE, POOL_GROUPS * wlen), jnp.bool_)
    for gi, win in enumerate(POOL_WINDOWS):
        s_rel = col - gi * wlen - HALO
        lo = jnp.maximum(t_loc - win // 2, lo_seq)
        hi = jnp.minimum(t_loc + win - win // 2, hi_seq)
        band = band | ((col >= gi * wlen) & (col < (gi + 1) * wlen) & (s_rel >= lo) & (s_rel < hi))
    t_row = lax.broadcasted_iota(jnp.int32, (TILE, POOL_WIDTH), 0)
    lane_g = lax.broadcasted_iota(jnp.int32, (TILE, POOL_WIDTH), 1) // (POOL_WIDTH // POOL_GROUPS)
    cnt = jnp.zeros((TILE, POOL_WIDTH), F32)
    for gi, win in enumerate(POOL_WINDOWS):
        lo = jnp.maximum(t_row - win // 2, lo_seq)
        hi = jnp.minimum(t_row + win - win // 2, hi_seq)
        cnt = jnp.where(lane_g == gi, (hi - lo).astype(F32), cnt)
    plm = plm_ref[...]
    z_hi, z_lo = _split2(zw)
    band_b = band.astype(F32).astype(BF16)
    psum = _dot(band_b, _rep(z_hi, POOL_GROUPS, plm)) + _dot(band_b, _rep(z_lo, POOL_GROUPS, plm))
    p = psum / cnt - zcur
    pooled = _dot(p.astype(BF16), pw_ref[...]) * pool_scale
    out_ref[:, cw + SGU_WIDTH:cw + SGU_WIDTH + POOL_WIDTH] = pooled.astype(BF16)


def _mixers(zm, ctx_tiles, seq_tiles_ctx, seq_tiles_lat, dw, vecs, wcat, bs, pw_bd, sgm, plm):
    n = zm.shape[0]
    n_tiles = n // TILE
    per = TILE // HALO
    n_halo = n // HALO
    return pl.pallas_call(
        functools.partial(_mixers_kernel, ctx_tiles, seq_tiles_ctx, seq_tiles_lat),
        grid=(n_tiles,),
        in_specs=[
            pl.BlockSpec((TILE, MX_COLS), lambda i: (i, 0)),
            pl.BlockSpec((HALO, MX_COLS), lambda i: (jnp.maximum(i * per - 1, 0), 0)),
            pl.BlockSpec((HALO, MX_COLS), lambda i: (jnp.minimum((i + 1) * per, n_halo - 1), 0)),
            _const_spec(dw.shape),
            _const_spec(vecs.shape),
            _const_spec(wcat.shape),
            _const_spec(bs.shape),
            _const_spec(pw_bd.shape),
            _const_spec(sgm.shape),
            _const_spec(plm.shape),
        ],
        out_specs=pl.BlockSpec((TILE, MX_OUT), lambda i: (i, 0)),
        out_shape=jax.ShapeDtypeStruct((n, MX_OUT), BF16),
        scratch_shapes=[pltpu.VMEM((TILE + 2 * HALO, CONV_WIDTH), F32)],
        compiler_params=_cparams(("parallel",)),
        name="mixers",
    )(zm, zm, zm, dw, vecs, wcat, bs, pw_bd, sgm, plm)


def _route(logits_t):
    m = jnp.max(logits_t, axis=0, keepdims=True)
    e = jnp.exp(logits_t - m)
    p = e / jnp.sum(e, axis=0, keepdims=True)
    best_score = None
    best = None
    for g in range(N_EXPERT_GROUPS):
        rows = [p[g * EXPERTS_PER_GROUP + q:g * EXPERTS_PER_GROUP + q + 1] for q in range(EXPERTS_PER_GROUP)]
        score = None
        for a in range(EXPERTS_PER_GROUP):
            for b in range(a + 1, EXPERTS_PER_GROUP):
                pair = rows[a] + rows[b]
                score = pair if score is None else jnp.maximum(score, pair)
        if g == 0:
            best_score, best = score, jnp.zeros(score.shape, jnp.int32)
        else:
            upd = score > best_score
            best = jnp.where(upd, g, best)
            best_score = jnp.where(upd, score, best_score)
    eidx = lax.broadcasted_iota(jnp.int32, p.shape, 0)
    neg = -jnp.inf
    masked = jnp.where(eidx // EXPERTS_PER_GROUP == best, p, neg)
    m1 = jnp.max(masked, axis=0, keepdims=True)
    i1 = jnp.min(jnp.where(masked == m1, eidx, N_EXPERTS), axis=0, keepdims=True)
    masked2 = jnp.where(eidx == i1, neg, masked)
    m2 = jnp.max(masked2, axis=0, keepdims=True)
    i2 = jnp.min(jnp.where(masked2 == m2, eidx, N_EXPERTS), axis=0, keepdims=True)
    tot = m1 + m2
    return jnp.where(eidx == i1, m1 / tot, 0.0) + jnp.where(eidx == i2, m2 / tot, 0.0)


def _merge_kernel(x_ref, mod_ref, n1_ref, n2_ref, ya_ref, mx_ref, wg_ref, bg_ref, wro_ref, wco_ref, wso_ref,
                  wpo_ref, wout_ref, rw_ref, rb_ref, x1_ref, h2_ref, gates_ref):
    d = D_MODEL
    x = x_ref[...]
    hb = _norm_mod(x, n1_ref[...], mod_ref[:, 0:d], mod_ref[:, d:2 * d]).astype(BF16)
    cw = CONV_WIDTH
    branches = (
        (ya_ref[...], wro_ref),
        (mx_ref[:, 0:cw], wco_ref),
        (mx_ref[:, cw:cw + SGU_WIDTH], wso_ref),
        (mx_ref[:, cw + SGU_WIDTH:MX_OUT], wpo_ref),
    )
    merged = None
    for bi, (y_in, w_ref) in enumerate(branches):
        zg = _dot(hb, wg_ref[:, bi * d:(bi + 1) * d]) + bg_ref[:, bi * d:(bi + 1) * d]
        term = _sigmoid(zg) * _dot(y_in, w_ref[...])
        merged = term if merged is None else merged + term
    x1 = x + mod_ref[:, 2 * d:3 * d] * _dot(merged.astype(BF16), wout_ref[...])
    x1_ref[...] = x1
    h2 = _norm_mod(x1, n2_ref[...], mod_ref[:, 3 * d:4 * d], mod_ref[:, 4 * d:5 * d])
    h2_ref[...] = h2.astype(BF16)
    logits = _dot3(h2, rw_ref[...])
    gates_ref[...] = _route(logits.T[0:N_EXPERTS] + rb_ref[...])


def _merge(x, mod_l, mod_row, n1, n2, ya, mx, wg, bg, wro, wco, wso, wpo, wout, rw_t, rb):
    n = x.shape[0]
    consts = (n1, n2)
    weights = (wg, bg, wro, wco, wso, wpo, wout, rw_t, rb)
    return pl.pallas_call(
        _merge_kernel,
        grid=(n // TILE,),
        in_specs=[
            pl.BlockSpec((TILE, D_MODEL), lambda i: (i, 0)),
            pl.BlockSpec((None, 1, 6 * D_MODEL), lambda i: (mod_row(i), 0, 0)),
            *[_const_spec(a.shape) for a in consts],
            pl.BlockSpec((TILE, RWKV_WIDTH), lambda i: (i, 0)),
            pl.BlockSpec((TILE, MX_OUT), lambda i: (i, 0)),
            *[_const_spec(a.shape) for a in weights],
        ],
        out_specs=[
            pl.BlockSpec((TILE, D_MODEL), lambda i: (i, 0)),
            pl.BlockSpec((TILE, D_MODEL), lambda i: (i, 0)),
            pl.BlockSpec((N_EXPERTS, TILE), lambda i: (0, i)),
        ],
        out_shape=[
            jax.ShapeDtypeStruct((n, D_MODEL), F32),
            jax.ShapeDtypeStruct((n, D_MODEL), BF16),
            jax.ShapeDtypeStruct((N_EXPERTS, n), F32),
        ],
        compiler_params=_cparams(("parallel",)),
        name="merge_router",
    )(x, mod_l, n1, n2, ya, mx, *weights)


MOE_TILE = 1024


def _moe_kernel(h_ref, gates_ref, x_ref, mod_ref, wg_ref, wu_ref, wd_ref, o_ref, acc_ref):
    e = pl.program_id(1)

    @pl.when(e == 0)
    def _():
        acc_ref[...] = jnp.zeros_like(acc_ref)

    h = h_ref[...]
    act = _silu(_dot(h, wg_ref[...])) * _dot(h, wu_ref[...])
    ff = _dot(act.astype(BF16), wd_ref[...])
    lane = lax.broadcasted_iota(jnp.int32, gates_ref.shape, 1)
    gate = jnp.sum(jnp.where(lane == e, gates_ref[...], 0.0), axis=1, keepdims=True)
    acc_ref[...] += gate * ff

    @pl.when(e == N_EXPERTS - 1)
    def _():
        o_ref[...] = x_ref[...] + mod_ref[:, 5 * D_MODEL:6 * D_MODEL] * acc_ref[...]


def _moe(h2, gates, x1, mod_l, mod_row_moe, wg, wu, wd):
    n = h2.shape[0]
    tm = MOE_TILE
    return pl.pallas_call(
        _moe_kernel,
        grid=(n // tm, N_EXPERTS),
        in_specs=[
            pl.BlockSpec((tm, D_MODEL), lambda i, e: (i, 0)),
            pl.BlockSpec((tm, N_EXPERTS), lambda i, e: (i, 0)),
            pl.BlockSpec((tm, D_MODEL), lambda i, e: (i, 0)),
            pl.BlockSpec((None, 1, 6 * D_MODEL), lambda i, e: (mod_row_moe(i), 0, 0)),
            pl.BlockSpec((None, D_MODEL, D_EXPERT), lambda i, e: (e, 0, 0)),
            pl.BlockSpec((None, D_MODEL, D_EXPERT), lambda i, e: (e, 0, 0)),
            pl.BlockSpec((None, D_EXPERT, D_MODEL), lambda i, e: (e, 0, 0)),
        ],
        out_specs=pl.BlockSpec((tm, D_MODEL), lambda i, e: (i, 0)),
        out_shape=jax.ShapeDtypeStruct((n, D_MODEL), F32),
        scratch_shapes=[pltpu.VMEM((tm, D_MODEL), F32)],
        compiler_params=_cparams(("parallel", "arbitrary")),
        name="moe",
    )(h2, gates, x1, mod_l, wg, wu, wd)


def _final_norm_kernel(x_ref, g_ref, o_ref):
    x = x_ref[...]
    o_ref[...] = x * lax.rsqrt(jnp.mean(x * x, axis=-1, keepdims=True) + NORM_EPS) * g_ref[...]


def _final_norm(x, g, tile0, n_tiles):
    return pl.pallas_call(
        _final_norm_kernel,
        grid=(n_tiles,),
        in_specs=[pl.BlockSpec((TILE, D_MODEL), lambda i: (i + tile0, 0)), _const_spec((1, D_MODEL))],
        out_specs=pl.BlockSpec((TILE, D_MODEL), lambda i: (i, 0)),
        out_shape=jax.ShapeDtypeStruct((n_tiles * TILE, D_MODEL), F32),
        compiler_params=_cparams(("parallel",)),
        name="final_norm",
    )(x, g)


def _grid_pos_embed(n_tokens):
    rows = n_tokens // GRID_W
    quarter = D_MODEL // 4
    half = D_MODEL // 2
    omega = 1.0 / (10000.0 ** (jnp.arange(quarter, dtype=F32) / quarter))
    ang_r = jnp.arange(rows, dtype=F32)[:, None] * omega
    ang_c = jnp.arange(GRID_W, dtype=F32)[:, None] * omega
    emb_r = jnp.concatenate([jnp.sin(ang_r), jnp.cos(ang_r)], axis=-1)
    emb_c = jnp.concatenate([jnp.sin(ang_c), jnp.cos(ang_c)], axis=-1)
    emb = jnp.concatenate([jnp.broadcast_to(emb_r[:, None, :], (rows, GRID_W, half)),
                           jnp.broadcast_to(emb_c[None, :, :], (rows, GRID_W, half))], axis=-1)
    return emb.reshape(rows * GRID_W, D_MODEL)


def _pad_rows(w, rows, offset):
    out = jnp.zeros(w.shape[:-2] + (rows, w.shape[-1]), w.dtype)
    return lax.dynamic_update_slice_in_dim(out, w, offset, axis=-2)


def _hg_cols(w):
    return jnp.moveaxis(w.reshape(w.shape[:-1] + (N_HG, HG_LANES)), -2, 0)


def kernel(x_prompt, x_sample, state_rwkv, c, c_ctx, norm1_g, norm2_g, w_mod, b_mod, w_in, b_in, rwkv_w0, rwkv_w_up,
           rwkv_a0, rwkv_a_up, rwkv_g_up, rwkv_k_k, rwkv_k_a, rwkv_r_k, rwkv_gn_g, rwkv_gn_b, rwkv_w_o, conv_dw,
           conv_dw_b, conv_ln_g, conv_ln_b, conv_w_o, sgu_ln_g, sgu_ln_b, sgu_w_s, sgu_b_s, sgu_w_o, pool_w,
           pool_scale, pool_w_o, w_out, moe_w_gate, moe_w_up, moe_w_down, router_w, router_b, final_norm_g):
    n_ctx, t_ctx, d = x_prompt.shape
    n_lat, t_lat, _ = x_sample.shape
    n_layers = w_in.shape[0]
    assert d == D_MODEL and t_ctx % TILE == 0 and t_lat % TILE == 0 and n_lat < MOD_ROWS
    ctx_rows, lat_rows = n_ctx * t_ctx, n_lat * t_lat
    assert ctx_rows % t_lat == 0 and ctx_rows % MOE_TILE == 0 and lat_rows % MOE_TILE == 0
    ctx_tiles = ctx_rows // TILE
    seq_tiles_ctx, seq_tiles_lat = t_ctx // TILE, t_lat // TILE

    def mod_row_for(tile_rows):
        ctx_t = ctx_rows // tile_rows
        per_lat = t_lat // tile_rows
        return lambda i: jnp.where(i < ctx_t, n_lat, jnp.maximum(i - ctx_t, 0) // per_lat)

    mod_row = mod_row_for(TILE)
    mod_row_moe = mod_row_for(MOE_TILE)

    cond = jnp.zeros((MOD_ROWS, d), F32).at[:n_lat].set(c).at[n_lat].set(c_ctx)
    mod = _modulation(cond, w_mod, b_mod).reshape(n_layers, MOD_ROWS, 1, 6 * d)

    x = _embed(x_prompt.reshape(ctx_rows, d), x_sample.reshape(lat_rows, d), _grid_pos_embed(t_lat), seq_tiles_lat)

    n_rw = RW_COLS
    n_mx = MX_COLS
    wa = w_in[:, :, :n_rw].astype(BF16)
    ba = b_in[:, None, :n_rw]
    wm = w_in[:, :, n_rw:n_rw + n_mx].astype(BF16)
    bm = b_in[:, None, n_rw:n_rw + n_mx]
    wgt = w_in[:, :, n_rw + n_mx:].astype(BF16)
    bgt = b_in[:, None, n_rw + n_mx:]

    par = jnp.zeros((n_layers, P_ROWS, RWKV_WIDTH), F32)
    par = par.at[:, P_W0:P_W0 + 2].set(rwkv_w0).at[:, P_A0:P_A0 + 2].set(rwkv_a0)
    par = par.at[:, P_KK].set(rwkv_k_k).at[:, P_KA].set(rwkv_k_a)
    par = par.at[:, P_RK].set(rwkv_r_k.reshape(n_layers, RWKV_WIDTH))
    par = par.at[:, P_GNG].set(rwkv_gn_g).at[:, P_GNB].set(rwkv_gn_b)
    par = jnp.moveaxis(_hg_cols(par), 0, 1)
    lr_k = RWKV_DECAY_RANK + RWKV_ICLR_RANK
    wup = jnp.moveaxis(_hg_cols(_pad_rows(rwkv_w_up, lr_k, 0)), 0, 2).astype(BF16)
    aup = jnp.moveaxis(_hg_cols(_pad_rows(rwkv_a_up, lr_k, RWKV_DECAY_RANK)), 0, 2).astype(BF16)
    gup = jnp.moveaxis(_hg_cols(rwkv_g_up), 0, 1).astype(BF16)
    rw_consts = _rwkv_consts()

    mix_vecs = jnp.zeros((n_layers, 8, CONV_WIDTH), F32)
    for row, vec in enumerate((conv_dw_b, conv_ln_g, conv_ln_b, sgu_ln_g, sgu_ln_b, pool_scale)):
        mix_vecs = mix_vecs.at[:, row].set(vec)
    sgu_wcat = jnp.transpose(sgu_w_s, (0, 2, 1, 3)).reshape(n_layers, SGU_CHUNK, SGU_GROUPS * SGU_CHUNK).astype(BF16)
    sgu_bs = jnp.repeat(jnp.swapaxes(sgu_b_s, 1, 2), SGU_WIDTH // SGU_GROUPS, axis=2)
    pc = POOL_WIDTH // POOL_GROUPS
    pool_bd = (pool_w[:, :, :, None, :] * jnp.eye(POOL_GROUPS, dtype=F32)[None, :, None, :, None])
    pool_bd = pool_bd.reshape(n_layers, POOL_WIDTH, POOL_WIDTH).astype(BF16)
    lane_grp = jnp.arange(SGU_WIDTH)[None, :] // (SGU_WIDTH // SGU_GROUPS)
    sgm = (jnp.arange(SGU_GROUPS * SGU_CHUNK)[:, None] // SGU_CHUNK == lane_grp).astype(BF16)
    wlen = TILE + 2 * HALO
    plm = (jnp.arange(POOL_GROUPS * wlen)[:, None] // wlen == jnp.arange(POOL_WIDTH)[None, :] // pc).astype(BF16)

    wro, wco, wso, wpo, wout = (w.astype(BF16) for w in (rwkv_w_o, conv_w_o, sgu_w_o, pool_w_o, w_out))
    rw_t = jnp.pad(router_w, ((0, 0), (0, ROUTER_LANES - N_EXPERTS)))
    rb = router_b[:, None]
    moe_g, moe_u, moe_d = moe_w_gate.astype(BF16), moe_w_up.astype(BF16), moe_w_down.astype(BF16)

    ctx_s0 = jnp.zeros((n_ctx, N_HG, 2, RWKV_HEAD_DIM, HG_LANES), F32)
    ctx_states = []
    for l in range(n_layers):
        zr, zm = _in_projection(x, mod[l], mod_row, norm1_g[l][None], wa[l], ba[l], wm[l], bm[l])
        rw_args = (par[l], wup[l], aup[l], gup[l], rw_consts)
        ya_ctx, s_fin = _rwkv_mixer(zr, ctx_s0, n_ctx, t_ctx, 0, *rw_args)
        ya_lat, _ = _rwkv_mixer(zr, _states_to_kernel(state_rwkv[:, l]), n_lat, t_lat, ctx_rows // t_lat, *rw_args)
        ctx_states.append(_states_from_kernel(s_fin))
        ya = jnp.concatenate([ya_ctx, ya_lat], axis=0)
        mx = _mixers(zm, ctx_tiles, seq_tiles_ctx, seq_tiles_lat, conv_dw[l], mix_vecs[l], sgu_wcat[l], sgu_bs[l],
                     pool_bd[l], sgm, plm)
        x1, h2, gates_t = _merge(x, mod[l], mod_row, norm1_g[l][None], norm2_g[l][None], ya, mx, wgt[l], bgt[l],
                                 wro[l], wco[l], wso[l], wpo[l], wout[l], rw_t, rb)
        x = _moe(h2, gates_t.T, x1, mod[l], mod_row_moe, moe_g[l], moe_u[l], moe_d[l])

    g_fin = final_norm_g[None]
    y_prompt = _final_norm(x, g_fin, 0, ctx_tiles).reshape(n_ctx, t_ctx, d)
    y_sample = _final_norm(x, g_fin, ctx_tiles, lat_rows // TILE).reshape(n_lat, t_lat, d)
    new_state = jnp.stack(ctx_states, axis=1).astype(x_prompt.dtype)
    return (y_prompt, y_sample, new_state)
```

```python
import functools
import math

import jax
import jax.numpy as jnp
from jax import lax
from jax.experimental import pallas as pl
from jax.experimental.pallas import tpu as pltpu

F32 = jnp.float32
BF16 = jnp.bfloat16

D_MODEL = 1024
GRID_W = 64
RWKV_HEADS = 8
RWKV_HEAD_DIM = 64
RWKV_WIDTH = RWKV_HEADS * RWKV_HEAD_DIM
RWKV_DECAY_RANK = 64
RWKV_ICLR_RANK = 64
RWKV_GATE_RANK = 128
RWKV_GN_EPS = 64e-5
CONV_WIDTH = 256
CONV_KERNEL = 31
SGU_GROUPS = 4
SGU_WIDTH = 256
SGU_CHUNK = 128
POOL_GROUPS = 4
POOL_WIDTH = 256
POOL_WINDOWS = (2, 4, 8, 16)
N_BRANCHES = 4
N_EXPERTS = 16
N_EXPERT_GROUPS = 4
EXPERTS_PER_GROUP = N_EXPERTS // N_EXPERT_GROUPS
D_EXPERT = 512
NORM_EPS = 1e-6
LN_EPS = 1e-5

RW_COLS = 3 * RWKV_WIDTH + RWKV_DECAY_RANK + RWKV_ICLR_RANK + RWKV_GATE_RANK
MX_COLS = 2 * CONV_WIDTH + 2 * SGU_WIDTH + POOL_WIDTH
MX_OUT = CONV_WIDTH + SGU_WIDTH + POOL_WIDTH

TILE = 256
PROJ_TILE = 512
MOE_TILE = 1024
HALO = 16
SUBLANES = 8
SHIFTED_ROWS = TILE + 2 * HALO - SUBLANES
HG_LANES = 256
HEADS_PER_HG = HG_LANES // RWKV_HEAD_DIM
N_HG = RWKV_WIDTH // HG_LANES
SCAN_CHUNK = 64
assert SCAN_CHUNK == RWKV_HEAD_DIM
CHUNKS_PER_TRIP = 2
DECAY_SCALE = math.exp(-0.5)
MOD_ROWS = 16
ROUTER_LANES = 128

VMEM_LIMIT = 48 * 1024 * 1024
RWKV_VMEM_LIMIT = 56 * 1024 * 1024


def _cparams(sem, vmem_limit=VMEM_LIMIT):
    return pltpu.CompilerParams(dimension_semantics=sem, vmem_limit_bytes=vmem_limit)


def _dot(a, b):
    return jnp.dot(a, b, preferred_element_type=F32)


def _split2(x):
    hi = x.astype(BF16)
    lo = (x - hi.astype(F32)).astype(BF16)
    return hi, lo


def _dot3(a, b):
    a_hi, a_lo = _split2(a)
    b_hi, b_lo = _split2(b)
    return _dot(a_hi, b_hi) + (_dot(a_lo, b_hi) + _dot(a_hi, b_lo))


def _dot_exact_rhs(x, rhs):
    hi, lo = _split2(x)
    return _dot(hi, rhs) + _dot(lo, rhs)


def _dot_exact_lhs(lhs, x):
    hi, lo = _split2(x)
    return _dot(lhs, hi) + _dot(lhs, lo)


def _sigmoid(x):
    return 1.0 / (1.0 + jnp.exp(-x))


def _silu(x):
    return x * _sigmoid(x)


def _norm_mod(x, g, shift, scale):
    y = x * lax.rsqrt(jnp.mean(x * x, axis=-1, keepdims=True) + NORM_EPS) * g
    return y * (1.0 + scale) + shift


def _layer_norm(x, g, b):
    mu = jnp.mean(x, axis=-1, keepdims=True)
    xc = x - mu
    var = jnp.mean(xc * xc, axis=-1, keepdims=True)
    return xc * lax.rsqrt(var + LN_EPS) * g + b


def _const_spec(shape):
    nd = len(shape)
    return pl.BlockSpec(shape, lambda *_: (0,) * nd)


def _mod_kernel(c_ref, w_ref, b_ref, o_ref):
    o_ref[...] = _dot3(_silu(c_ref[...]), w_ref[...]) + b_ref[...]


def _modulation(cond, w_mod, b_mod):
    n_layers = w_mod.shape[0]
    tn = 1536
    return pl.pallas_call(
        _mod_kernel,
        grid=(n_layers, 6 * D_MODEL // tn),
        in_specs=[
            pl.BlockSpec((MOD_ROWS, D_MODEL), lambda l, j: (0, 0)),
            pl.BlockSpec((None, D_MODEL, tn), lambda l, j: (l, 0, j)),
            pl.BlockSpec((None, 1, tn), lambda l, j: (l, 0, j)),
        ],
        out_specs=pl.BlockSpec((None, MOD_ROWS, tn), lambda l, j: (l, 0, j)),
        out_shape=jax.ShapeDtypeStruct((n_layers, MOD_ROWS, 6 * D_MODEL), F32),
        compiler_params=_cparams(("parallel", "parallel")),
        name="modulation",
    )(cond, w_mod, b_mod.reshape(n_layers, 1, 6 * D_MODEL))


def _embed_kernel(n_ctx_tiles, xp_ref, xs_ref, pos_ref, o_ref):
    i = pl.program_id(0)

    @pl.when(i < n_ctx_tiles)
    def _():
        o_ref[...] = xp_ref[...]

    @pl.when(i >= n_ctx_tiles)
    def _():
        o_ref[...] = xs_ref[...] + pos_ref[...]


def _embed(xp, xs, pos, lat_tiles):
    n_ctx_tiles = xp.shape[0] // TILE
    n_lat_tiles = xs.shape[0] // TILE
    n_tiles = n_ctx_tiles + n_lat_tiles
    return pl.pallas_call(
        functools.partial(_embed_kernel, n_ctx_tiles),
        grid=(n_tiles,),
        in_specs=[
            pl.BlockSpec((TILE, D_MODEL), lambda i: (jnp.minimum(i, n_ctx_tiles - 1), 0)),
            pl.BlockSpec((TILE, D_MODEL), lambda i: (jnp.maximum(i - n_ctx_tiles, 0), 0)),
            pl.BlockSpec((TILE, D_MODEL), lambda i: (jnp.maximum(i - n_ctx_tiles, 0) % lat_tiles, 0)),
        ],
        out_specs=pl.BlockSpec((TILE, D_MODEL), lambda i: (i, 0)),
        out_shape=jax.ShapeDtypeStruct((n_tiles * TILE, D_MODEL), F32),
        compiler_params=_cparams(("parallel",)),
        name="embed",
    )(xp, xs, pos)


def _inproj_kernel(x_ref, mod_ref, g_ref, wa_ref, ba_ref, wm_ref, bm_ref, zr_ref, zm_ref):
    h = _norm_mod(x_ref[...], g_ref[...], mod_ref[:, 0:D_MODEL], mod_ref[:, D_MODEL:2 * D_MODEL])
    hb = h.astype(BF16)
    zr_ref[...] = _dot(hb, wa_ref[...]) + ba_ref[...]
    zm_ref[...] = (_dot(hb, wm_ref[...]) + bm_ref[...]).astype(BF16)


def _in_projection(x, mod_l, mod_row, norm_g, wa, ba, wm, bm):
    n = x.shape[0]
    tm = PROJ_TILE
    return pl.pallas_call(
        _inproj_kernel,
        grid=(n // tm,),
        in_specs=[
            pl.BlockSpec((tm, D_MODEL), lambda i: (i, 0)),
            pl.BlockSpec((None, 1, 6 * D_MODEL), lambda i: (mod_row(i), 0, 0)),
            _const_spec((1, D_MODEL)),
            _const_spec((D_MODEL, RW_COLS)),
            _const_spec((1, RW_COLS)),
            _const_spec((D_MODEL, MX_COLS)),
            _const_spec((1, MX_COLS)),
        ],
        out_specs=[
            pl.BlockSpec((tm, RW_COLS), lambda i: (i, 0)),
            pl.BlockSpec((tm, MX_COLS), lambda i: (i, 0)),
        ],
        out_shape=[
            jax.ShapeDtypeStruct((n, RW_COLS), F32),
            jax.ShapeDtypeStruct((n, MX_COLS), BF16),
        ],
        compiler_params=_cparams(("parallel",)),
        name="in_projection",
    )(x, mod_l, norm_g, wa, ba, wm, bm)


P_W0, P_A0, P_KK, P_KA, P_RK, P_GNG, P_GNB = 0, 2, 4, 5, 6, 7, 8
P_ROWS = 16


def _rep(xb, reps, mask):
    return jnp.concatenate([xb] * reps, axis=0) * mask


def _rwkv_kernel(t_len, aliased, *refs):
    if aliased:
        refs = refs[1:]
    (r_ref, k_ref, v_ref, lr_ref, s0_ref, par_ref, wup_ref, aup_ref, gup_ref,
     bd_ref, cm_ref, tri_ref, eyec_ref, out_ref, sfin_ref, y_scr, st_scr) = refs
    c = SCAN_CHUNK
    n_chunks = t_len // c
    n_trips = n_chunks // CHUNKS_PER_TRIP
    lr_k = RWKV_DECAY_RANK + RWKV_ICLR_RANK
    probs = [(d, sl, g) for sl in range(CHUNKS_PER_TRIP) for d in (0, 1) for g in range(N_HG)]
    n_sq = int(math.log2(c))

    def lanes(g):
        return slice(g * HG_LANES, (g + 1) * HG_LANES)

    def rep(xb):
        return _rep(xb, HEADS_PER_HG, bd_ref[...])

    def seg_sums(xs, passes):
        n = xs[0].shape[0]
        if passes == 1:
            out = _dot(jnp.concatenate([x.astype(BF16) for x in xs], axis=0), bd_ref[...])
            return [out[i * n:(i + 1) * n] for i in range(len(xs))]
        parts = []
        for x in xs:
            parts.extend(_split2(x))
        out = _dot(jnp.concatenate(parts, axis=0), bd_ref[...])
        return [out[2 * i * n:(2 * i + 1) * n] + out[(2 * i + 1) * n:(2 * i + 2) * n] for i in range(len(xs))]

    def fold(x_bd):
        n = x_bd.shape[0] // HEADS_PER_HG
        return (x_bd[0:n] + x_bd[n:2 * n]) + (x_bd[2 * n:3 * n] + x_bd[3 * n:4 * n])

    def par(g, row):
        return par_ref[g, row:row + 1, :]

    st_scr[...] = s0_ref[...]

    def trip(j, first_touch):
        rows = {}
        for sl in range(CHUNKS_PER_TRIP):
            rows[0, sl] = pl.ds(pl.multiple_of((CHUNKS_PER_TRIP * j + sl) * c, c), c)
            rows[1, sl] = pl.ds(pl.multiple_of((n_chunks - 1 - CHUNKS_PER_TRIP * j - sl) * c, c), c)
        lr = {key: lr_ref[rw, 0:lr_k] for key, rw in rows.items()}
        th = {key: jnp.tanh(x).astype(BF16) for key, x in lr.items()}
        lrb = {key: x.astype(BF16) for key, x in lr.items()}
        ss = [dict(r=r_ref[rows[d, sl], lanes(g)], k=k_ref[rows[d, sl], lanes(g)], v=v_ref[rows[d, sl], lanes(g)])
              for d, sl, g in probs]

        for (d, sl, g), s in zip(probs, ss):
            s["w_pre"] = par(g, P_W0 + d) + _dot(th[d, sl], wup_ref[d, g])
            s["a_pre"] = par(g, P_A0 + d) + _dot(lrb[d, sl], aup_ref[d, g])
            s["kkr"] = s["k"] * par(g, P_KK)
        for s, ssq in zip(ss, seg_sums([s["kkr"] * s["kkr"] for s in ss], 1)):
            s["ssq"] = ssq
        for (d, sl, g), s in zip(probs, ss):
            s["lw"] = -DECAY_SCALE * _sigmoid(s["w_pre"])
            s["cum"] = _dot_exact_lhs(tri_ref[d], s["lw"])
        for (d, sl, g), s in zip(probs, ss):
            a = _sigmoid(s["a_pre"])
            kk = s["kkr"] / jnp.maximum(jnp.sqrt(s["ssq"]), 1e-12)
            kd = s["k"] * (1.0 + (a - 1.0) * par(g, P_KA))
            cum = s["cum"]
            e_incl = jnp.exp(cum)
            e_neg = jnp.exp(-cum)
            gam = e_incl[c - 1:c, :] if d == 0 else e_incl[0:1, :]
            s["at"] = (-kk * jnp.exp(cum - s["lw"])).astype(BF16)
            s["rt"] = s["r"] * e_incl
            bt = kk * a * e_neg
            kt = kd * e_neg
            s["vb"] = s["v"].astype(BF16)
            s["bonus"] = s["r"] * kd * par(g, P_RK)
            s["gam_rows"] = _dot_exact_rhs(eyec_ref[...] * gam, bd_ref[...])
            tr = jnp.concatenate([bt, kt], axis=0).T
            swapped = pltpu.roll(tr, c, axis=1)
            first_half = lax.broadcasted_iota(jnp.int32, tr.shape, 1) < c
            bt_t = jnp.where(first_half, tr, swapped).astype(BF16)
            kt_t = jnp.where(first_half, swapped, tr).astype(BF16)
            s["bt_bd"] = jnp.concatenate([bt_t, bt_t], axis=1) * bd_ref[...]
            s["kt_bd"] = jnp.concatenate([kt_t, kt_t], axis=1) * bd_ref[...]
            lhs2 = jnp.concatenate([s["at"], s["rt"].astype(BF16)], axis=0)
            s["m_b"] = _dot(lhs2, s["bt_bd"])
            s["m_k"] = _dot(lhs2, s["kt_bd"])
        for s, bsum in zip(ss, seg_sums([s["bonus"] for s in ss], 1)):
            s["bsum"] = bsum
        for (d, sl, g), s in zip(probs, ss):
            causal = cm_ref[d]
            m_b = s["m_b"] * causal
            s["x"] = m_b[0:c]
            s["a_rb"] = m_b[c:2 * c].astype(BF16)
            s["t"] = eyec_ref[...] + s["x"]
            s["bt_c"] = fold(s["bt_bd"])
            av = _dot(jnp.concatenate([(s["m_k"] * causal).astype(BF16), fold(s["kt_bd"])], axis=0), rep(s["vb"]))
            s["g1"] = av[0:c].astype(BF16)
            s["y0"] = av[c:2 * c] + s["bsum"] * s["v"]
            s["z"] = av[2 * c:2 * c + RWKV_HEAD_DIM]

        for step in range(1, n_sq):
            for s in ss:
                x_bd = rep(s["x"].astype(BF16))
                if step == 1:
                    s["x"] = _dot(s["x"].astype(BF16), x_bd)
                else:
                    both = _dot(jnp.concatenate([s["x"], s["t"]], axis=0).astype(BF16), x_bd)
                    s["x"], s["t"] = both[0:c], s["t"] + both[c:2 * c]
        for s in ss:
            s["t"] = (s["t"] + _dot(s["t"].astype(BF16), rep(s["x"].astype(BF16)))).astype(BF16)

        for s in ss:
            s["w"] = _dot(s["t"], rep(s["at"])).astype(BF16)
            s["u0"] = _dot(s["t"], rep(s["g1"])).astype(BF16)
        for s in ss:
            lhs = jnp.concatenate([s["a_rb"], s["bt_c"]], axis=0)
            with_w = _dot(lhs, rep(s["w"]))
            with_u0 = _dot(lhs, rep(s["u0"]))
            s["q"] = (s["rt"] + with_w[0:c]).astype(BF16)
            s["p"] = (eyec_ref[...] + with_w[c:c + RWKV_HEAD_DIM]).astype(BF16)
            s["y0"] = s["y0"] + with_u0[0:c]
            s["z"] = s["z"] + with_u0[c:c + RWKV_HEAD_DIM]

        for (d, sl, g), s in zip(probs, ss):
            both = _dot(jnp.concatenate([s["p"], s["q"]], axis=0), rep(st_scr[g, d].astype(BF16)))
            st_scr[g, d] = s["gam_rows"] * (both[0:RWKV_HEAD_DIM] + s["z"])
            y = both[RWKV_HEAD_DIM:RWKV_HEAD_DIM + c] + s["y0"]
            if first_touch:
                y_scr[rows[d, sl], lanes(g)] = y
            else:
                y_scr[rows[d, sl], lanes(g)] += y

    half = n_trips // 2
    lax.fori_loop(0, half, lambda j, carry: (trip(j, True), carry)[1], 0)
    lax.fori_loop(half, n_trips, lambda j, carry: (trip(j, False), carry)[1], 0)
    sfin_ref[...] = st_scr[...]

    inv_n = 1.0 / RWKV_HEAD_DIM

    def finish(j, carry):
        rows = pl.ds(pl.multiple_of(j * TILE, TILE), TILE)
        sg = _sigmoid(lr_ref[rows, lr_k:HG_LANES]).astype(BF16)
        ys = [y_scr[rows, lanes(g)] for g in range(N_HG)]
        mus = [m * inv_n for m in seg_sums(ys, 2)]
        ycs = [y - mu for y, mu in zip(ys, mus)]
        variances = [v * inv_n for v in seg_sums([yc * yc for yc in ycs], 2)]
        for g in range(N_HG):
            yn = ycs[g] * lax.rsqrt(variances[g] + RWKV_GN_EPS) * par(g, P_GNG) + par(g, P_GNB)
            out_ref[rows, lanes(g)] = (yn * _dot(sg, gup_ref[g])).astype(BF16)
        return carry

    lax.fori_loop(0, t_len // TILE, finish, 0)


def _rwkv_mixer(zr, s0, n_seq, t_len, row_off, y_other, par, wup, aup, gup, consts):
    lr_block = 3 * RWKV_WIDTH // HG_LANES
    state_spec = pl.BlockSpec((None, N_HG, 2, RWKV_HEAD_DIM, HG_LANES), lambda b: (b, 0, 0, 0, 0))
    weights = (par, wup, aup, gup)
    aliased = y_other is not None
    lead_specs = [pl.BlockSpec(memory_space=pl.ANY)] if aliased else []
    lead_args = [y_other] if aliased else []
    return pl.pallas_call(
        functools.partial(_rwkv_kernel, t_len, aliased),
        grid=(n_seq,),
        in_specs=[
            *lead_specs,
            pl.BlockSpec((t_len, RWKV_WIDTH), lambda b: (b + row_off, 0)),
            pl.BlockSpec((t_len, RWKV_WIDTH), lambda b: (b + row_off, 1)),
            pl.BlockSpec((t_len, RWKV_WIDTH), lambda b: (b + row_off, 2)),
            pl.BlockSpec((t_len, HG_LANES), lambda b: (b + row_off, lr_block)),
            state_spec,
            *[_const_spec(a.shape) for a in weights],
            *[_const_spec(a.shape) for a in consts],
        ],
        out_specs=[pl.BlockSpec((t_len, RWKV_WIDTH), lambda b: (b + row_off, 0)), state_spec],
        out_shape=[
            jax.ShapeDtypeStruct((zr.shape[0], RWKV_WIDTH), BF16),
            jax.ShapeDtypeStruct((n_seq, N_HG, 2, RWKV_HEAD_DIM, HG_LANES), F32),
        ],
        scratch_shapes=[
            pltpu.VMEM((t_len, RWKV_WIDTH), F32),
            pltpu.VMEM((N_HG, 2, RWKV_HEAD_DIM, HG_LANES), F32),
        ],
        input_output_aliases={0: 0} if aliased else {},
        compiler_params=_cparams(("parallel",), RWKV_VMEM_LIMIT),
        name="rwkv_mixer",
    )(*lead_args, zr, zr, zr, zr, s0, *weights, *consts)


def _rwkv_consts():
    c = SCAN_CHUNK
    ri = jnp.arange(HG_LANES)[:, None]
    ci = jnp.arange(HG_LANES)[None, :]
    bd = (ri // RWKV_HEAD_DIM == ci // RWKV_HEAD_DIM).astype(BF16)
    t = jnp.arange(c)[:, None]
    i = (jnp.arange(HEADS_PER_HG * c) % c)[None, :]
    cm = jnp.stack([jnp.concatenate([i < t, i <= t]), jnp.concatenate([i > t, i >= t])]).astype(F32)
    ti = jnp.arange(c)[None, :]
    tri = jnp.stack([ti <= t, ti >= t]).astype(BF16)
    eyec = (i == t).astype(F32)
    return bd, cm, tri, eyec


def _states_to_kernel(s):
    b = s.shape[0]
    st = s.reshape(b, 2, N_HG, HEADS_PER_HG, RWKV_HEAD_DIM, RWKV_HEAD_DIM)
    return jnp.transpose(st, (0, 2, 1, 5, 3, 4)).reshape(b, N_HG, 2, RWKV_HEAD_DIM, HG_LANES)


def _states_from_kernel(sk):
    b = sk.shape[0]
    st = sk.reshape(b, N_HG, 2, RWKV_HEAD_DIM, HEADS_PER_HG, RWKV_HEAD_DIM)
    return jnp.transpose(st, (0, 2, 1, 4, 5, 3)).reshape(b, 2, RWKV_HEADS, RWKV_HEAD_DIM, RWKV_HEAD_DIM)


def _pool_tables():
    wlen = TILE + 2 * HALO
    t_loc = jnp.arange(TILE)[:, None]
    s_rel = jnp.arange(wlen)[None, :] - HALO
    lane_g = jnp.arange(POOL_WIDTH)[None, :] // (POOL_WIDTH // POOL_GROUPS)
    bands, counts = [], []
    for first in (0, 1):
        for last in (0, 1):
            lo_seq = 0 if first else -HALO
            hi_seq = TILE if last else TILE + HALO
            band_groups = []
            cnt = jnp.zeros((TILE, POOL_WIDTH), F32)
            for gi, win in enumerate(POOL_WINDOWS):
                lo = jnp.maximum(t_loc - win // 2, lo_seq)
                hi = jnp.minimum(t_loc + win - win // 2, hi_seq)
                band_groups.append((s_rel >= lo) & (s_rel < hi))
                cnt = jnp.where(lane_g == gi, (hi - lo).astype(F32), cnt)
            bands.append(jnp.concatenate(band_groups, axis=1))
            counts.append(cnt)
    return jnp.stack(bands).astype(BF16), jnp.stack(counts)


def _mixers_kernel(ctx_tiles, seq_tiles_ctx, seq_tiles_lat, zc_ref, zp_ref, zn_ref, dw_ref, vec_ref, wcat_ref,
                   bs_ref, pw_ref, sgm_ref, plm_ref, band_ref, cnt_ref, out_ref, win_scr, shift_scr):
    i = pl.program_id(0)
    j_ctx = i % seq_tiles_ctx
    j_lat = jnp.maximum(i - ctx_tiles, 0) % seq_tiles_lat
    is_ctx = i < ctx_tiles
    first = jnp.where(is_ctx, j_ctx == 0, j_lat == 0)
    last = jnp.where(is_ctx, j_ctx == seq_tiles_ctx - 1, j_lat == seq_tiles_lat - 1)
    keep_prev = jnp.where(first, 0.0, 1.0)
    keep_next = jnp.where(last, 0.0, 1.0)
    case = 2 * first.astype(jnp.int32) + last.astype(jnp.int32)

    cw = CONV_WIDTH
    conv_b, cln_g, cln_b = vec_ref[0:1, :], vec_ref[1:2, :], vec_ref[2:3, :]
    sln_g, sln_b, pool_scale = vec_ref[3:4, :], vec_ref[4:5, :], vec_ref[5:6, :]

    glu = lambda z_ref: z_ref[:, 0:cw].astype(F32) * _sigmoid(z_ref[:, cw:2 * cw].astype(F32))
    win_scr[0:HALO, :] = glu(zp_ref) * keep_prev
    win_scr[HALO:HALO + TILE, :] = glu(zc_ref)
    win_scr[HALO + TILE:2 * HALO + TILE, :] = glu(zn_ref) * keep_next
    for r in range(1, SUBLANES):
        shift_scr[r] = win_scr[pl.ds(r, SHIFTED_ROWS), :]
    acc = jnp.zeros((TILE, cw), F32) + conv_b
    pad = CONV_KERNEL // 2
    for j in range(CONV_KERNEL):
        q, r = divmod(HALO - pad + j, SUBLANES)
        rows = pl.ds(q * SUBLANES, TILE)
        tap = win_scr[rows, :] if r == 0 else shift_scr[r, rows, :]
        acc = acc + tap * dw_ref[j:j + 1, :]
    out_ref[:, 0:cw] = _silu(_layer_norm(acc, cln_g, cln_b)).astype(BF16)

    su = zc_ref[:, 2 * cw:2 * cw + SGU_WIDTH].astype(F32)
    sv = zc_ref[:, 2 * cw + SGU_WIDTH:2 * cw + 2 * SGU_WIDTH].astype(F32)
    vn = _layer_norm(sv, sln_g, sln_b).astype(BF16)
    sgm = sgm_ref[...]
    for ch in range(TILE // SGU_CHUNK):
        rows = slice(ch * SGU_CHUNK, (ch + 1) * SGU_CHUNK)
        s = _dot(wcat_ref[...], _rep(vn[rows], SGU_GROUPS, sgm)) + bs_ref[...]
        out_ref[rows, cw:cw + SGU_WIDTH] = (su[rows] * s).astype(BF16)

    zoff = 2 * cw + 2 * SGU_WIDTH
    zcur = zc_ref[:, zoff:zoff + POOL_WIDTH]
    zw = jnp.concatenate([zp_ref[:, zoff:zoff + POOL_WIDTH], zcur, zn_ref[:, zoff:zoff + POOL_WIDTH]], axis=0)
    psum = _dot(band_ref[case], _rep(zw, POOL_GROUPS, plm_ref[...]))
    p = psum / cnt_ref[case] - zcur.astype(F32)
    pooled = _dot(p.astype(BF16), pw_ref[...]) * pool_scale
    out_ref[:, cw + SGU_WIDTH:cw + SGU_WIDTH + POOL_WIDTH] = pooled.astype(BF16)


def _mixers(zm, ctx_tiles, seq_tiles_ctx, seq_tiles_lat, *tables):
    n = zm.shape[0]
    n_tiles = n // TILE
    per = TILE // HALO
    n_halo = n // HALO
    return pl.pallas_call(
        functools.partial(_mixers_kernel, ctx_tiles, seq_tiles_ctx, seq_tiles_lat),
        grid=(n_tiles,),
        in_specs=[
            pl.BlockSpec((TILE, MX_COLS), lambda i: (i, 0)),
            pl.BlockSpec((HALO, MX_COLS), lambda i: (jnp.maximum(i * per - 1, 0), 0)),
            pl.BlockSpec((HALO, MX_COLS), lambda i: (jnp.minimum((i + 1) * per, n_halo - 1), 0)),
            *[_const_spec(a.shape) for a in tables],
        ],
        out_specs=pl.BlockSpec((TILE, MX_OUT), lambda i: (i, 0)),
        out_shape=jax.ShapeDtypeStruct((n, MX_OUT), BF16),
        scratch_shapes=[
            pltpu.VMEM((TILE + 2 * HALO, CONV_WIDTH), F32),
            pltpu.VMEM((SUBLANES, SHIFTED_ROWS, CONV_WIDTH), F32),
        ],
        compiler_params=_cparams(("parallel",)),
        name="mixers",
    )(zm, zm, zm, *tables)


def _route(logits_t):
    m = jnp.max(logits_t, axis=0, keepdims=True)
    e = jnp.exp(logits_t - m)
    p = e / jnp.sum(e, axis=0, keepdims=True)
    best_score = None
    best = None
    for g in range(N_EXPERT_GROUPS):
        rows = [p[g * EXPERTS_PER_GROUP + q:g * EXPERTS_PER_GROUP + q + 1] for q in range(EXPERTS_PER_GROUP)]
        score = None
        for a in range(EXPERTS_PER_GROUP):
            for b in range(a + 1, EXPERTS_PER_GROUP):
                pair = rows[a] + rows[b]
                score = pair if score is None else jnp.maximum(score, pair)
        if g == 0:
            best_score, best = score, jnp.zeros(score.shape, jnp.int32)
        else:
            upd = score > best_score
            best = jnp.where(upd, g, best)
            best_score = jnp.where(upd, score, best_score)
    eidx = lax.broadcasted_iota(jnp.int32, p.shape, 0)
    neg = -jnp.inf
    masked = jnp.where(eidx // EXPERTS_PER_GROUP == best, p, neg)
    m1 = jnp.max(masked, axis=0, keepdims=True)
    i1 = jnp.min(jnp.where(masked == m1, eidx, N_EXPERTS), axis=0, keepdims=True)
    masked2 = jnp.where(eidx == i1, neg, masked)
    m2 = jnp.max(masked2, axis=0, keepdims=True)
    i2 = jnp.min(jnp.where(masked2 == m2, eidx, N_EXPERTS), axis=0, keepdims=True)
    tot = m1 + m2
    return jnp.where(eidx == i1, m1 / tot, 0.0) + jnp.where(eidx == i2, m2 / tot, 0.0)


def _merge_kernel(x_ref, mod_ref, n1_ref, n2_ref, ya_ref, mx_ref, wg_ref, bg_ref, wro_ref, wco_ref, wso_ref,
                  wpo_ref, wout_ref, rw_ref, rb_ref, x1_ref, h2_ref, gates_ref):
    d = D_MODEL
    x = x_ref[...]
    hb = _norm_mod(x, n1_ref[...], mod_ref[:, 0:d], mod_ref[:, d:2 * d]).astype(BF16)
    cw = CONV_WIDTH
    branches = (
        (ya_ref[...], wro_ref),
        (mx_ref[:, 0:cw], wco_ref),
        (mx_ref[:, cw:cw + SGU_WIDTH], wso_ref),
        (mx_ref[:, cw + SGU_WIDTH:MX_OUT], wpo_ref),
    )
    merged = None
    for bi, (y_in, w_ref) in enumerate(branches):
        zg = _dot(hb, wg_ref[:, bi * d:(bi + 1) * d]) + bg_ref[:, bi * d:(bi + 1) * d]
        term = _sigmoid(zg) * _dot(y_in, w_ref[...])
        merged = term if merged is None else merged + term
    x1 = x + mod_ref[:, 2 * d:3 * d] * _dot(merged.astype(BF16), wout_ref[...])
    x1_ref[...] = x1
    h2 = _norm_mod(x1, n2_ref[...], mod_ref[:, 3 * d:4 * d], mod_ref[:, 4 * d:5 * d])
    h2_ref[...] = h2.astype(BF16)
    logits = _dot3(h2, rw_ref[...])
    gates_ref[...] = _route(logits.T[0:N_EXPERTS] + rb_ref[...])


def _merge(x, mod_l, mod_row, n1, n2, ya, mx, wg, bg, wro, wco, wso, wpo, wout, rw_t, rb):
    n = x.shape[0]
    tm = TILE
    consts = (n1, n2)
    weights = (wg, bg, wro, wco, wso, wpo, wout, rw_t, rb)
    return pl.pallas_call(
        _merge_kernel,
        grid=(n // tm,),
        in_specs=[
            pl.BlockSpec((tm, D_MODEL), lambda i: (i, 0)),
            pl.BlockSpec((None, 1, 6 * D_MODEL), lambda i: (mod_row(i), 0, 0)),
            *[_const_spec(a.shape) for a in consts],
            pl.BlockSpec((tm, RWKV_WIDTH), lambda i: (i, 0)),
            pl.BlockSpec((tm, MX_OUT), lambda i: (i, 0)),
            *[_const_spec(a.shape) for a in weights],
        ],
        out_specs=[
            pl.BlockSpec((tm, D_MODEL), lambda i: (i, 0)),
            pl.BlockSpec((tm, D_MODEL), lambda i: (i, 0)),
            pl.BlockSpec((N_EXPERTS, tm), lambda i: (0, i)),
        ],
        out_shape=[
            jax.ShapeDtypeStruct((n, D_MODEL), F32),
            jax.ShapeDtypeStruct((n, D_MODEL), BF16),
            jax.ShapeDtypeStruct((N_EXPERTS, n), F32),
        ],
        compiler_params=_cparams(("parallel",)),
        name="merge_router",
    )(x, mod_l, n1, n2, ya, mx, *weights)


def _moe_kernel(h_ref, gates_ref, x_ref, mod_ref, wg_ref, wu_ref, wd_ref, o_ref, acc_ref):
    e = pl.program_id(1)

    @pl.when(e == 0)
    def _():
        acc_ref[...] = jnp.zeros_like(acc_ref)

    h = h_ref[...]
    act = _silu(_dot(h, wg_ref[...].astype(BF16))) * _dot(h, wu_ref[...].astype(BF16))
    ff = _dot(act.astype(BF16), wd_ref[...].astype(BF16))
    lane = lax.broadcasted_iota(jnp.int32, gates_ref.shape, 1)
    gate = jnp.sum(jnp.where(lane == e, gates_ref[...], 0.0), axis=1, keepdims=True)
    acc_ref[...] += gate * ff

    @pl.when(e == N_EXPERTS - 1)
    def _():
        o_ref[...] = x_ref[...] + mod_ref[:, 5 * D_MODEL:6 * D_MODEL] * acc_ref[...]


def _moe(h2, gates, x1, mod_l, mod_row_moe, wg, wu, wd):
    n = h2.shape[0]
    tm = MOE_TILE
    return pl.pallas_call(
        _moe_kernel,
        grid=(n // tm, N_EXPERTS),
        in_specs=[
            pl.BlockSpec((tm, D_MODEL), lambda i, e: (i, 0)),
            pl.BlockSpec((tm, N_EXPERTS), lambda i, e: (i, 0)),
            pl.BlockSpec((tm, D_MODEL), lambda i, e: (i, 0)),
            pl.BlockSpec((None, 1, 6 * D_MODEL), lambda i, e: (mod_row_moe(i), 0, 0)),
            pl.BlockSpec((None, D_MODEL, D_EXPERT), lambda i, e: (e, 0, 0)),
            pl.BlockSpec((None, D_MODEL, D_EXPERT), lambda i, e: (e, 0, 0)),
            pl.BlockSpec((None, D_EXPERT, D_MODEL), lambda i, e: (e, 0, 0)),
        ],
        out_specs=pl.BlockSpec((tm, D_MODEL), lambda i, e: (i, 0)),
        out_shape=jax.ShapeDtypeStruct((n, D_MODEL), F32),
        scratch_shapes=[pltpu.VMEM((tm, D_MODEL), F32)],
        compiler_params=_cparams(("parallel", "arbitrary")),
        name="moe",
    )(h2, gates, x1, mod_l, wg, wu, wd)


def _final_norm_kernel(x_ref, g_ref, o_ref):
    x = x_ref[...]
    o_ref[...] = x * lax.rsqrt(jnp.mean(x * x, axis=-1, keepdims=True) + NORM_EPS) * g_ref[...]


def _final_norm(x, g, tile0, n_tiles):
    return pl.pallas_call(
        _final_norm_kernel,
        grid=(n_tiles,),
        in_specs=[pl.BlockSpec((TILE, D_MODEL), lambda i: (i + tile0, 0)), _const_spec((1, D_MODEL))],
        out_specs=pl.BlockSpec((TILE, D_MODEL), lambda i: (i, 0)),
        out_shape=jax.ShapeDtypeStruct((n_tiles * TILE, D_MODEL), F32),
        compiler_params=_cparams(("parallel",)),
        name="final_norm",
    )(x, g)


def _grid_pos_embed(n_tokens):
    rows = n_tokens // GRID_W
    quarter = D_MODEL // 4
    half = D_MODEL // 2
    omega = 1.0 / (10000.0 ** (jnp.arange(quarter, dtype=F32) / quarter))
    ang_r = jnp.arange(rows, dtype=F32)[:, None] * omega
    ang_c = jnp.arange(GRID_W, dtype=F32)[:, None] * omega
    emb_r = jnp.concatenate([jnp.sin(ang_r), jnp.cos(ang_r)], axis=-1)
    emb_c = jnp.concatenate([jnp.sin(ang_c), jnp.cos(ang_c)], axis=-1)
    emb = jnp.concatenate([jnp.broadcast_to(emb_r[:, None, :], (rows, GRID_W, half)),
                           jnp.broadcast_to(emb_c[None, :, :], (rows, GRID_W, half))], axis=-1)
    return emb.reshape(rows * GRID_W, D_MODEL)


def _pad_rows(w, rows, offset):
    out = jnp.zeros(w.shape[:-2] + (rows, w.shape[-1]), w.dtype)
    return lax.dynamic_update_slice_in_dim(out, w, offset, axis=-2)


def _hg_cols(w):
    return jnp.moveaxis(w.reshape(w.shape[:-1] + (N_HG, HG_LANES)), -2, 0)


def kernel(x_prompt, x_sample, state_rwkv, c, c_ctx, norm1_g, norm2_g, w_mod, b_mod, w_in, b_in, rwkv_w0, rwkv_w_up,
           rwkv_a0, rwkv_a_up, rwkv_g_up, rwkv_k_k, rwkv_k_a, rwkv_r_k, rwkv_gn_g, rwkv_gn_b, rwkv_w_o, conv_dw,
           conv_dw_b, conv_ln_g, conv_ln_b, conv_w_o, sgu_ln_g, sgu_ln_b, sgu_w_s, sgu_b_s, sgu_w_o, pool_w,
           pool_scale, pool_w_o, w_out, moe_w_gate, moe_w_up, moe_w_down, router_w, router_b, final_norm_g):
    n_ctx, t_ctx, d = x_prompt.shape
    n_lat, t_lat, _ = x_sample.shape
    n_layers = w_in.shape[0]
    assert d == D_MODEL and t_ctx % TILE == 0 and t_lat % TILE == 0 and n_lat < MOD_ROWS
    ctx_rows, lat_rows = n_ctx * t_ctx, n_lat * t_lat
    assert ctx_rows % t_lat == 0 and ctx_rows % MOE_TILE == 0 and lat_rows % MOE_TILE == 0
    assert t_lat % PROJ_TILE == 0 and MOE_TILE % PROJ_TILE == 0
    ctx_tiles = ctx_rows // TILE
    seq_tiles_ctx, seq_tiles_lat = t_ctx // TILE, t_lat // TILE

    def mod_row_for(tile_rows):
        ctx_t = ctx_rows // tile_rows
        per_lat = t_lat // tile_rows
        return lambda i: jnp.where(i < ctx_t, n_lat, jnp.maximum(i - ctx_t, 0) // per_lat)

    mod_row = mod_row_for(TILE)
    mod_row_proj = mod_row_for(PROJ_TILE)
    mod_row_moe = mod_row_for(MOE_TILE)

    cond = jnp.zeros((MOD_ROWS, d), F32).at[:n_lat].set(c).at[n_lat].set(c_ctx)
    mod = _modulation(cond, w_mod, b_mod).reshape(n_layers, MOD_ROWS, 1, 6 * d)

    x = _embed(x_prompt.reshape(ctx_rows, d), x_sample.reshape(lat_rows, d), _grid_pos_embed(t_lat), seq_tiles_lat)

    n_rw = RW_COLS
    n_mx = MX_COLS
    wa = w_in[:, :, :n_rw].astype(BF16)
    ba = b_in[:, None, :n_rw]
    wm = w_in[:, :, n_rw:n_rw + n_mx].astype(BF16)
    bm = b_in[:, None, n_rw:n_rw + n_mx]
    wgt = w_in[:, :, n_rw + n_mx:].astype(BF16)
    bgt = b_in[:, None, n_rw + n_mx:]

    par = jnp.zeros((n_layers, P_ROWS, RWKV_WIDTH), F32)
    par = par.at[:, P_W0:P_W0 + 2].set(rwkv_w0).at[:, P_A0:P_A0 + 2].set(rwkv_a0)
    par = par.at[:, P_KK].set(rwkv_k_k).at[:, P_KA].set(rwkv_k_a)
    par = par.at[:, P_RK].set(rwkv_r_k.reshape(n_layers, RWKV_WIDTH))
    par = par.at[:, P_GNG].set(rwkv_gn_g).at[:, P_GNB].set(rwkv_gn_b)
    par = jnp.moveaxis(_hg_cols(par), 0, 1)
    lr_k = RWKV_DECAY_RANK + RWKV_ICLR_RANK
    wup = jnp.moveaxis(_hg_cols(_pad_rows(rwkv_w_up, lr_k, 0)), 0, 2).astype(BF16)
    aup = jnp.moveaxis(_hg_cols(_pad_rows(rwkv_a_up, lr_k, RWKV_DECAY_RANK)), 0, 2).astype(BF16)
    gup = jnp.moveaxis(_hg_cols(rwkv_g_up), 0, 1).astype(BF16)
    rw_consts = _rwkv_consts()

    mix_vecs = jnp.zeros((n_layers, 8, CONV_WIDTH), F32)
    for row, vec in enumerate((conv_dw_b, conv_ln_g, conv_ln_b, sgu_ln_g, sgu_ln_b, pool_scale)):
        mix_vecs = mix_vecs.at[:, row].set(vec)
    sgu_wcat = jnp.transpose(sgu_w_s, (0, 2, 1, 3)).reshape(n_layers, SGU_CHUNK, SGU_GROUPS * SGU_CHUNK).astype(BF16)
    sgu_bs = jnp.repeat(jnp.swapaxes(sgu_b_s, 1, 2), SGU_WIDTH // SGU_GROUPS, axis=2)
    pc = POOL_WIDTH // POOL_GROUPS
    pool_bd = (pool_w[:, :, :, None, :] * jnp.eye(POOL_GROUPS, dtype=F32)[None, :, None, :, None])
    pool_bd = pool_bd.reshape(n_layers, POOL_WIDTH, POOL_WIDTH).astype(BF16)
    lane_grp = jnp.arange(SGU_WIDTH)[None, :] // (SGU_WIDTH // SGU_GROUPS)
    sgm = (jnp.arange(SGU_GROUPS * SGU_CHUNK)[:, None] // SGU_CHUNK == lane_grp).astype(BF16)
    wlen = TILE + 2 * HALO
    plm = (jnp.arange(POOL_GROUPS * wlen)[:, None] // wlen == jnp.arange(POOL_WIDTH)[None, :] // pc).astype(BF16)
    pool_band, pool_cnt = _pool_tables()

    wro, wco, wso, wpo, wout = (w.astype(BF16) for w in (rwkv_w_o, conv_w_o, sgu_w_o, pool_w_o, w_out))
    rw_t = jnp.pad(router_w, ((0, 0), (0, ROUTER_LANES - N_EXPERTS)))
    rb = router_b[:, None]

    ctx_s0 = jnp.zeros((n_ctx, N_HG, 2, RWKV_HEAD_DIM, HG_LANES), F32)
    ctx_states = []
    for l in range(n_layers):
        zr, zm = _in_projection(x, mod[l], mod_row_proj, norm1_g[l][None], wa[l], ba[l], wm[l], bm[l])
        rw_args = (par[l], wup[l], aup[l], gup[l], rw_consts)
        ya, s_fin = _rwkv_mixer(zr, ctx_s0, n_ctx, t_ctx, 0, None, *rw_args)
        ya, _ = _rwkv_mixer(zr, _states_to_kernel(state_rwkv[:, l]), n_lat, t_lat, ctx_rows // t_lat, ya, *rw_args)
        ctx_states.append(_states_from_kernel(s_fin))
        mx = _mixers(zm, ctx_tiles, seq_tiles_ctx, seq_tiles_lat, conv_dw[l], mix_vecs[l], sgu_wcat[l], sgu_bs[l],
                     pool_bd[l], sgm, plm, pool_band, pool_cnt)
        x1, h2, gates_t = _merge(x, mod[l], mod_row, norm1_g[l][None], norm2_g[l][None], ya, mx, wgt[l], bgt[l],
                                 wro[l], wco[l], wso[l], wpo[l], wout[l], rw_t, rb)
        x = _moe(h2, gates_t.T, x1, mod[l], mod_row_moe, moe_w_gate[l], moe_w_up[l], moe_w_down[l])

    g_fin = final_norm_g[None]
    y_prompt = _final_norm(x, g_fin, 0, ctx_tiles).reshape(n_ctx, t_ctx, d)
    y_sample = _final_norm(x, g_fin, ctx_tiles, lat_rows // TILE).reshape(n_lat, t_lat, d)
    new_state = jnp.stack(ctx_states, axis=1).astype(x_prompt.dtype)
    return (y_prompt, y_sample, new_state)
```

```python
import functools
import math

import jax
import jax.numpy as jnp
from jax import lax
from jax.experimental import pallas as pl
from jax.experimental.pallas import tpu as pltpu

F32 = jnp.float32
BF16 = jnp.bfloat16

D_MODEL = 1024
GRID_W = 64
RWKV_HEADS = 8
RWKV_HEAD_DIM = 64
RWKV_WIDTH = RWKV_HEADS * RWKV_HEAD_DIM
RWKV_DECAY_RANK = 64
RWKV_ICLR_RANK = 64
RWKV_GATE_RANK = 128
RWKV_GN_EPS = 64e-5
CONV_WIDTH = 256
CONV_KERNEL = 31
SGU_GROUPS = 4
SGU_WIDTH = 256
SGU_CHUNK = 128
POOL_GROUPS = 4
POOL_WIDTH = 256
POOL_WINDOWS = (2, 4, 8, 16)
N_BRANCHES = 4
N_EXPERTS = 16
TOP_K = 2
N_EXPERT_GROUPS = 4
EXPERTS_PER_GROUP = N_EXPERTS // N_EXPERT_GROUPS
D_EXPERT = 512
NORM_EPS = 1e-6
LN_EPS = 1e-5

RW_COLS = 3 * RWKV_WIDTH + RWKV_DECAY_RANK + RWKV_ICLR_RANK + RWKV_GATE_RANK
MX_COLS = 2 * CONV_WIDTH + 2 * SGU_WIDTH + POOL_WIDTH
MX_OUT = CONV_WIDTH + SGU_WIDTH + POOL_WIDTH

TILE = 256
PROJ_TILE = 512
HALO = 16
SUBLANES = 8
SHIFTED_ROWS = TILE + 2 * HALO - SUBLANES
HG_LANES = 256
HEADS_PER_HG = HG_LANES // RWKV_HEAD_DIM
N_HG = RWKV_WIDTH // HG_LANES
SCAN_CHUNK = 64
assert SCAN_CHUNK == RWKV_HEAD_DIM
CHUNKS_PER_TRIP = 2
DECAY_SCALE = math.exp(-0.5)
MOD_ROWS = 16
ROUTER_LANES = 128

VMEM_LIMIT = 48 * 1024 * 1024
RWKV_VMEM_LIMIT = 56 * 1024 * 1024


def _cparams(sem, vmem_limit=VMEM_LIMIT):
    return pltpu.CompilerParams(dimension_semantics=sem, vmem_limit_bytes=vmem_limit)


def _dot(a, b):
    return jnp.dot(a, b, preferred_element_type=F32)


def _split2(x):
    hi = x.astype(BF16)
    lo = (x - hi.astype(F32)).astype(BF16)
    return hi, lo


def _dot3(a, b):
    a_hi, a_lo = _split2(a)
    b_hi, b_lo = _split2(b)
    return _dot(a_hi, b_hi) + (_dot(a_lo, b_hi) + _dot(a_hi, b_lo))


def _dot_exact_rhs(x, rhs):
    hi, lo = _split2(x)
    return _dot(hi, rhs) + _dot(lo, rhs)


def _dot_exact_lhs(lhs, x):
    hi, lo = _split2(x)
    return _dot(lhs, hi) + _dot(lhs, lo)


def _sigmoid(x):
    return 1.0 / (1.0 + jnp.exp(-x))


def _silu(x):
    return x * _sigmoid(x)


def _norm_mod(x, g, shift, scale):
    y = x * lax.rsqrt(jnp.mean(x * x, axis=-1, keepdims=True) + NORM_EPS) * g
    return y * (1.0 + scale) + shift


def _layer_norm(x, g, b):
    mu = jnp.mean(x, axis=-1, keepdims=True)
    xc = x - mu
    var = jnp.mean(xc * xc, axis=-1, keepdims=True)
    return xc * lax.rsqrt(var + LN_EPS) * g + b


def _const_spec(shape):
    nd = len(shape)
    return pl.BlockSpec(shape, lambda *_: (0,) * nd)


def _mod_kernel(c_ref, w_ref, b_ref, o_ref):
    o_ref[...] = _dot3(_silu(c_ref[...]), w_ref[...]) + b_ref[...]


def _modulation(cond, w_mod, b_mod):
    n_layers = w_mod.shape[0]
    tn = 1536
    return pl.pallas_call(
        _mod_kernel,
        grid=(n_layers, 6 * D_MODEL // tn),
        in_specs=[
            pl.BlockSpec((MOD_ROWS, D_MODEL), lambda l, j: (0, 0)),
            pl.BlockSpec((None, D_MODEL, tn), lambda l, j: (l, 0, j)),
            pl.BlockSpec((None, 1, tn), lambda l, j: (l, 0, j)),
        ],
        out_specs=pl.BlockSpec((None, MOD_ROWS, tn), lambda l, j: (l, 0, j)),
        out_shape=jax.ShapeDtypeStruct((n_layers, MOD_ROWS, 6 * D_MODEL), F32),
        compiler_params=_cparams(("parallel", "parallel")),
        name="modulation",
    )(cond, w_mod, b_mod.reshape(n_layers, 1, 6 * D_MODEL))


def _embed_kernel(n_ctx_tiles, xp_ref, xs_ref, pos_ref, o_ref):
    i = pl.program_id(0)

    @pl.when(i < n_ctx_tiles)
    def _():
        o_ref[...] = xp_ref[...]

    @pl.when(i >= n_ctx_tiles)
    def _():
        o_ref[...] = xs_ref[...] + pos_ref[...]


def _embed(xp, xs, pos, lat_tiles):
    n_ctx_tiles = xp.shape[0] // TILE
    n_lat_tiles = xs.shape[0] // TILE
    n_tiles = n_ctx_tiles + n_lat_tiles
    return pl.pallas_call(
        functools.partial(_embed_kernel, n_ctx_tiles),
        grid=(n_tiles,),
        in_specs=[
            pl.BlockSpec((TILE, D_MODEL), lambda i: (jnp.minimum(i, n_ctx_tiles - 1), 0)),
            pl.BlockSpec((TILE, D_MODEL), lambda i: (jnp.maximum(i - n_ctx_tiles, 0), 0)),
            pl.BlockSpec((TILE, D_MODEL), lambda i: (jnp.maximum(i - n_ctx_tiles, 0) % lat_tiles, 0)),
        ],
        out_specs=pl.BlockSpec((TILE, D_MODEL), lambda i: (i, 0)),
        out_shape=jax.ShapeDtypeStruct((n_tiles * TILE, D_MODEL), F32),
        compiler_params=_cparams(("parallel",)),
        name="embed",
    )(xp, xs, pos)


def _inproj_kernel(x_ref, mod_ref, g_ref, wa_ref, ba_ref, wm_ref, bm_ref, zr_ref, zm_ref):
    h = _norm_mod(x_ref[...], g_ref[...], mod_ref[:, 0:D_MODEL], mod_ref[:, D_MODEL:2 * D_MODEL])
    hb = h.astype(BF16)
    zr_ref[...] = _dot(hb, wa_ref[...]) + ba_ref[...]
    zm_ref[...] = (_dot(hb, wm_ref[...]) + bm_ref[...]).astype(BF16)


def _in_projection(x, mod_l, mod_row, norm_g, wa, ba, wm, bm):
    n = x.shape[0]
    tm = PROJ_TILE
    return pl.pallas_call(
        _inproj_kernel,
        grid=(n // tm,),
        in_specs=[
            pl.BlockSpec((tm, D_MODEL), lambda i: (i, 0)),
            pl.BlockSpec((None, 1, 6 * D_MODEL), lambda i: (mod_row(i), 0, 0)),
            _const_spec((1, D_MODEL)),
            _const_spec((D_MODEL, RW_COLS)),
            _const_spec((1, RW_COLS)),
            _const_spec((D_MODEL, MX_COLS)),
            _const_spec((1, MX_COLS)),
        ],
        out_specs=[
            pl.BlockSpec((tm, RW_COLS), lambda i: (i, 0)),
            pl.BlockSpec((tm, MX_COLS), lambda i: (i, 0)),
        ],
        out_shape=[
            jax.ShapeDtypeStruct((n, RW_COLS), F32),
            jax.ShapeDtypeStruct((n, MX_COLS), BF16),
        ],
        compiler_params=_cparams(("parallel",)),
        name="in_projection",
    )(x, mod_l, norm_g, wa, ba, wm, bm)


P_W0, P_A0, P_KK, P_KA, P_RK, P_GNG, P_GNB = 0, 2, 4, 5, 6, 7, 8
P_ROWS = 16


def _rep(xb, reps, mask):
    return jnp.concatenate([xb] * reps, axis=0) * mask


def _rwkv_kernel(t_len, y_all_ref, r_ref, k_ref, v_ref, lr_ref, s0_ref, par_ref, wup_ref, aup_ref, gup_ref,
                 bd_ref, cm_ref, tri_ref, eyec_ref, out_ref, sfin_ref, y_scr, st_scr):
    del y_all_ref
    c = SCAN_CHUNK
    n_chunks = t_len // c
    n_trips = n_chunks // CHUNKS_PER_TRIP
    lr_k = RWKV_DECAY_RANK + RWKV_ICLR_RANK
    probs = [(d, sl, g) for sl in range(CHUNKS_PER_TRIP) for d in (0, 1) for g in range(N_HG)]
    n_sq = int(math.log2(c))

    def lanes(g):
        return slice(g * HG_LANES, (g + 1) * HG_LANES)

    def rep(xb):
        return _rep(xb, HEADS_PER_HG, bd_ref[...])

    def seg_sums(xs, passes):
        n = xs[0].shape[0]
        if passes == 1:
            out = _dot(jnp.concatenate([x.astype(BF16) for x in xs], axis=0), bd_ref[...])
            return [out[i * n:(i + 1) * n] for i in range(len(xs))]
        parts = []
        for x in xs:
            parts.extend(_split2(x))
        out = _dot(jnp.concatenate(parts, axis=0), bd_ref[...])
        return [out[2 * i * n:(2 * i + 1) * n] + out[(2 * i + 1) * n:(2 * i + 2) * n] for i in range(len(xs))]

    def fold(x_bd):
        n = x_bd.shape[0] // HEADS_PER_HG
        return (x_bd[0:n] + x_bd[n:2 * n]) + (x_bd[2 * n:3 * n] + x_bd[3 * n:4 * n])

    def par(g, row):
        return par_ref[g, row:row + 1, :]

    st_scr[...] = s0_ref[...]

    def trip(j, first_touch):
        rows = {}
        for sl in range(CHUNKS_PER_TRIP):
            rows[0, sl] = pl.ds(pl.multiple_of((CHUNKS_PER_TRIP * j + sl) * c, c), c)
            rows[1, sl] = pl.ds(pl.multiple_of((n_chunks - 1 - CHUNKS_PER_TRIP * j - sl) * c, c), c)
        lr = {key: lr_ref[rw, 0:lr_k] for key, rw in rows.items()}
        th = {key: jnp.tanh(x).astype(BF16) for key, x in lr.items()}
        lrb = {key: x.astype(BF16) for key, x in lr.items()}
        ss = [dict(r=r_ref[rows[d, sl], lanes(g)], k=k_ref[rows[d, sl], lanes(g)], v=v_ref[rows[d, sl], lanes(g)])
              for d, sl, g in probs]

        for (d, sl, g), s in zip(probs, ss):
            s["w_pre"] = par(g, P_W0 + d) + _dot(th[d, sl], wup_ref[d, g])
            s["a_pre"] = par(g, P_A0 + d) + _dot(lrb[d, sl], aup_ref[d, g])
            s["kkr"] = s["k"] * par(g, P_KK)
        for s, ssq in zip(ss, seg_sums([s["kkr"] * s["kkr"] for s in ss], 1)):
            s["ssq"] = ssq
        for (d, sl, g), s in zip(probs, ss):
            s["lw"] = -DECAY_SCALE * _sigmoid(s["w_pre"])
            s["cum"] = _dot_exact_lhs(tri_ref[d], s["lw"])
        for (d, sl, g), s in zip(probs, ss):
            a = _sigmoid(s["a_pre"])
            kk = s["kkr"] / jnp.maximum(jnp.sqrt(s["ssq"]), 1e-12)
            kd = s["k"] * (1.0 + (a - 1.0) * par(g, P_KA))
            cum = s["cum"]
            e_incl = jnp.exp(cum)
            e_neg = jnp.exp(-cum)
            gam = e_incl[c - 1:c, :] if d == 0 else e_incl[0:1, :]
            s["at"] = (-kk * jnp.exp(cum - s["lw"])).astype(BF16)
            s["rt"] = s["r"] * e_incl
            bt = kk * a * e_neg
            kt = kd * e_neg
            s["vb"] = s["v"].astype(BF16)
            s["bonus"] = s["r"] * kd * par(g, P_RK)
            s["gam_rows"] = _dot_exact_rhs(eyec_ref[...] * gam, bd_ref[...])
            tr = jnp.concatenate([bt, kt], axis=0).T
            swapped = pltpu.roll(tr, c, axis=1)
            first_half = lax.broadcasted_iota(jnp.int32, tr.shape, 1) < c
            bt_t = jnp.where(first_half, tr, swapped).astype(BF16)
            kt_t = jnp.where(first_half, swapped, tr).astype(BF16)
            s["bt_bd"] = jnp.concatenate([bt_t, bt_t], axis=1) * bd_ref[...]
            s["kt_bd"] = jnp.concatenate([kt_t, kt_t], axis=1) * bd_ref[...]
            lhs2 = jnp.concatenate([s["at"], s["rt"].astype(BF16)], axis=0)
            s["m_b"] = _dot(lhs2, s["bt_bd"])
            s["m_k"] = _dot(lhs2, s["kt_bd"])
        for s, bsum in zip(ss, seg_sums([s["bonus"] for s in ss], 1)):
            s["bsum"] = bsum
        for (d, sl, g), s in zip(probs, ss):
            causal = cm_ref[d]
            m_b = s["m_b"] * causal
            s["x"] = m_b[0:c]
            s["a_rb"] = m_b[c:2 * c].astype(BF16)
            s["t"] = eyec_ref[...] + s["x"]
            s["bt_c"] = fold(s["bt_bd"])
            av = _dot(jnp.concatenate([(s["m_k"] * causal).astype(BF16), fold(s["kt_bd"])], axis=0), rep(s["vb"]))
            s["g1"] = av[0:c].astype(BF16)
            s["y0"] = av[c:2 * c] + s["bsum"] * s["v"]
            s["z"] = av[2 * c:2 * c + RWKV_HEAD_DIM]

        for step in range(1, n_sq):
            for s in ss:
                x_bd = rep(s["x"].astype(BF16))
                if step == 1:
                    s["x"] = _dot(s["x"].astype(BF16), x_bd)
                else:
                    both = _dot(jnp.concatenate([s["x"], s["t"]], axis=0).astype(BF16), x_bd)
                    s["x"], s["t"] = both[0:c], s["t"] + both[c:2 * c]
        for s in ss:
            s["t"] = (s["t"] + _dot(s["t"].astype(BF16), rep(s["x"].astype(BF16)))).astype(BF16)

        for s in ss:
            s["w"] = _dot(s["t"], rep(s["at"])).astype(BF16)
            s["u0"] = _dot(s["t"], rep(s["g1"])).astype(BF16)
        for s in ss:
            lhs = jnp.concatenate([s["a_rb"], s["bt_c"]], axis=0)
            with_w = _dot(lhs, rep(s["w"]))
            with_u0 = _dot(lhs, rep(s["u0"]))
            s["q"] = (s["rt"] + with_w[0:c]).astype(BF16)
            s["p"] = (eyec_ref[...] + with_w[c:c + RWKV_HEAD_DIM]).astype(BF16)
            s["y0"] = s["y0"] + with_u0[0:c]
            s["z"] = s["z"] + with_u0[c:c + RWKV_HEAD_DIM]

        for (d, sl, g), s in zip(probs, ss):
            both = _dot(jnp.concatenate([s["p"], s["q"]], axis=0), rep(st_scr[g, d].astype(BF16)))
            st_scr[g, d] = s["gam_rows"] * (both[0:RWKV_HEAD_DIM] + s["z"])
            y = both[RWKV_HEAD_DIM:RWKV_HEAD_DIM + c] + s["y0"]
            if first_touch:
                y_scr[rows[d, sl], lanes(g)] = y
            else:
                y_scr[rows[d, sl], lanes(g)] += y

    half = n_trips // 2
    lax.fori_loop(0, half, lambda j, carry: (trip(j, True), carry)[1], 0)
    lax.fori_loop(half, n_trips, lambda j, carry: (trip(j, False), carry)[1], 0)
    sfin_ref[...] = st_scr[...]

    inv_n = 1.0 / RWKV_HEAD_DIM

    def finish(j, carry):
        rows = pl.ds(pl.multiple_of(j * TILE, TILE), TILE)
        sg = _sigmoid(lr_ref[rows, lr_k:HG_LANES]).astype(BF16)
        ys = [y_scr[rows, lanes(g)] for g in range(N_HG)]
        mus = [m * inv_n for m in seg_sums(ys, 2)]
        ycs = [y - mu for y, mu in zip(ys, mus)]
        variances = [v * inv_n for v in seg_sums([yc * yc for yc in ycs], 2)]
        for g in range(N_HG):
            yn = ycs[g] * lax.rsqrt(variances[g] + RWKV_GN_EPS) * par(g, P_GNG) + par(g, P_GNB)
            out_ref[rows, lanes(g)] = (yn * _dot(sg, gup_ref[g])).astype(BF16)
        return carry

    lax.fori_loop(0, t_len // TILE, finish, 0)


def _rwkv_mixer(zr, s0, n_seq, t_len, row_off, y_all, par, wup, aup, gup, consts):
    lr_block = 3 * RWKV_WIDTH // HG_LANES
    state_spec = pl.BlockSpec((None, N_HG, 2, RWKV_HEAD_DIM, HG_LANES), lambda b: (b, 0, 0, 0, 0))
    weights = (par, wup, aup, gup)
    return pl.pallas_call(
        functools.partial(_rwkv_kernel, t_len),
        grid=(n_seq,),
        in_specs=[
            pl.BlockSpec(memory_space=pl.ANY),
            pl.BlockSpec((t_len, RWKV_WIDTH), lambda b: (b + row_off, 0)),
            pl.BlockSpec((t_len, RWKV_WIDTH), lambda b: (b + row_off, 1)),
            pl.BlockSpec((t_len, RWKV_WIDTH), lambda b: (b + row_off, 2)),
            pl.BlockSpec((t_len, HG_LANES), lambda b: (b + row_off, lr_block)),
            state_spec,
            *[_const_spec(a.shape) for a in weights],
            *[_const_spec(a.shape) for a in consts],
        ],
        out_specs=[pl.BlockSpec((t_len, RWKV_WIDTH), lambda b: (b + row_off, 0)), state_spec],
        out_shape=[
            jax.ShapeDtypeStruct((zr.shape[0], RWKV_WIDTH), BF16),
            jax.ShapeDtypeStruct((n_seq, N_HG, 2, RWKV_HEAD_DIM, HG_LANES), F32),
        ],
        scratch_shapes=[
            pltpu.VMEM((t_len, RWKV_WIDTH), F32),
            pltpu.VMEM((N_HG, 2, RWKV_HEAD_DIM, HG_LANES), F32),
        ],
        input_output_aliases={0: 0},
        compiler_params=_cparams(("parallel",), RWKV_VMEM_LIMIT),
        name="rwkv_mixer",
    )(y_all, zr, zr, zr, zr, s0, *weights, *consts)


def _rwkv_consts():
    c = SCAN_CHUNK
    ri = jnp.arange(HG_LANES)[:, None]
    ci = jnp.arange(HG_LANES)[None, :]
    bd = (ri // RWKV_HEAD_DIM == ci // RWKV_HEAD_DIM).astype(BF16)
    t = jnp.arange(c)[:, None]
    i = (jnp.arange(HEADS_PER_HG * c) % c)[None, :]
    cm = jnp.stack([jnp.concatenate([i < t, i <= t]), jnp.concatenate([i > t, i >= t])]).astype(F32)
    ti = jnp.arange(c)[None, :]
    tri = jnp.stack([ti <= t, ti >= t]).astype(BF16)
    eyec = (i == t).astype(F32)
    return bd, cm, tri, eyec


def _states_to_kernel(s):
    b = s.shape[0]
    st = s.reshape(b, 2, N_HG, HEADS_PER_HG, RWKV_HEAD_DIM, RWKV_HEAD_DIM)
    return jnp.transpose(st, (0, 2, 1, 5, 3, 4)).reshape(b, N_HG, 2, RWKV_HEAD_DIM, HG_LANES)


def _states_from_kernel(sk):
    b = sk.shape[0]
    st = sk.reshape(b, N_HG, 2, RWKV_HEAD_DIM, HEADS_PER_HG, RWKV_HEAD_DIM)
    return jnp.transpose(st, (0, 2, 1, 4, 5, 3)).reshape(b, 2, RWKV_HEADS, RWKV_HEAD_DIM, RWKV_HEAD_DIM)


def _pool_tables():
    wlen = TILE + 2 * HALO
    t_loc = jnp.arange(TILE)[:, None]
    s_rel = jnp.arange(wlen)[None, :] - HALO
    lane_g = jnp.arange(POOL_WIDTH)[None, :] // (POOL_WIDTH // POOL_GROUPS)
    bands, counts = [], []
    for first in (0, 1):
        for last in (0, 1):
            lo_seq = 0 if first else -HALO
            hi_seq = TILE if last else TILE + HALO
            band_groups = []
            cnt = jnp.zeros((TILE, POOL_WIDTH), F32)
            for gi, win in enumerate(POOL_WINDOWS):
                lo = jnp.maximum(t_loc - win // 2, lo_seq)
                hi = jnp.minimum(t_loc + win - win // 2, hi_seq)
                band_groups.append((s_rel >= lo) & (s_rel < hi))
                cnt = jnp.where(lane_g == gi, (hi - lo).astype(F32), cnt)
            bands.append(jnp.concatenate(band_groups, axis=1))
            counts.append(cnt)
    return jnp.stack(bands).astype(BF16), jnp.stack(counts)


def _mixers_kernel(ctx_tiles, seq_tiles_ctx, seq_tiles_lat, zc_ref, zp_ref, zn_ref, dw_ref, vec_ref, wcat_ref,
                   bs_ref, pw_ref, sgm_ref, plm_ref, band_ref, cnt_ref, out_ref, win_scr, shift_scr):
    i = pl.program_id(0)
    j_ctx = i % seq_tiles_ctx
    j_lat = jnp.maximum(i - ctx_tiles, 0) % seq_tiles_lat
    is_ctx = i < ctx_tiles
    first = jnp.where(is_ctx, j_ctx == 0, j_lat == 0)
    last = jnp.where(is_ctx, j_ctx == seq_tiles_ctx - 1, j_lat == seq_tiles_lat - 1)
    keep_prev = jnp.where(first, 0.0, 1.0)
    keep_next = jnp.where(last, 0.0, 1.0)
    case = 2 * first.astype(jnp.int32) + last.astype(jnp.int32)

    cw = CONV_WIDTH
    conv_b, cln_g, cln_b = vec_ref[0:1, :], vec_ref[1:2, :], vec_ref[2:3, :]
    sln_g, sln_b, pool_scale = vec_ref[3:4, :], vec_ref[4:5, :], vec_ref[5:6, :]

    glu = lambda z_ref: z_ref[:, 0:cw].astype(F32) * _sigmoid(z_ref[:, cw:2 * cw].astype(F32))
    win_scr[0:HALO, :] = glu(zp_ref) * keep_prev
    win_scr[HALO:HALO + TILE, :] = glu(zc_ref)
    win_scr[HALO + TILE:2 * HALO + TILE, :] = glu(zn_ref) * keep_next
    for r in range(1, SUBLANES):
        shift_scr[r] = win_scr[pl.ds(r, SHIFTED_ROWS), :]
    acc = jnp.zeros((TILE, cw), F32) + conv_b
    pad = CONV_KERNEL // 2
    for j in range(CONV_KERNEL):
        q, r = divmod(HALO - pad + j, SUBLANES)
        rows = pl.ds(q * SUBLANES, TILE)
        tap = win_scr[rows, :] if r == 0 else shift_scr[r, rows, :]
        acc = acc + tap * dw_ref[j:j + 1, :]
    out_ref[:, 0:cw] = _silu(_layer_norm(acc, cln_g, cln_b)).astype(BF16)

    su = zc_ref[:, 2 * cw:2 * cw + SGU_WIDTH].astype(F32)
    sv = zc_ref[:, 2 * cw + SGU_WIDTH:2 * cw + 2 * SGU_WIDTH].astype(F32)
    vn = _layer_norm(sv, sln_g, sln_b).astype(BF16)
    sgm = sgm_ref[...]
    for ch in range(TILE // SGU_CHUNK):
        rows = slice(ch * SGU_CHUNK, (ch + 1) * SGU_CHUNK)
        s = _dot(wcat_ref[...], _rep(vn[rows], SGU_GROUPS, sgm)) + bs_ref[...]
        out_ref[rows, cw:cw + SGU_WIDTH] = (su[rows] * s).astype(BF16)

    zoff = 2 * cw + 2 * SGU_WIDTH
    zcur = zc_ref[:, zoff:zoff + POOL_WIDTH]
    zw = jnp.concatenate([zp_ref[:, zoff:zoff + POOL_WIDTH], zcur, zn_ref[:, zoff:zoff + POOL_WIDTH]], axis=0)
    psum = _dot(band_ref[case], _rep(zw, POOL_GROUPS, plm_ref[...]))
    p = psum / cnt_ref[case] - zcur.astype(F32)
    pooled = _dot(p.astype(BF16), pw_ref[...]) * pool_scale
    out_ref[:, cw + SGU_WIDTH:cw + SGU_WIDTH + POOL_WIDTH] = pooled.astype(BF16)


def _mixers(zm, ctx_tiles, seq_tiles_ctx, seq_tiles_lat, *tables):
    n = zm.shape[0]
    n_tiles = n // TILE
    per = TILE // HALO
    n_halo = n // HALO
    return pl.pallas_call(
        functools.partial(_mixers_kernel, ctx_tiles, seq_tiles_ctx, seq_tiles_lat),
        grid=(n_tiles,),
        in_specs=[
            pl.BlockSpec((TILE, MX_COLS), lambda i: (i, 0)),
            pl.BlockSpec((HALO, MX_COLS), lambda i: (jnp.maximum(i * per - 1, 0), 0)),
            pl.BlockSpec((HALO, MX_COLS), lambda i: (jnp.minimum((i + 1) * per, n_halo - 1), 0)),
            *[_const_spec(a.shape) for a in tables],
        ],
        out_specs=pl.BlockSpec((TILE, MX_OUT), lambda i: (i, 0)),
        out_shape=jax.ShapeDtypeStruct((n, MX_OUT), BF16),
        scratch_shapes=[
            pltpu.VMEM((TILE + 2 * HALO, CONV_WIDTH), F32),
            pltpu.VMEM((SUBLANES, SHIFTED_ROWS, CONV_WIDTH), F32),
        ],
        compiler_params=_cparams(("parallel",)),
        name="mixers",
    )(zm, zm, zm, *tables)


def _route(logits_t):
    m = jnp.max(logits_t, axis=0, keepdims=True)
    e = jnp.exp(logits_t - m)
    p = e / jnp.sum(e, axis=0, keepdims=True)
    best_score = None
    best = None
    for g in range(N_EXPERT_GROUPS):
        rows = [p[g * EXPERTS_PER_GROUP + q:g * EXPERTS_PER_GROUP + q + 1] for q in range(EXPERTS_PER_GROUP)]
        score = None
        for a in range(EXPERTS_PER_GROUP):
            for b in range(a + 1, EXPERTS_PER_GROUP):
                pair = rows[a] + rows[b]
                score = pair if score is None else jnp.maximum(score, pair)
        if g == 0:
            best_score, best = score, jnp.zeros(score.shape, jnp.int32)
        else:
            upd = score > best_score
            best = jnp.where(upd, g, best)
            best_score = jnp.where(upd, score, best_score)
    eidx = lax.broadcasted_iota(jnp.int32, p.shape, 0)
    neg = -jnp.inf
    masked = jnp.where(eidx // EXPERTS_PER_GROUP == best, p, neg)
    m1 = jnp.max(masked, axis=0, keepdims=True)
    i1 = jnp.min(jnp.where(masked == m1, eidx, N_EXPERTS), axis=0, keepdims=True)
    masked2 = jnp.where(eidx == i1, neg, masked)
    m2 = jnp.max(masked2, axis=0, keepdims=True)
    i2 = jnp.min(jnp.where(masked2 == m2, eidx, N_EXPERTS), axis=0, keepdims=True)
    tot = m1 + m2
    return jnp.concatenate([i1, i2], axis=0), jnp.concatenate([m1 / tot, m2 / tot], axis=0)


def _merge_kernel(x_ref, mod_ref, n1_ref, n2_ref, ya_ref, mx_ref, wg_ref, bg_ref, wro_ref, wco_ref, wso_ref,
                  wpo_ref, wout_ref, rw_ref, rb_ref, x1_ref, h2_ref, ri_ref, rwt_ref):
    d = D_MODEL
    x = x_ref[...]
    hb = _norm_mod(x, n1_ref[...], mod_ref[:, 0:d], mod_ref[:, d:2 * d]).astype(BF16)
    cw = CONV_WIDTH
    branches = (
        (ya_ref[...], wro_ref),
        (mx_ref[:, 0:cw], wco_ref),
        (mx_ref[:, cw:cw + SGU_WIDTH], wso_ref),
        (mx_ref[:, cw + SGU_WIDTH:MX_OUT], wpo_ref),
    )
    merged = None
    for bi, (y_in, w_ref) in enumerate(branches):
        zg = _dot(hb, wg_ref[:, bi * d:(bi + 1) * d]) + bg_ref[:, bi * d:(bi + 1) * d]
        term = _sigmoid(zg) * _dot(y_in, w_ref[...])
        merged = term if merged is None else merged + term
    x1 = x + mod_ref[:, 2 * d:3 * d] * _dot(merged.astype(BF16), wout_ref[...])
    x1_ref[...] = x1
    h2 = _norm_mod(x1, n2_ref[...], mod_ref[:, 3 * d:4 * d], mod_ref[:, 4 * d:5 * d])
    h2_ref[...] = h2
    logits = _dot3(h2, rw_ref[...])
    ri_ref[...], rwt_ref[...] = _route(logits.T[0:N_EXPERTS] + rb_ref[...])


def _merge(x, mod_l, mod_row, n1, n2, ya, mx, wg, bg, wro, wco, wso, wpo, wout, rw_t, rb):
    n = x.shape[0]
    tm = TILE
    consts = (n1, n2)
    weights = (wg, bg, wro, wco, wso, wpo, wout, rw_t, rb)
    return pl.pallas_call(
        _merge_kernel,
        grid=(n // tm,),
        in_specs=[
            pl.BlockSpec((tm, D_MODEL), lambda i: (i, 0)),
            pl.BlockSpec((None, 1, 6 * D_MODEL), lambda i: (mod_row(i), 0, 0)),
            *[_const_spec(a.shape) for a in consts],
            pl.BlockSpec((tm, RWKV_WIDTH), lambda i: (i, 0)),
            pl.BlockSpec((tm, MX_OUT), lambda i: (i, 0)),
            *[_const_spec(a.shape) for a in weights],
        ],
        out_specs=[
            pl.BlockSpec((tm, D_MODEL), lambda i: (i, 0)),
            pl.BlockSpec((tm, D_MODEL), lambda i: (i, 0)),
            pl.BlockSpec((TOP_K, tm), lambda i: (0, i)),
            pl.BlockSpec((TOP_K, tm), lambda i: (0, i)),
        ],
        out_shape=[
            jax.ShapeDtypeStruct((n, D_MODEL), F32),
            jax.ShapeDtypeStruct((n, D_MODEL), F32),
            jax.ShapeDtypeStruct((TOP_K, n), jnp.int32),
            jax.ShapeDtypeStruct((TOP_K, n), F32),
        ],
        compiler_params=_cparams(("parallel",)),
        name="merge_router",
    )(x, mod_l, n1, n2, ya, mx, *weights)


PAIRS_PER_GROUP = EXPERTS_PER_GROUP * (EXPERTS_PER_GROUP - 1) // 2
N_PAIR_CLASSES = N_EXPERT_GROUPS * PAIRS_PER_GROUP
ROUTED_TILE = 256


def _dispatch_plan(route_i, route_w):
    n = route_i.shape[1]
    tm = ROUTED_TILE
    n_tiles = n // tm + N_PAIR_CLASSES
    i1, i2 = route_i[0], route_i[1]
    first_lower = i1 < i2
    lo, hi = jnp.minimum(i1, i2), jnp.maximum(i1, i2)
    w_lo = jnp.where(first_lower, route_w[0], route_w[1])
    w_hi = jnp.where(first_lower, route_w[1], route_w[0])
    a, b = lo % EXPERTS_PER_GROUP, hi % EXPERTS_PER_GROUP
    pair = a * (2 * EXPERTS_PER_GROUP - 1 - a) // 2 + (b - a - 1)
    cls = (lo // EXPERTS_PER_GROUP) * PAIRS_PER_GROUP + pair
    onehot = (cls[:, None] == jnp.arange(N_PAIR_CLASSES)[None, :]).astype(jnp.int32)
    counts = jnp.sum(onehot, axis=0)
    rank = jnp.sum((jnp.cumsum(onehot, axis=0) - onehot) * onehot, axis=1)
    padded = (counts + tm - 1) // tm * tm
    ends = jnp.cumsum(padded)
    dest = (ends - padded)[cls] + rank
    src = jnp.zeros((n_tiles * tm,), jnp.int32).at[dest].set(jnp.arange(n, dtype=jnp.int32))
    w_sorted = jnp.zeros((n_tiles * tm, TOP_K), F32).at[dest].set(jnp.stack([w_lo, w_hi], axis=1))
    tile_row0 = jnp.arange(n_tiles, dtype=jnp.int32) * tm
    tile_valid = (tile_row0 < ends[-1]).astype(jnp.int32)
    tile_cls = jnp.minimum(jnp.searchsorted(ends, tile_row0, side="right"), N_PAIR_CLASSES - 1)
    cls_ids = jnp.arange(N_PAIR_CLASSES)
    pa = jnp.array([x for x in range(EXPERTS_PER_GROUP) for _ in range(x + 1, EXPERTS_PER_GROUP)], jnp.int32)
    pb = jnp.array([y for x in range(EXPERTS_PER_GROUP) for y in range(x + 1, EXPERTS_PER_GROUP)], jnp.int32)
    cls_lo = (cls_ids // PAIRS_PER_GROUP) * EXPERTS_PER_GROUP + pa[cls_ids % PAIRS_PER_GROUP]
    cls_hi = (cls_ids // PAIRS_PER_GROUP) * EXPERTS_PER_GROUP + pb[cls_ids % PAIRS_PER_GROUP]
    last_cls = tile_cls[jnp.maximum(ends[-1] // tm - 1, 0)]
    tile_cls = jnp.where(tile_valid == 1, tile_cls, last_cls)
    return dest.astype(jnp.int32), src, w_sorted, cls_lo[tile_cls].astype(jnp.int32), \
        cls_hi[tile_cls].astype(jnp.int32), tile_valid


def _start_row_gather(idx_ref, idx0, rows_hbm, buf, sem, n_rows):
    def start(r, carry):
        pltpu.make_async_copy(rows_hbm.at[pl.ds(idx_ref[idx0 + r], 1)], buf.at[pl.ds(r, 1)], sem).start()
        return carry
    lax.fori_loop(0, n_rows, start, 0, unroll=8)


def _wait_row_gather(rows_hbm, buf, sem, n_rows):
    pltpu.make_async_copy(rows_hbm.at[pl.ds(0, n_rows)], buf, sem).wait()


def _pipelined_gather(idx_ref, rows_hbm, bufs, sems, n_rows):
    i = pl.program_id(0)
    slot = i % 2

    @pl.when(i == 0)
    def _():
        _start_row_gather(idx_ref, 0, rows_hbm, bufs.at[0], sems.at[0], n_rows)

    @pl.when(i + 1 < pl.num_programs(0))
    def _():
        _start_row_gather(idx_ref, (i + 1) * n_rows, rows_hbm, bufs.at[1 - slot], sems.at[1 - slot], n_rows)

    _wait_row_gather(rows_hbm, bufs.at[slot], sems.at[slot], n_rows)
    return slot


def _experts_kernel(lo_ref, hi_ref, valid_ref, src_ref, w_ref, h_hbm, wg_lo, wu_lo, wd_lo, wg_hi, wu_hi, wd_hi,
                    o_ref, hbuf, sems):
    slot = _pipelined_gather(src_ref, h_hbm, hbuf, sems, ROUTED_TILE)
    valid = valid_ref[pl.program_id(0)] == 1

    @pl.when(valid)
    def _():
        hb = hbuf[slot].astype(BF16)

        def expert(wg, wu, wd):
            act = _silu(_dot(hb, wg[...].astype(BF16))) * _dot(hb, wu[...].astype(BF16))
            return _dot(act.astype(BF16), wd[...].astype(BF16))

        w = w_ref[...]
        o_ref[...] = w[:, 0:1] * expert(wg_lo, wu_lo, wd_lo) + w[:, 1:2] * expert(wg_hi, wu_hi, wd_hi)

    @pl.when(jnp.logical_not(valid))
    def _():
        o_ref[...] = jnp.zeros(o_ref.shape, F32)


def _experts(h2, src, w_sorted, tile_lo, tile_hi, tile_valid, layer, wg, wu, wd):
    tm = ROUTED_TILE
    n_tiles = tile_lo.shape[0]
    up_spec = lambda sel: pl.BlockSpec((None, None, D_MODEL, D_EXPERT), lambda i, lo, hi, v, s: (layer, sel(lo, hi)[i], 0, 0))
    down_spec = lambda sel: pl.BlockSpec((None, None, D_EXPERT, D_MODEL), lambda i, lo, hi, v, s: (layer, sel(lo, hi)[i], 0, 0))
    pick_lo = lambda lo, hi: lo
    pick_hi = lambda lo, hi: hi
    return pl.pallas_call(
        _experts_kernel,
        grid_spec=pltpu.PrefetchScalarGridSpec(
            num_scalar_prefetch=4,
            grid=(n_tiles,),
            in_specs=[
                pl.BlockSpec((tm, TOP_K), lambda i, *_: (i, 0)),
                pl.BlockSpec(memory_space=pl.ANY),
                up_spec(pick_lo), up_spec(pick_lo), down_spec(pick_lo),
                up_spec(pick_hi), up_spec(pick_hi), down_spec(pick_hi),
            ],
            out_specs=pl.BlockSpec((tm, D_MODEL), lambda i, *_: (i, 0)),
            scratch_shapes=[pltpu.VMEM((2, tm, D_MODEL), F32), pltpu.SemaphoreType.DMA((2,))],
        ),
        out_shape=jax.ShapeDtypeStruct((n_tiles * tm, D_MODEL), F32),
        compiler_params=_cparams(("arbitrary",)),
        name="experts",
    )(tile_lo, tile_hi, tile_valid, src, w_sorted, h2, wg, wu, wd, wg, wu, wd)


def _combine_kernel(dest_ref, x_ref, mod_ref, ff_hbm, o_ref, fbuf, sems):
    slot = _pipelined_gather(dest_ref, ff_hbm, fbuf, sems, ROUTED_TILE)
    o_ref[...] = x_ref[...] + mod_ref[:, 5 * D_MODEL:6 * D_MODEL] * fbuf[slot]


def _combine(ff_sorted, dest, x1, mod_l, mod_row):
    n = x1.shape[0]
    tm = ROUTED_TILE
    return pl.pallas_call(
        _combine_kernel,
        grid_spec=pltpu.PrefetchScalarGridSpec(
            num_scalar_prefetch=1,
            grid=(n // tm,),
            in_specs=[
                pl.BlockSpec((tm, D_MODEL), lambda i, d: (i, 0)),
                pl.BlockSpec((None, 1, 6 * D_MODEL), lambda i, d: (mod_row(i), 0, 0)),
                pl.BlockSpec(memory_space=pl.ANY),
            ],
            out_specs=pl.BlockSpec((tm, D_MODEL), lambda i, d: (i, 0)),
            scratch_shapes=[pltpu.VMEM((2, tm, D_MODEL), F32), pltpu.SemaphoreType.DMA((2,))],
        ),
        out_shape=jax.ShapeDtypeStruct((n, D_MODEL), F32),
        compiler_params=_cparams(("arbitrary",)),
        name="moe_combine",
    )(dest, x1, mod_l, ff_sorted)


def _final_norm_kernel(x_ref, g_ref, o_ref):
    x = x_ref[...]
    o_ref[...] = x * lax.rsqrt(jnp.mean(x * x, axis=-1, keepdims=True) + NORM_EPS) * g_ref[...]


def _final_norm(x, g, tile0, n_tiles):
    return pl.pallas_call(
        _final_norm_kernel,
        grid=(n_tiles,),
        in_specs=[pl.BlockSpec((TILE, D_MODEL), lambda i: (i + tile0, 0)), _const_spec((1, D_MODEL))],
        out_specs=pl.BlockSpec((TILE, D_MODEL), lambda i: (i, 0)),
        out_shape=jax.ShapeDtypeStruct((n_tiles * TILE, D_MODEL), F32),
        compiler_params=_cparams(("parallel",)),
        name="final_norm",
    )(x, g)


def _grid_pos_embed(n_tokens):
    rows = n_tokens // GRID_W
    quarter = D_MODEL // 4
    half = D_MODEL // 2
    omega = 1.0 / (10000.0 ** (jnp.arange(quarter, dtype=F32) / quarter))
    ang_r = jnp.arange(rows, dtype=F32)[:, None] * omega
    ang_c = jnp.arange(GRID_W, dtype=F32)[:, None] * omega
    emb_r = jnp.concatenate([jnp.sin(ang_r), jnp.cos(ang_r)], axis=-1)
    emb_c = jnp.concatenate([jnp.sin(ang_c), jnp.cos(ang_c)], axis=-1)
    emb = jnp.concatenate([jnp.broadcast_to(emb_r[:, None, :], (rows, GRID_W, half)),
                           jnp.broadcast_to(emb_c[None, :, :], (rows, GRID_W, half))], axis=-1)
    return emb.reshape(rows * GRID_W, D_MODEL)


def _pad_rows(w, rows, offset):
    out = jnp.zeros(w.shape[:-2] + (rows, w.shape[-1]), w.dtype)
    return lax.dynamic_update_slice_in_dim(out, w, offset, axis=-2)


def _hg_cols(w):
    return jnp.moveaxis(w.reshape(w.shape[:-1] + (N_HG, HG_LANES)), -2, 0)


def kernel(x_prompt, x_sample, state_rwkv, c, c_ctx, norm1_g, norm2_g, w_mod, b_mod, w_in, b_in, rwkv_w0, rwkv_w_up,
           rwkv_a0, rwkv_a_up, rwkv_g_up, rwkv_k_k, rwkv_k_a, rwkv_r_k, rwkv_gn_g, rwkv_gn_b, rwkv_w_o, conv_dw,
           conv_dw_b, conv_ln_g, conv_ln_b, conv_w_o, sgu_ln_g, sgu_ln_b, sgu_w_s, sgu_b_s, sgu_w_o, pool_w,
           pool_scale, pool_w_o, w_out, moe_w_gate, moe_w_up, moe_w_down, router_w, router_b, final_norm_g):
    n_ctx, t_ctx, d = x_prompt.shape
    n_lat, t_lat, _ = x_sample.shape
    n_layers = w_in.shape[0]
    assert d == D_MODEL and t_ctx % TILE == 0 and t_lat % TILE == 0 and n_lat < MOD_ROWS
    ctx_rows, lat_rows = n_ctx * t_ctx, n_lat * t_lat
    assert ctx_rows % t_lat == 0 and t_lat % PROJ_TILE == 0 and ROUTED_TILE == TILE
    ctx_tiles = ctx_rows // TILE
    seq_tiles_ctx, seq_tiles_lat = t_ctx // TILE, t_lat // TILE

    def mod_row_for(tile_rows):
        ctx_t = ctx_rows // tile_rows
        per_lat = t_lat // tile_rows
        return lambda i: jnp.where(i < ctx_t, n_lat, jnp.maximum(i - ctx_t, 0) // per_lat)

    mod_row = mod_row_for(TILE)
    mod_row_proj = mod_row_for(PROJ_TILE)

    cond = jnp.zeros((MOD_ROWS, d), F32).at[:n_lat].set(c).at[n_lat].set(c_ctx)
    mod = _modulation(cond, w_mod, b_mod).reshape(n_layers, MOD_ROWS, 1, 6 * d)

    x = _embed(x_prompt.reshape(ctx_rows, d), x_sample.reshape(lat_rows, d), _grid_pos_embed(t_lat), seq_tiles_lat)

    n_rw = RW_COLS
    n_mx = MX_COLS
    wa = w_in[:, :, :n_rw].astype(BF16)
    ba = b_in[:, None, :n_rw]
    wm = w_in[:, :, n_rw:n_rw + n_mx].astype(BF16)
    bm = b_in[:, None, n_rw:n_rw + n_mx]
    wgt = w_in[:, :, n_rw + n_mx:].astype(BF16)
    bgt = b_in[:, None, n_rw + n_mx:]

    par = jnp.zeros((n_layers, P_ROWS, RWKV_WIDTH), F32)
    par = par.at[:, P_W0:P_W0 + 2].set(rwkv_w0).at[:, P_A0:P_A0 + 2].set(rwkv_a0)
    par = par.at[:, P_KK].set(rwkv_k_k).at[:, P_KA].set(rwkv_k_a)
    par = par.at[:, P_RK].set(rwkv_r_k.reshape(n_layers, RWKV_WIDTH))
    par = par.at[:, P_GNG].set(rwkv_gn_g).at[:, P_GNB].set(rwkv_gn_b)
    par = jnp.moveaxis(_hg_cols(par), 0, 1)
    lr_k = RWKV_DECAY_RANK + RWKV_ICLR_RANK
    wup = jnp.moveaxis(_hg_cols(_pad_rows(rwkv_w_up, lr_k, 0)), 0, 2).astype(BF16)
    aup = jnp.moveaxis(_hg_cols(_pad_rows(rwkv_a_up, lr_k, RWKV_DECAY_RANK)), 0, 2).astype(BF16)
    gup = jnp.moveaxis(_hg_cols(rwkv_g_up), 0, 1).astype(BF16)
    rw_consts = _rwkv_consts()

    mix_vecs = jnp.zeros((n_layers, 8, CONV_WIDTH), F32)
    for row, vec in enumerate((conv_dw_b, conv_ln_g, conv_ln_b, sgu_ln_g, sgu_ln_b, pool_scale)):
        mix_vecs = mix_vecs.at[:, row].set(vec)
    sgu_wcat = jnp.transpose(sgu_w_s, (0, 2, 1, 3)).reshape(n_layers, SGU_CHUNK, SGU_GROUPS * SGU_CHUNK).astype(BF16)
    sgu_bs = jnp.repeat(jnp.swapaxes(sgu_b_s, 1, 2), SGU_WIDTH // SGU_GROUPS, axis=2)
    pc = POOL_WIDTH // POOL_GROUPS
    pool_bd = (pool_w[:, :, :, None, :] * jnp.eye(POOL_GROUPS, dtype=F32)[None, :, None, :, None])
    pool_bd = pool_bd.reshape(n_layers, POOL_WIDTH, POOL_WIDTH).astype(BF16)
    lane_grp = jnp.arange(SGU_WIDTH)[None, :] // (SGU_WIDTH // SGU_GROUPS)
    sgm = (jnp.arange(SGU_GROUPS * SGU_CHUNK)[:, None] // SGU_CHUNK == lane_grp).astype(BF16)
    wlen = TILE + 2 * HALO
    plm = (jnp.arange(POOL_GROUPS * wlen)[:, None] // wlen == jnp.arange(POOL_WIDTH)[None, :] // pc).astype(BF16)
    pool_band, pool_cnt = _pool_tables()

    wro, wco, wso, wpo, wout = (w.astype(BF16) for w in (rwkv_w_o, conv_w_o, sgu_w_o, pool_w_o, w_out))
    rw_t = jnp.pad(router_w, ((0, 0), (0, ROUTER_LANES - N_EXPERTS)))
    rb = router_b[:, None]

    ctx_s0 = jnp.zeros((n_ctx, N_HG, 2, RWKV_HEAD_DIM, HG_LANES), F32)
    ctx_states = []
    for l in range(n_layers):
        zr, zm = _in_projection(x, mod[l], mod_row_proj, norm1_g[l][None], wa[l], ba[l], wm[l], bm[l])
        rw_args = (par[l], wup[l], aup[l], gup[l], rw_consts)
        ya = jnp.zeros((x.shape[0], RWKV_WIDTH), BF16)
        ya, s_fin = _rwkv_mixer(zr, ctx_s0, n_ctx, t_ctx, 0, ya, *rw_args)
        ya, _ = _rwkv_mixer(zr, _states_to_kernel(state_rwkv[:, l]), n_lat, t_lat, ctx_rows // t_lat, ya, *rw_args)
        ctx_states.append(_states_from_kernel(s_fin))
        mx = _mixers(zm, ctx_tiles, seq_tiles_ctx, seq_tiles_lat, conv_dw[l], mix_vecs[l], sgu_wcat[l], sgu_bs[l],
                     pool_bd[l], sgm, plm, pool_band, pool_cnt)
        x1, h2, route_i, route_w = _merge(x, mod[l], mod_row, norm1_g[l][None], norm2_g[l][None], ya, mx, wgt[l], bgt[l],
                                 wro[l], wco[l], wso[l], wpo[l], wout[l], rw_t, rb)
        dest, src, w_sorted, tile_lo, tile_hi, tile_valid = _dispatch_plan(route_i, route_w)
        ff = _experts(h2, src, w_sorted, tile_lo, tile_hi, tile_valid, l, moe_w_gate, moe_w_up, moe_w_down)
        x = _combine(ff, dest, x1, mod[l], mod_row)

    g_fin = final_norm_g[None]
    y_prompt = _final_norm(x, g_fin, 0, ctx_tiles).reshape(n_ctx, t_ctx, d)
    y_sample = _final_norm(x, g_fin, ctx_tiles, lat_rows // TILE).reshape(n_lat, t_lat, d)
    new_state = jnp.stack(ctx_states, axis=1).astype(x_prompt.dtype)
    return (y_prompt, y_sample, new_state)
```

```python
import functools
import math

import jax
import jax.numpy as jnp
from jax import lax
from jax.experimental import pallas as pl
from jax.experimental.pallas import tpu as pltpu

F32 = jnp.float32
BF16 = jnp.bfloat16

D_MODEL = 1024
GRID_W = 64
RWKV_HEADS = 8
RWKV_HEAD_DIM = 64
RWKV_WIDTH = RWKV_HEADS * RWKV_HEAD_DIM
RWKV_DECAY_RANK = 64
RWKV_ICLR_RANK = 64
RWKV_GATE_RANK = 128
RWKV_GN_EPS = 64e-5
CONV_WIDTH = 256
CONV_KERNEL = 31
SGU_GROUPS = 4
SGU_WIDTH = 256
SGU_CHUNK = 128
POOL_GROUPS = 4
POOL_WIDTH = 256
POOL_WINDOWS = (2, 4, 8, 16)
N_BRANCHES = 4
N_EXPERTS = 16
TOP_K = 2
N_EXPERT_GROUPS = 4
EXPERTS_PER_GROUP = N_EXPERTS // N_EXPERT_GROUPS
D_EXPERT = 512
NORM_EPS = 1e-6
LN_EPS = 1e-5

RW_COLS = 3 * RWKV_WIDTH + RWKV_DECAY_RANK + RWKV_ICLR_RANK + RWKV_GATE_RANK
MX_COLS = 2 * CONV_WIDTH + 2 * SGU_WIDTH + POOL_WIDTH
MX_OUT = CONV_WIDTH + SGU_WIDTH + POOL_WIDTH

TILE = 256
PROJ_TILE = 512
HALO = 16
SUBLANES = 8
SHIFTED_ROWS = TILE + 2 * HALO - SUBLANES
HG_LANES = 256
HEADS_PER_HG = HG_LANES // RWKV_HEAD_DIM
N_HG = RWKV_WIDTH // HG_LANES
SCAN_CHUNK = 64
assert SCAN_CHUNK == RWKV_HEAD_DIM
CHUNKS_PER_TRIP = 2
DECAY_SCALE = math.exp(-0.5)
MOD_ROWS = 16
ROUTER_LANES = 128

VMEM_LIMIT = 48 * 1024 * 1024
RWKV_VMEM_LIMIT = 56 * 1024 * 1024


def _cparams(sem, vmem_limit=VMEM_LIMIT):
    return pltpu.CompilerParams(dimension_semantics=sem, vmem_limit_bytes=vmem_limit)


def _dot(a, b):
    return jnp.dot(a, b, preferred_element_type=F32)


def _split2(x):
    hi = x.astype(BF16)
    lo = (x - hi.astype(F32)).astype(BF16)
    return hi, lo


def _dot3(a, b):
    a_hi, a_lo = _split2(a)
    b_hi, b_lo = _split2(b)
    return _dot(a_hi, b_hi) + (_dot(a_lo, b_hi) + _dot(a_hi, b_lo))


def _dot_exact_rhs(x, rhs):
    hi, lo = _split2(x)
    return _dot(hi, rhs) + _dot(lo, rhs)


def _dot_exact_lhs(lhs, x):
    hi, lo = _split2(x)
    return _dot(lhs, hi) + _dot(lhs, lo)


def _sigmoid(x):
    return 1.0 / (1.0 + jnp.exp(-x))


def _silu(x):
    return x * _sigmoid(x)


def _norm_mod(x, g, shift, scale):
    y = x * lax.rsqrt(jnp.mean(x * x, axis=-1, keepdims=True) + NORM_EPS) * g
    return y * (1.0 + scale) + shift


def _layer_norm(x, g, b):
    mu = jnp.mean(x, axis=-1, keepdims=True)
    xc = x - mu
    var = jnp.mean(xc * xc, axis=-1, keepdims=True)
    return xc * lax.rsqrt(var + LN_EPS) * g + b


def _const_spec(shape):
    nd = len(shape)
    return pl.BlockSpec(shape, lambda *_: (0,) * nd)


def _mod_kernel(c_ref, w_ref, b_ref, o_ref):
    o_ref[...] = _dot3(_silu(c_ref[...]), w_ref[...]) + b_ref[...]


def _modulation(cond, w_mod, b_mod):
    n_layers = w_mod.shape[0]
    tn = 1536
    return pl.pallas_call(
        _mod_kernel,
        grid=(n_layers, 6 * D_MODEL // tn),
        in_specs=[
            pl.BlockSpec((MOD_ROWS, D_MODEL), lambda l, j: (0, 0)),
            pl.BlockSpec((None, D_MODEL, tn), lambda l, j: (l, 0, j)),
            pl.BlockSpec((None, 1, tn), lambda l, j: (l, 0, j)),
        ],
        out_specs=pl.BlockSpec((None, MOD_ROWS, tn), lambda l, j: (l, 0, j)),
        out_shape=jax.ShapeDtypeStruct((n_layers, MOD_ROWS, 6 * D_MODEL), F32),
        compiler_params=_cparams(("parallel", "parallel")),
        name="modulation",
    )(cond, w_mod, b_mod.reshape(n_layers, 1, 6 * D_MODEL))


def _embed_kernel(n_ctx_tiles, xp_ref, xs_ref, pos_ref, o_ref):
    i = pl.program_id(0)

    @pl.when(i < n_ctx_tiles)
    def _():
        o_ref[...] = xp_ref[...]

    @pl.when(i >= n_ctx_tiles)
    def _():
        o_ref[...] = xs_ref[...] + pos_ref[...]


def _embed(xp, xs, pos, lat_tiles):
    n_ctx_tiles = xp.shape[0] // TILE
    n_lat_tiles = xs.shape[0] // TILE
    n_tiles = n_ctx_tiles + n_lat_tiles
    return pl.pallas_call(
        functools.partial(_embed_kernel, n_ctx_tiles),
        grid=(n_tiles,),
        in_specs=[
            pl.BlockSpec((TILE, D_MODEL), lambda i: (jnp.minimum(i, n_ctx_tiles - 1), 0)),
            pl.BlockSpec((TILE, D_MODEL), lambda i: (jnp.maximum(i - n_ctx_tiles, 0), 0)),
            pl.BlockSpec((TILE, D_MODEL), lambda i: (jnp.maximum(i - n_ctx_tiles, 0) % lat_tiles, 0)),
        ],
        out_specs=pl.BlockSpec((TILE, D_MODEL), lambda i: (i, 0)),
        out_shape=jax.ShapeDtypeStruct((n_tiles * TILE, D_MODEL), F32),
        compiler_params=_cparams(("parallel",)),
        name="embed",
    )(xp, xs, pos)


def _inproj_kernel(x_ref, mod_ref, g_ref, wa_ref, ba_ref, wm_ref, bm_ref, zr_ref, zm_ref):
    h = _norm_mod(x_ref[...], g_ref[...], mod_ref[:, 0:D_MODEL], mod_ref[:, D_MODEL:2 * D_MODEL])
    hb = h.astype(BF16)
    zr_ref[...] = _dot(hb, wa_ref[...]) + ba_ref[...]
    zm_ref[...] = (_dot(hb, wm_ref[...]) + bm_ref[...]).astype(BF16)


def _in_projection(x, mod_l, mod_row, norm_g, wa, ba, wm, bm):
    n = x.shape[0]
    tm = PROJ_TILE
    return pl.pallas_call(
        _inproj_kernel,
        grid=(n // tm,),
        in_specs=[
            pl.BlockSpec((tm, D_MODEL), lambda i: (i, 0)),
            pl.BlockSpec((None, 1, 6 * D_MODEL), lambda i: (mod_row(i), 0, 0)),
            _const_spec((1, D_MODEL)),
            _const_spec((D_MODEL, RW_COLS)),
            _const_spec((1, RW_COLS)),
            _const_spec((D_MODEL, MX_COLS)),
            _const_spec((1, MX_COLS)),
        ],
        out_specs=[
            pl.BlockSpec((tm, RW_COLS), lambda i: (i, 0)),
            pl.BlockSpec((tm, MX_COLS), lambda i: (i, 0)),
        ],
        out_shape=[
            jax.ShapeDtypeStruct((n, RW_COLS), F32),
            jax.ShapeDtypeStruct((n, MX_COLS), BF16),
        ],
        compiler_params=_cparams(("parallel",)),
        name="in_projection",
    )(x, mod_l, norm_g, wa, ba, wm, bm)


P_W0, P_A0, P_KK, P_KA, P_RK, P_GNG, P_GNB = 0, 2, 4, 5, 6, 7, 8
P_ROWS = 16


def _rep(xb, reps, mask):
    return jnp.concatenate([xb] * reps, axis=0) * mask


def _rwkv_kernel(t_len, y_all_ref, r_ref, k_ref, v_ref, lr_ref, s0_ref, par_ref, wup_ref, aup_ref, gup_ref,
                 bd_ref, cm_ref, tri_ref, eyec_ref, out_ref, sfin_ref, y_scr, st_scr):
    del y_all_ref
    c = SCAN_CHUNK
    n_chunks = t_len // c
    n_trips = n_chunks // CHUNKS_PER_TRIP
    lr_k = RWKV_DECAY_RANK + RWKV_ICLR_RANK
    probs = [(d, sl, g) for sl in range(CHUNKS_PER_TRIP) for d in (0, 1) for g in range(N_HG)]
    n_sq = int(math.log2(c))

    def lanes(g):
        return slice(g * HG_LANES, (g + 1) * HG_LANES)

    def rep(xb):
        return _rep(xb, HEADS_PER_HG, bd_ref[...])

    def seg_sums(xs, passes):
        n = xs[0].shape[0]
        if passes == 1:
            out = _dot(jnp.concatenate([x.astype(BF16) for x in xs], axis=0), bd_ref[...])
            return [out[i * n:(i + 1) * n] for i in range(len(xs))]
        parts = []
        for x in xs:
            parts.extend(_split2(x))
        out = _dot(jnp.concatenate(parts, axis=0), bd_ref[...])
        return [out[2 * i * n:(2 * i + 1) * n] + out[(2 * i + 1) * n:(2 * i + 2) * n] for i in range(len(xs))]

    def fold(x_bd):
        n = x_bd.shape[0] // HEADS_PER_HG
        return (x_bd[0:n] + x_bd[n:2 * n]) + (x_bd[2 * n:3 * n] + x_bd[3 * n:4 * n])

    def par(g, row):
        return par_ref[g, row:row + 1, :]

    st_scr[...] = s0_ref[...]

    def trip(j, first_touch):
        rows = {}
        for sl in range(CHUNKS_PER_TRIP):
            rows[0, sl] = pl.ds(pl.multiple_of((CHUNKS_PER_TRIP * j + sl) * c, c), c)
            rows[1, sl] = pl.ds(pl.multiple_of((n_chunks - 1 - CHUNKS_PER_TRIP * j - sl) * c, c), c)
        lr = {key: lr_ref[rw, 0:lr_k] for key, rw in rows.items()}
        th = {key: jnp.tanh(x).astype(BF16) for key, x in lr.items()}
        lrb = {key: x.astype(BF16) for key, x in lr.items()}
        ss = [dict(r=r_ref[rows[d, sl], lanes(g)], k=k_ref[rows[d, sl], lanes(g)], v=v_ref[rows[d, sl], lanes(g)])
              for d, sl, g in probs]

        for (d, sl, g), s in zip(probs, ss):
            s["w_pre"] = par(g, P_W0 + d) + _dot(th[d, sl], wup_ref[d, g])
            s["a_pre"] = par(g, P_A0 + d) + _dot(lrb[d, sl], aup_ref[d, g])
            s["kkr"] = s["k"] * par(g, P_KK)
        for s, ssq in zip(ss, seg_sums([s["kkr"] * s["kkr"] for s in ss], 1)):
            s["ssq"] = ssq
        for (d, sl, g), s in zip(probs, ss):
            s["lw"] = -DECAY_SCALE * _sigmoid(s["w_pre"])
            s["cum"] = _dot_exact_lhs(tri_ref[d], s["lw"])
        for (d, sl, g), s in zip(probs, ss):
            a = _sigmoid(s["a_pre"])
            kk = s["kkr"] / jnp.maximum(jnp.sqrt(s["ssq"]), 1e-12)
            kd = s["k"] * (1.0 + (a - 1.0) * par(g, P_KA))
            cum = s["cum"]
            e_incl = jnp.exp(cum)
            e_neg = jnp.exp(-cum)
            gam = e_incl[c - 1:c, :] if d == 0 else e_incl[0:1, :]
            s["at"] = (-kk * jnp.exp(cum - s["lw"])).astype(BF16)
            s["rt"] = s["r"] * e_incl
            bt = kk * a * e_neg
            kt = kd * e_neg
            s["vb"] = s["v"].astype(BF16)
            s["bonus"] = s["r"] * kd * par(g, P_RK)
            s["gam_rows"] = _dot_exact_rhs(eyec_ref[...] * gam, bd_ref[...])
            tr = jnp.concatenate([bt, kt], axis=0).T
            swapped = pltpu.roll(tr, c, axis=1)
            first_half = lax.broadcasted_iota(jnp.int32, tr.shape, 1) < c
            bt_t = jnp.where(first_half, tr, swapped).astype(BF16)
            kt_t = jnp.where(first_half, swapped, tr).astype(BF16)
            s["bt_bd"] = jnp.concatenate([bt_t, bt_t], axis=1) * bd_ref[...]
            s["kt_bd"] = jnp.concatenate([kt_t, kt_t], axis=1) * bd_ref[...]
            lhs2 = jnp.concatenate([s["at"], s["rt"].astype(BF16)], axis=0)
            s["m_b"] = _dot(lhs2, s["bt_bd"])
            s["m_k"] = _dot(lhs2, s["kt_bd"])
        for s, bsum in zip(ss, seg_sums([s["bonus"] for s in ss], 1)):
            s["bsum"] = bsum
        for (d, sl, g), s in zip(probs, ss):
            causal = cm_ref[d]
            m_b = s["m_b"] * causal
            s["x"] = m_b[0:c]
            s["a_rb"] = m_b[c:2 * c].astype(BF16)
            s["t"] = eyec_ref[...] + s["x"]
            s["bt_c"] = fold(s["bt_bd"])
            av = _dot(jnp.concatenate([(s["m_k"] * causal).astype(BF16), fold(s["kt_bd"])], axis=0), rep(s["vb"]))
            s["g1"] = av[0:c].astype(BF16)
            s["y0"] = av[c:2 * c] + s["bsum"] * s["v"]
            s["z"] = av[2 * c:2 * c + RWKV_HEAD_DIM]

        for step in range(1, n_sq):
            for s in ss:
                x_bd = rep(s["x"].astype(BF16))
                if step == 1:
                    s["x"] = _dot(s["x"].astype(BF16), x_bd)
                else:
                    both = _dot(jnp.concatenate([s["x"], s["t"]], axis=0).astype(BF16), x_bd)
                    s["x"], s["t"] = both[0:c], s["t"] + both[c:2 * c]
        for s in ss:
            s["t"] = (s["t"] + _dot(s["t"].astype(BF16), rep(s["x"].astype(BF16)))).astype(BF16)

        for s in ss:
            s["w"] = _dot(s["t"], rep(s["at"])).astype(BF16)
            s["u0"] = _dot(s["t"], rep(s["g1"])).astype(BF16)
        for s in ss:
            lhs = jnp.concatenate([s["a_rb"], s["bt_c"]], axis=0)
            with_w = _dot(lhs, rep(s["w"]))
            with_u0 = _dot(lhs, rep(s["u0"]))
            s["q"] = (s["rt"] + with_w[0:c]).astype(BF16)
            s["p"] = (eyec_ref[...] + with_w[c:c + RWKV_HEAD_DIM]).astype(BF16)
            s["y0"] = s["y0"] + with_u0[0:c]
            s["z"] = s["z"] + with_u0[c:c + RWKV_HEAD_DIM]

        for (d, sl, g), s in zip(probs, ss):
            both = _dot(jnp.concatenate([s["p"], s["q"]], axis=0), rep(st_scr[g, d].astype(BF16)))
            st_scr[g, d] = s["gam_rows"] * (both[0:RWKV_HEAD_DIM] + s["z"])
            y = both[RWKV_HEAD_DIM:RWKV_HEAD_DIM + c] + s["y0"]
            if first_touch:
                y_scr[rows[d, sl], lanes(g)] = y
            else:
                y_scr[rows[d, sl], lanes(g)] += y

    half = n_trips // 2
    lax.fori_loop(0, half, lambda j, carry: (trip(j, True), carry)[1], 0)
    lax.fori_loop(half, n_trips, lambda j, carry: (trip(j, False), carry)[1], 0)
    sfin_ref[...] = st_scr[...]

    inv_n = 1.0 / RWKV_HEAD_DIM

    def finish(j, carry):
        rows = pl.ds(pl.multiple_of(j * TILE, TILE), TILE)
        sg = _sigmoid(lr_ref[rows, lr_k:HG_LANES]).astype(BF16)
        ys = [y_scr[rows, lanes(g)] for g in range(N_HG)]
        mus = [m * inv_n for m in seg_sums(ys, 2)]
        ycs = [y - mu for y, mu in zip(ys, mus)]
        variances = [v * inv_n for v in seg_sums([yc * yc for yc in ycs], 2)]
        for g in range(N_HG):
            yn = ycs[g] * lax.rsqrt(variances[g] + RWKV_GN_EPS) * par(g, P_GNG) + par(g, P_GNB)
            out_ref[rows, lanes(g)] = (yn * _dot(sg, gup_ref[g])).astype(BF16)
        return carry

    lax.fori_loop(0, t_len // TILE, finish, 0)


def _rwkv_mixer(zr, s0, n_seq, t_len, row_off, y_all, par, wup, aup, gup, consts):
    lr_block = 3 * RWKV_WIDTH // HG_LANES
    state_spec = pl.BlockSpec((None, N_HG, 2, RWKV_HEAD_DIM, HG_LANES), lambda b: (b, 0, 0, 0, 0))
    weights = (par, wup, aup, gup)
    return pl.pallas_call(
        functools.partial(_rwkv_kernel, t_len),
        grid=(n_seq,),
        in_specs=[
            pl.BlockSpec(memory_space=pl.ANY),
            pl.BlockSpec((t_len, RWKV_WIDTH), lambda b: (b + row_off, 0)),
            pl.BlockSpec((t_len, RWKV_WIDTH), lambda b: (b + row_off, 1)),
            pl.BlockSpec((t_len, RWKV_WIDTH), lambda b: (b + row_off, 2)),
            pl.BlockSpec((t_len, HG_LANES), lambda b: (b + row_off, lr_block)),
            state_spec,
            *[_const_spec(a.shape) for a in weights],
            *[_const_spec(a.shape) for a in consts],
        ],
        out_specs=[pl.BlockSpec((t_len, RWKV_WIDTH), lambda b: (b + row_off, 0)), state_spec],
        out_shape=[
            jax.ShapeDtypeStruct((zr.shape[0], RWKV_WIDTH), BF16),
            jax.ShapeDtypeStruct((n_seq, N_HG, 2, RWKV_HEAD_DIM, HG_LANES), F32),
        ],
        scratch_shapes=[
            pltpu.VMEM((t_len, RWKV_WIDTH), F32),
            pltpu.VMEM((N_HG, 2, RWKV_HEAD_DIM, HG_LANES), F32),
        ],
        input_output_aliases={0: 0},
        compiler_params=_cparams(("parallel",), RWKV_VMEM_LIMIT),
        name="rwkv_mixer",
    )(y_all, zr, zr, zr, zr, s0, *weights, *consts)


def _rwkv_consts():
    c = SCAN_CHUNK
    ri = jnp.arange(HG_LANES)[:, None]
    ci = jnp.arange(HG_LANES)[None, :]
    bd = (ri // RWKV_HEAD_DIM == ci // RWKV_HEAD_DIM).astype(BF16)
    t = jnp.arange(c)[:, None]
    i = (jnp.arange(HEADS_PER_HG * c) % c)[None, :]
    cm = jnp.stack([jnp.concatenate([i < t, i <= t]), jnp.concatenate([i > t, i >= t])]).astype(F32)
    ti = jnp.arange(c)[None, :]
    tri = jnp.stack([ti <= t, ti >= t]).astype(BF16)
    eyec = (i == t).astype(F32)
    return bd, cm, tri, eyec


def _states_to_kernel(s):
    b = s.shape[0]
    st = s.reshape(b, 2, N_HG, HEADS_PER_HG, RWKV_HEAD_DIM, RWKV_HEAD_DIM)
    return jnp.transpose(st, (0, 2, 1, 5, 3, 4)).reshape(b, N_HG, 2, RWKV_HEAD_DIM, HG_LANES)


def _states_from_kernel(sk):
    b = sk.shape[0]
    st = sk.reshape(b, N_HG, 2, RWKV_HEAD_DIM, HEADS_PER_HG, RWKV_HEAD_DIM)
    return jnp.transpose(st, (0, 2, 1, 4, 5, 3)).reshape(b, 2, RWKV_HEADS, RWKV_HEAD_DIM, RWKV_HEAD_DIM)


def _pool_tables():
    wlen = TILE + 2 * HALO
    t_loc = jnp.arange(TILE)[:, None]
    s_rel = jnp.arange(wlen)[None, :] - HALO
    lane_g = jnp.arange(POOL_WIDTH)[None, :] // (POOL_WIDTH // POOL_GROUPS)
    bands, counts = [], []
    for first in (0, 1):
        for last in (0, 1):
            lo_seq = 0 if first else -HALO
            hi_seq = TILE if last else TILE + HALO
            band_groups = []
            cnt = jnp.zeros((TILE, POOL_WIDTH), F32)
            for gi, win in enumerate(POOL_WINDOWS):
                lo = jnp.maximum(t_loc - win // 2, lo_seq)
                hi = jnp.minimum(t_loc + win - win // 2, hi_seq)
                band_groups.append((s_rel >= lo) & (s_rel < hi))
                cnt = jnp.where(lane_g == gi, (hi - lo).astype(F32), cnt)
            bands.append(jnp.concatenate(band_groups, axis=1))
            counts.append(cnt)
    return jnp.stack(bands).astype(BF16), jnp.stack(counts)


def _mixers_kernel(ctx_tiles, seq_tiles_ctx, seq_tiles_lat, zc_ref, zp_ref, zn_ref, dw_ref, vec_ref, wcat_ref,
                   bs_ref, pw_ref, sgm_ref, plm_ref, band_ref, cnt_ref, out_ref, win_scr, shift_scr):
    i = pl.program_id(0)
    j_ctx = i % seq_tiles_ctx
    j_lat = jnp.maximum(i - ctx_tiles, 0) % seq_tiles_lat
    is_ctx = i < ctx_tiles
    first = jnp.where(is_ctx, j_ctx == 0, j_lat == 0)
    last = jnp.where(is_ctx, j_ctx == seq_tiles_ctx - 1, j_lat == seq_tiles_lat - 1)
    keep_prev = jnp.where(first, 0.0, 1.0)
    keep_next = jnp.where(last, 0.0, 1.0)
    case = 2 * first.astype(jnp.int32) + last.astype(jnp.int32)

    cw = CONV_WIDTH
    conv_b, cln_g, cln_b = vec_ref[0:1, :], vec_ref[1:2, :], vec_ref[2:3, :]
    sln_g, sln_b, pool_scale = vec_ref[3:4, :], vec_ref[4:5, :], vec_ref[5:6, :]

    glu = lambda z_ref: z_ref[:, 0:cw].astype(F32) * _sigmoid(z_ref[:, cw:2 * cw].astype(F32))
    win_scr[0:HALO, :] = glu(zp_ref) * keep_prev
    win_scr[HALO:HALO + TILE, :] = glu(zc_ref)
    win_scr[HALO + TILE:2 * HALO + TILE, :] = glu(zn_ref) * keep_next
    for r in range(1, SUBLANES):
        shift_scr[r] = win_scr[pl.ds(r, SHIFTED_ROWS), :]
    acc = jnp.zeros((TILE, cw), F32) + conv_b
    pad = CONV_KERNEL // 2
    for j in range(CONV_KERNEL):
        q, r = divmod(HALO - pad + j, SUBLANES)
        rows = pl.ds(q * SUBLANES, TILE)
        tap = win_scr[rows, :] if r == 0 else shift_scr[r, rows, :]
        acc = acc + tap * dw_ref[j:j + 1, :]
    out_ref[:, 0:cw] = _silu(_layer_norm(acc, cln_g, cln_b)).astype(BF16)

    su = zc_ref[:, 2 * cw:2 * cw + SGU_WIDTH].astype(F32)
    sv = zc_ref[:, 2 * cw + SGU_WIDTH:2 * cw + 2 * SGU_WIDTH].astype(F32)
    vn = _layer_norm(sv, sln_g, sln_b).astype(BF16)
    sgm = sgm_ref[...]
    for ch in range(TILE // SGU_CHUNK):
        rows = slice(ch * SGU_CHUNK, (ch + 1) * SGU_CHUNK)
        s = _dot(wcat_ref[...], _rep(vn[rows], SGU_GROUPS, sgm)) + bs_ref[...]
        out_ref[rows, cw:cw + SGU_WIDTH] = (su[rows] * s).astype(BF16)

    zoff = 2 * cw + 2 * SGU_WIDTH
    zcur = zc_ref[:, zoff:zoff + POOL_WIDTH]
    zw = jnp.concatenate([zp_ref[:, zoff:zoff + POOL_WIDTH], zcur, zn_ref[:, zoff:zoff + POOL_WIDTH]], axis=0)
    psum = _dot(band_ref[case], _rep(zw, POOL_GROUPS, plm_ref[...]))
    p = psum / cnt_ref[case] - zcur.astype(F32)
    pooled = _dot(p.astype(BF16), pw_ref[...]) * pool_scale
    out_ref[:, cw + SGU_WIDTH:cw + SGU_WIDTH + POOL_WIDTH] = pooled.astype(BF16)


def _mixers(zm, ctx_tiles, seq_tiles_ctx, seq_tiles_lat, *tables):
    n = zm.shape[0]
    n_tiles = n // TILE
    per = TILE // HALO
    n_halo = n // HALO
    return pl.pallas_call(
        functools.partial(_mixers_kernel, ctx_tiles, seq_tiles_ctx, seq_tiles_lat),
        grid=(n_tiles,),
        in_specs=[
            pl.BlockSpec((TILE, MX_COLS), lambda i: (i, 0)),
            pl.BlockSpec((HALO, MX_COLS), lambda i: (jnp.maximum(i * per - 1, 0), 0)),
            pl.BlockSpec((HALO, MX_COLS), lambda i: (jnp.minimum((i + 1) * per, n_halo - 1), 0)),
            *[_const_spec(a.shape) for a in tables],
        ],
        out_specs=pl.BlockSpec((TILE, MX_OUT), lambda i: (i, 0)),
        out_shape=jax.ShapeDtypeStruct((n, MX_OUT), BF16),
        scratch_shapes=[
            pltpu.VMEM((TILE + 2 * HALO, CONV_WIDTH), F32),
            pltpu.VMEM((SUBLANES, SHIFTED_ROWS, CONV_WIDTH), F32),
        ],
        compiler_params=_cparams(("parallel",)),
        name="mixers",
    )(zm, zm, zm, *tables)


def _route(logits_t):
    m = jnp.max(logits_t, axis=0, keepdims=True)
    e = jnp.exp(logits_t - m)
    p = e / jnp.sum(e, axis=0, keepdims=True)
    best_score = None
    best = None
    for g in range(N_EXPERT_GROUPS):
        rows = [p[g * EXPERTS_PER_GROUP + q:g * EXPERTS_PER_GROUP + q + 1] for q in range(EXPERTS_PER_GROUP)]
        score = None
        for a in range(EXPERTS_PER_GROUP):
            for b in range(a + 1, EXPERTS_PER_GROUP):
                pair = rows[a] + rows[b]
                score = pair if score is None else jnp.maximum(score, pair)
        if g == 0:
            best_score, best = score, jnp.zeros(score.shape, jnp.int32)
        else:
            upd = score > best_score
            best = jnp.where(upd, g, best)
            best_score = jnp.where(upd, score, best_score)
    eidx = lax.broadcasted_iota(jnp.int32, p.shape, 0)
    neg = -jnp.inf
    masked = jnp.where(eidx // EXPERTS_PER_GROUP == best, p, neg)
    m1 = jnp.max(masked, axis=0, keepdims=True)
    i1 = jnp.min(jnp.where(masked == m1, eidx, N_EXPERTS), axis=0, keepdims=True)
    masked2 = jnp.where(eidx == i1, neg, masked)
    m2 = jnp.max(masked2, axis=0, keepdims=True)
    i2 = jnp.min(jnp.where(masked2 == m2, eidx, N_EXPERTS), axis=0, keepdims=True)
    tot = m1 + m2
    return jnp.concatenate([i1, i2], axis=0), jnp.concatenate([m1 / tot, m2 / tot], axis=0)


def _merge_kernel(x_ref, mod_ref, n1_ref, n2_ref, ya_ref, mx_ref, wg_ref, bg_ref, wro_ref, wco_ref, wso_ref,
                  wpo_ref, wout_ref, rw_ref, rb_ref, x1_ref, h2_ref, ri_ref, rwt_ref):
    d = D_MODEL
    x = x_ref[...]
    hb = _norm_mod(x, n1_ref[...], mod_ref[:, 0:d], mod_ref[:, d:2 * d]).astype(BF16)
    cw = CONV_WIDTH
    branches = (
        (ya_ref[...], wro_ref),
        (mx_ref[:, 0:cw], wco_ref),
        (mx_ref[:, cw:cw + SGU_WIDTH], wso_ref),
        (mx_ref[:, cw + SGU_WIDTH:MX_OUT], wpo_ref),
    )
    merged = None
    for bi, (y_in, w_ref) in enumerate(branches):
        zg = _dot(hb, wg_ref[:, bi * d:(bi + 1) * d]) + bg_ref[:, bi * d:(bi + 1) * d]
        term = _sigmoid(zg) * _dot(y_in, w_ref[...])
        merged = term if merged is None else merged + term
    x1 = x + mod_ref[:, 2 * d:3 * d] * _dot(merged.astype(BF16), wout_ref[...])
    x1_ref[...] = x1
    h2 = _norm_mod(x1, n2_ref[...], mod_ref[:, 3 * d:4 * d], mod_ref[:, 4 * d:5 * d])
    h2_ref[...] = h2
    logits = _dot3(h2, rw_ref[...])
    ri_ref[...], rwt_ref[...] = _route(logits.T[0:N_EXPERTS] + rb_ref[...])


def _merge(x, mod_l, mod_row, n1, n2, ya, mx, wg, bg, wro, wco, wso, wpo, wout, rw_t, rb):
    n = x.shape[0]
    tm = TILE
    consts = (n1, n2)
    weights = (wg, bg, wro, wco, wso, wpo, wout, rw_t, rb)
    return pl.pallas_call(
        _merge_kernel,
        grid=(n // tm,),
        in_specs=[
            pl.BlockSpec((tm, D_MODEL), lambda i: (i, 0)),
            pl.BlockSpec((None, 1, 6 * D_MODEL), lambda i: (mod_row(i), 0, 0)),
            *[_const_spec(a.shape) for a in consts],
            pl.BlockSpec((tm, RWKV_WIDTH), lambda i: (i, 0)),
            pl.BlockSpec((tm, MX_OUT), lambda i: (i, 0)),
            *[_const_spec(a.shape) for a in weights],
        ],
        out_specs=[
            pl.BlockSpec((tm, D_MODEL), lambda i: (i, 0)),
            pl.BlockSpec((tm, D_MODEL), lambda i: (i, 0)),
            pl.BlockSpec((TOP_K, tm), lambda i: (0, i)),
            pl.BlockSpec((TOP_K, tm), lambda i: (0, i)),
        ],
        out_shape=[
            jax.ShapeDtypeStruct((n, D_MODEL), F32),
            jax.ShapeDtypeStruct((n, D_MODEL), F32),
            jax.ShapeDtypeStruct((TOP_K, n), jnp.int32),
            jax.ShapeDtypeStruct((TOP_K, n), F32),
        ],
        compiler_params=_cparams(("parallel",)),
        name="merge_router",
    )(x, mod_l, n1, n2, ya, mx, *weights)


PAIRS_PER_GROUP = EXPERTS_PER_GROUP * (EXPERTS_PER_GROUP - 1) // 2
N_PAIR_CLASSES = N_EXPERT_GROUPS * PAIRS_PER_GROUP
ROUTED_TILE = 256


def _dispatch_plan(route_i, route_w):
    n = route_i.shape[1]
    tm = ROUTED_TILE
    n_tiles = n // tm + N_PAIR_CLASSES
    i1, i2 = route_i[0], route_i[1]
    first_lower = i1 < i2
    lo, hi = jnp.minimum(i1, i2), jnp.maximum(i1, i2)
    w_lo = jnp.where(first_lower, route_w[0], route_w[1])
    w_hi = jnp.where(first_lower, route_w[1], route_w[0])
    a, b = lo % EXPERTS_PER_GROUP, hi % EXPERTS_PER_GROUP
    pair = a * (2 * EXPERTS_PER_GROUP - 1 - a) // 2 + (b - a - 1)
    cls = (lo // EXPERTS_PER_GROUP) * PAIRS_PER_GROUP + pair
    onehot = (cls[:, None] == jnp.arange(N_PAIR_CLASSES)[None, :]).astype(jnp.int32)
    counts = jnp.sum(onehot, axis=0)
    rank = jnp.sum((jnp.cumsum(onehot, axis=0) - onehot) * onehot, axis=1)
    padded = (counts + tm - 1) // tm * tm
    ends = jnp.cumsum(padded)
    dest = (ends - padded)[cls] + rank
    tile_row0 = jnp.arange(n_tiles, dtype=jnp.int32) * tm
    tile_valid = (tile_row0 < ends[-1]).astype(jnp.int32)
    tile_cls = jnp.minimum(jnp.searchsorted(ends, tile_row0, side="right"), N_PAIR_CLASSES - 1)
    order = jnp.sort(cls * n + jnp.arange(n, dtype=jnp.int32)) % n
    row = jnp.arange(n_tiles * tm, dtype=jnp.int32)
    row_cls = jnp.repeat(tile_cls, tm)
    k = row - (ends - padded)[row_cls]
    row_valid = (k < counts[row_cls]) & (row < ends[-1])
    src = jnp.where(row_valid, order[jnp.clip((jnp.cumsum(counts) - counts)[row_cls] + k, 0, n - 1)], 0)
    w_sorted = jnp.where(row_valid[:, None], jnp.stack([w_lo, w_hi], axis=1)[src], 0.0)
    cls_ids = jnp.arange(N_PAIR_CLASSES)
    pa = jnp.array([x for x in range(EXPERTS_PER_GROUP) for _ in range(x + 1, EXPERTS_PER_GROUP)], jnp.int32)
    pb = jnp.array([y for x in range(EXPERTS_PER_GROUP) for y in range(x + 1, EXPERTS_PER_GROUP)], jnp.int32)
    cls_lo = (cls_ids // PAIRS_PER_GROUP) * EXPERTS_PER_GROUP + pa[cls_ids % PAIRS_PER_GROUP]
    cls_hi = (cls_ids // PAIRS_PER_GROUP) * EXPERTS_PER_GROUP + pb[cls_ids % PAIRS_PER_GROUP]
    last_cls = tile_cls[jnp.maximum(ends[-1] // tm - 1, 0)]
    tile_cls = jnp.where(tile_valid == 1, tile_cls, last_cls)
    return dest.astype(jnp.int32), src, w_sorted, cls_lo[tile_cls].astype(jnp.int32), cls_hi[tile_cls].astype(jnp.int32)


def _start_row_gather(idx_ref, idx0, rows_hbm, buf, sem, n_rows):
    def start(r, carry):
        pltpu.make_async_copy(rows_hbm.at[pl.ds(idx_ref[idx0 + r], 1)], buf.at[pl.ds(r, 1)], sem).start()
        return carry
    lax.fori_loop(0, n_rows, start, 0, unroll=8)


def _wait_row_gather(rows_hbm, buf, sem, n_rows):
    pltpu.make_async_copy(rows_hbm.at[pl.ds(0, n_rows)], buf, sem).wait()


def _pipelined_gather(idx_ref, rows_hbm, bufs, sems, n_rows):
    i = pl.program_id(0)
    slot = i % 2

    @pl.when(i == 0)
    def _():
        _start_row_gather(idx_ref, 0, rows_hbm, bufs.at[0], sems.at[0], n_rows)

    @pl.when(i + 1 < pl.num_programs(0))
    def _():
        _start_row_gather(idx_ref, (i + 1) * n_rows, rows_hbm, bufs.at[1 - slot], sems.at[1 - slot], n_rows)

    _wait_row_gather(rows_hbm, bufs.at[slot], sems.at[slot], n_rows)
    return slot


EXPERT_GATHER_DEPTH = 2


def _experts_kernel(lo_ref, hi_ref, src_ref, w_ref, h_hbm, wg_lo, wu_lo, wd_lo, wg_hi, wu_hi, wd_hi,
                    o_ref, hbuf, sems):
    del lo_ref, hi_ref
    tm = ROUTED_TILE
    n_slots = EXPERT_GATHER_DEPTH + 1
    i = pl.program_id(0)
    n = pl.num_programs(0)
    slot = i % n_slots

    @pl.when(i == 0)
    def _():
        for t in range(EXPERT_GATHER_DEPTH):
            @pl.when(t < n)
            def _():
                _start_row_gather(src_ref, t * tm, h_hbm, hbuf.at[t], sems.at[t], tm)

    _wait_row_gather(h_hbm, hbuf.at[slot], sems.at[slot], tm)

    def tile(prefetch):
        ahead = (i + EXPERT_GATHER_DEPTH) % n_slots
        rows = iter(range(tm))

        def request(count):
            if not prefetch:
                return
            for r in (next(rows) for _ in range(count)):
                row = src_ref[(i + EXPERT_GATHER_DEPTH) * tm + r]
                pltpu.make_async_copy(h_hbm.at[pl.ds(row, 1)], hbuf.at[ahead, pl.ds(r, 1)], sems.at[ahead]).start()

        hb = hbuf[slot].astype(BF16)
        per_dot = tm // 6

        def expert(wg, wu, wd):
            request(per_dot)
            gate = _dot(hb, wg[...].astype(BF16))
            request(per_dot)
            up = _dot(hb, wu[...].astype(BF16))
            request(per_dot)
            return _dot((_silu(gate) * up).astype(BF16), wd[...].astype(BF16))

        w = w_ref[...]
        out = w[:, 0:1] * expert(wg_lo, wu_lo, wd_lo) + w[:, 1:2] * expert(wg_hi, wu_hi, wd_hi)
        request(tm - 6 * per_dot)
        o_ref[...] = out

    @pl.when(i + EXPERT_GATHER_DEPTH < n)
    def _():
        tile(True)

    @pl.when(i + EXPERT_GATHER_DEPTH >= n)
    def _():
        tile(False)


def _experts(h2, src, w_sorted, tile_lo, tile_hi, layer, wg, wu, wd):
    tm = ROUTED_TILE
    n_tiles = tile_lo.shape[0]
    n_slots = EXPERT_GATHER_DEPTH + 1
    up_spec = lambda sel: pl.BlockSpec((None, None, D_MODEL, D_EXPERT), lambda i, lo, hi, s: (layer, sel(lo, hi)[i], 0, 0))
    down_spec = lambda sel: pl.BlockSpec((None, None, D_EXPERT, D_MODEL), lambda i, lo, hi, s: (layer, sel(lo, hi)[i], 0, 0))
    pick_lo = lambda lo, hi: lo
    pick_hi = lambda lo, hi: hi
    return pl.pallas_call(
        _experts_kernel,
        grid_spec=pltpu.PrefetchScalarGridSpec(
            num_scalar_prefetch=3,
            grid=(n_tiles,),
            in_specs=[
                pl.BlockSpec((tm, TOP_K), lambda i, *_: (i, 0)),
                pl.BlockSpec(memory_space=pl.ANY),
                up_spec(pick_lo), up_spec(pick_lo), down_spec(pick_lo),
                up_spec(pick_hi), up_spec(pick_hi), down_spec(pick_hi),
            ],
            out_specs=pl.BlockSpec((tm, D_MODEL), lambda i, *_: (i, 0)),
            scratch_shapes=[pltpu.VMEM((n_slots, tm, D_MODEL), F32), pltpu.SemaphoreType.DMA((n_slots,))],
        ),
        out_shape=jax.ShapeDtypeStruct((n_tiles * tm, D_MODEL), F32),
        compiler_params=_cparams(("arbitrary",)),
        name="experts",
    )(tile_lo, tile_hi, src, w_sorted, h2, wg, wu, wd, wg, wu, wd)


def _combine_kernel(dest_ref, x_ref, mod_ref, ff_hbm, o_ref, fbuf, sems):
    slot = _pipelined_gather(dest_ref, ff_hbm, fbuf, sems, ROUTED_TILE)
    o_ref[...] = x_ref[...] + mod_ref[:, 5 * D_MODEL:6 * D_MODEL] * fbuf[slot]


def _combine(ff_sorted, dest, x1, mod_l, mod_row):
    n = x1.shape[0]
    tm = ROUTED_TILE
    return pl.pallas_call(
        _combine_kernel,
        grid_spec=pltpu.PrefetchScalarGridSpec(
            num_scalar_prefetch=1,
            grid=(n // tm,),
            in_specs=[
                pl.BlockSpec((tm, D_MODEL), lambda i, d: (i, 0)),
                pl.BlockSpec((None, 1, 6 * D_MODEL), lambda i, d: (mod_row(i), 0, 0)),
                pl.BlockSpec(memory_space=pl.ANY),
            ],
            out_specs=pl.BlockSpec((tm, D_MODEL), lambda i, d: (i, 0)),
            scratch_shapes=[pltpu.VMEM((2, tm, D_MODEL), F32), pltpu.SemaphoreType.DMA((2,))],
        ),
        out_shape=jax.ShapeDtypeStruct((n, D_MODEL), F32),
        compiler_params=_cparams(("arbitrary",)),
        name="moe_combine",
    )(dest, x1, mod_l, ff_sorted)


def _final_norm_kernel(x_ref, g_ref, o_ref):
    x = x_ref[...]
    o_ref[...] = x * lax.rsqrt(jnp.mean(x * x, axis=-1, keepdims=True) + NORM_EPS) * g_ref[...]


def _final_norm(x, g, tile0, n_tiles):
    return pl.pallas_call(
        _final_norm_kernel,
        grid=(n_tiles,),
        in_specs=[pl.BlockSpec((TILE, D_MODEL), lambda i: (i + tile0, 0)), _const_spec((1, D_MODEL))],
        out_specs=pl.BlockSpec((TILE, D_MODEL), lambda i: (i, 0)),
        out_shape=jax.ShapeDtypeStruct((n_tiles * TILE, D_MODEL), F32),
        compiler_params=_cparams(("parallel",)),
        name="final_norm",
    )(x, g)


def _grid_pos_embed(n_tokens):
    rows = n_tokens // GRID_W
    quarter = D_MODEL // 4
    half = D_MODEL // 2
    omega = 1.0 / (10000.0 ** (jnp.arange(quarter, dtype=F32) / quarter))
    ang_r = jnp.arange(rows, dtype=F32)[:, None] * omega
    ang_c = jnp.arange(GRID_W, dtype=F32)[:, None] * omega
    emb_r = jnp.concatenate([jnp.sin(ang_r), jnp.cos(ang_r)], axis=-1)
    emb_c = jnp.concatenate([jnp.sin(ang_c), jnp.cos(ang_c)], axis=-1)
    emb = jnp.concatenate([jnp.broadcast_to(emb_r[:, None, :], (rows, GRID_W, half)),
                           jnp.broadcast_to(emb_c[None, :, :], (rows, GRID_W, half))], axis=-1)
    return emb.reshape(rows * GRID_W, D_MODEL)


def _pad_rows(w, rows, offset):
    out = jnp.zeros(w.shape[:-2] + (rows, w.shape[-1]), w.dtype)
    return lax.dynamic_update_slice_in_dim(out, w, offset, axis=-2)


def _hg_cols(w):
    return jnp.moveaxis(w.reshape(w.shape[:-1] + (N_HG, HG_LANES)), -2, 0)


def kernel(x_prompt, x_sample, state_rwkv, c, c_ctx, norm1_g, norm2_g, w_mod, b_mod, w_in, b_in, rwkv_w0, rwkv_w_up,
           rwkv_a0, rwkv_a_up, rwkv_g_up, rwkv_k_k, rwkv_k_a, rwkv_r_k, rwkv_gn_g, rwkv_gn_b, rwkv_w_o, conv_dw,
           conv_dw_b, conv_ln_g, conv_ln_b, conv_w_o, sgu_ln_g, sgu_ln_b, sgu_w_s, sgu_b_s, sgu_w_o, pool_w,
           pool_scale, pool_w_o, w_out, moe_w_gate, moe_w_up, moe_w_down, router_w, router_b, final_norm_g):
    n_ctx, t_ctx, d = x_prompt.shape
    n_lat, t_lat, _ = x_sample.shape
    n_layers = w_in.shape[0]
    assert d == D_MODEL and t_ctx % TILE == 0 and t_lat % TILE == 0 and n_lat < MOD_ROWS
    ctx_rows, lat_rows = n_ctx * t_ctx, n_lat * t_lat
    assert ctx_rows % t_lat == 0 and t_lat % PROJ_TILE == 0 and ROUTED_TILE == TILE
    ctx_tiles = ctx_rows // TILE
    seq_tiles_ctx, seq_tiles_lat = t_ctx // TILE, t_lat // TILE

    def mod_row_for(tile_rows):
        ctx_t = ctx_rows // tile_rows
        per_lat = t_lat // tile_rows
        return lambda i: jnp.where(i < ctx_t, n_lat, jnp.maximum(i - ctx_t, 0) // per_lat)

    mod_row = mod_row_for(TILE)
    mod_row_proj = mod_row_for(PROJ_TILE)

    cond = jnp.zeros((MOD_ROWS, d), F32).at[:n_lat].set(c).at[n_lat].set(c_ctx)
    mod = _modulation(cond, w_mod, b_mod).reshape(n_layers, MOD_ROWS, 1, 6 * d)

    x = _embed(x_prompt.reshape(ctx_rows, d), x_sample.reshape(lat_rows, d), _grid_pos_embed(t_lat), seq_tiles_lat)

    n_rw = RW_COLS
    n_mx = MX_COLS
    wa = w_in[:, :, :n_rw].astype(BF16)
    ba = b_in[:, None, :n_rw]
    wm = w_in[:, :, n_rw:n_rw + n_mx].astype(BF16)
    bm = b_in[:, None, n_rw:n_rw + n_mx]
    wgt = w_in[:, :, n_rw + n_mx:].astype(BF16)
    bgt = b_in[:, None, n_rw + n_mx:]

    par = jnp.zeros((n_layers, P_ROWS, RWKV_WIDTH), F32)
    par = par.at[:, P_W0:P_W0 + 2].set(rwkv_w0).at[:, P_A0:P_A0 + 2].set(rwkv_a0)
    par = par.at[:, P_KK].set(rwkv_k_k).at[:, P_KA].set(rwkv_k_a)
    par = par.at[:, P_RK].set(rwkv_r_k.reshape(n_layers, RWKV_WIDTH))
    par = par.at[:, P_GNG].set(rwkv_gn_g).at[:, P_GNB].set(rwkv_gn_b)
    par = jnp.moveaxis(_hg_cols(par), 0, 1)
    lr_k = RWKV_DECAY_RANK + RWKV_ICLR_RANK
    wup = jnp.moveaxis(_hg_cols(_pad_rows(rwkv_w_up, lr_k, 0)), 0, 2).astype(BF16)
    aup = jnp.moveaxis(_hg_cols(_pad_rows(rwkv_a_up, lr_k, RWKV_DECAY_RANK)), 0, 2).astype(BF16)
    gup = jnp.moveaxis(_hg_cols(rwkv_g_up), 0, 1).astype(BF16)
    rw_consts = _rwkv_consts()

    mix_vecs = jnp.zeros((n_layers, 8, CONV_WIDTH), F32)
    for row, vec in enumerate((conv_dw_b, conv_ln_g, conv_ln_b, sgu_ln_g, sgu_ln_b, pool_scale)):
        mix_vecs = mix_vecs.at[:, row].set(vec)
    sgu_wcat = jnp.transpose(sgu_w_s, (0, 2, 1, 3)).reshape(n_layers, SGU_CHUNK, SGU_GROUPS * SGU_CHUNK).astype(BF16)
    sgu_bs = jnp.repeat(jnp.swapaxes(sgu_b_s, 1, 2), SGU_WIDTH // SGU_GROUPS, axis=2)
    pc = POOL_WIDTH // POOL_GROUPS
    pool_bd = (pool_w[:, :, :, None, :] * jnp.eye(POOL_GROUPS, dtype=F32)[None, :, None, :, None])
    pool_bd = pool_bd.reshape(n_layers, POOL_WIDTH, POOL_WIDTH).astype(BF16)
    lane_grp = jnp.arange(SGU_WIDTH)[None, :] // (SGU_WIDTH // SGU_GROUPS)
    sgm = (jnp.arange(SGU_GROUPS * SGU_CHUNK)[:, None] // SGU_CHUNK == lane_grp).astype(BF16)
    wlen = TILE + 2 * HALO
    plm = (jnp.arange(POOL_GROUPS * wlen)[:, None] // wlen == jnp.arange(POOL_WIDTH)[None, :] // pc).astype(BF16)
    pool_band, pool_cnt = _pool_tables()

    wro, wco, wso, wpo, wout = (w.astype(BF16) for w in (rwkv_w_o, conv_w_o, sgu_w_o, pool_w_o, w_out))
    rw_t = jnp.pad(router_w, ((0, 0), (0, ROUTER_LANES - N_EXPERTS)))
    rb = router_b[:, None]

    ctx_s0 = jnp.zeros((n_ctx, N_HG, 2, RWKV_HEAD_DIM, HG_LANES), F32)
    ctx_states = []
    for l in range(n_layers):
        zr, zm = _in_projection(x, mod[l], mod_row_proj, norm1_g[l][None], wa[l], ba[l], wm[l], bm[l])
        rw_args = (par[l], wup[l], aup[l], gup[l], rw_consts)
        ya = jnp.zeros((x.shape[0], RWKV_WIDTH), BF16)
        ya, s_fin = _rwkv_mixer(zr, ctx_s0, n_ctx, t_ctx, 0, ya, *rw_args)
        ya, _ = _rwkv_mixer(zr, _states_to_kernel(state_rwkv[:, l]), n_lat, t_lat, ctx_rows // t_lat, ya, *rw_args)
        ctx_states.append(_states_from_kernel(s_fin))
        mx = _mixers(zm, ctx_tiles, seq_tiles_ctx, seq_tiles_lat, conv_dw[l], mix_vecs[l], sgu_wcat[l], sgu_bs[l],
                     pool_bd[l], sgm, plm, pool_band, pool_cnt)
        x1, h2, route_i, route_w = _merge(x, mod[l], mod_row, norm1_g[l][None], norm2_g[l][None], ya, mx, wgt[l], bgt[l],
                                 wro[l], wco[l], wso[l], wpo[l], wout[l], rw_t, rb)
        dest, src, w_sorted, tile_lo, tile_hi = _dispatch_plan(route_i, route_w)
        ff = _experts(h2, src, w_sorted, tile_lo, tile_hi, l, moe_w_gate, moe_w_up, moe_w_down)
        x = _combine(ff, dest, x1, mod[l], mod_row)

    g_fin = final_norm_g[None]
    y_prompt = _final_norm(x, g_fin, 0, ctx_tiles).reshape(n_ctx, t_ctx, d)
    y_sample = _final_norm(x, g_fin, ctx_tiles, lat_rows // TILE).reshape(n_lat, t_lat, d)
    new_state = jnp.stack(ctx_states, axis=1).astype(x_prompt.dtype)
    return (y_prompt, y_sample, new_state)
```

```python
import functools
import math

import jax
import jax.numpy as jnp
from jax import lax
from jax.experimental import pallas as pl
from jax.experimental.pallas import tpu as pltpu

F32 = jnp.float32
BF16 = jnp.bfloat16

D_MODEL = 1024
GRID_W = 64
RWKV_HEADS = 8
RWKV_HEAD_DIM = 64
RWKV_WIDTH = RWKV_HEADS * RWKV_HEAD_DIM
RWKV_DECAY_RANK = 64
RWKV_ICLR_RANK = 64
RWKV_GATE_RANK = 128
RWKV_GN_EPS = 64e-5
CONV_WIDTH = 256
CONV_KERNEL = 31
SGU_GROUPS = 4
SGU_WIDTH = 256
SGU_CHUNK = 128
POOL_GROUPS = 4
POOL_WIDTH = 256
POOL_WINDOWS = (2, 4, 8, 16)
N_BRANCHES = 4
N_EXPERTS = 16
TOP_K = 2
N_EXPERT_GROUPS = 4
EXPERTS_PER_GROUP = N_EXPERTS // N_EXPERT_GROUPS
D_EXPERT = 512
NORM_EPS = 1e-6
LN_EPS = 1e-5

RW_COLS = 3 * RWKV_WIDTH + RWKV_DECAY_RANK + RWKV_ICLR_RANK + RWKV_GATE_RANK
MX_COLS = 2 * CONV_WIDTH + 2 * SGU_WIDTH + POOL_WIDTH
MX_OUT = CONV_WIDTH + SGU_WIDTH + POOL_WIDTH

TILE = 256
PROJ_TILE = 512
HALO = 16
SUBLANES = 8
SHIFTED_ROWS = TILE + 2 * HALO - SUBLANES
HG_LANES = 256
HEADS_PER_HG = HG_LANES // RWKV_HEAD_DIM
N_HG = RWKV_WIDTH // HG_LANES
SCAN_CHUNK = 64
assert SCAN_CHUNK == RWKV_HEAD_DIM
CHUNKS_PER_TRIP = 2
DECAY_SCALE = math.exp(-0.5)
MOD_ROWS = 16
ROUTER_LANES = 128

VMEM_LIMIT = 48 * 1024 * 1024
RWKV_VMEM_LIMIT = 56 * 1024 * 1024


def _cparams(sem, vmem_limit=VMEM_LIMIT):
    return pltpu.CompilerParams(dimension_semantics=sem, vmem_limit_bytes=vmem_limit)


def _dot(a, b):
    return jnp.dot(a, b, preferred_element_type=F32)


def _split2(x):
    hi = x.astype(BF16)
    lo = (x - hi.astype(F32)).astype(BF16)
    return hi, lo


def _dot3(a, b):
    a_hi, a_lo = _split2(a)
    b_hi, b_lo = _split2(b)
    return _dot(a_hi, b_hi) + (_dot(a_lo, b_hi) + _dot(a_hi, b_lo))


def _dot_exact_rhs(x, rhs):
    hi, lo = _split2(x)
    return _dot(hi, rhs) + _dot(lo, rhs)


def _dot_exact_lhs(lhs, x):
    hi, lo = _split2(x)
    return _dot(lhs, hi) + _dot(lhs, lo)


def _sigmoid(x):
    return 1.0 / (1.0 + jnp.exp(-x))


def _silu(x):
    return x * _sigmoid(x)


def _norm_mod(x, g, shift, scale):
    y = x * lax.rsqrt(jnp.mean(x * x, axis=-1, keepdims=True) + NORM_EPS) * g
    return y * (1.0 + scale) + shift


def _layer_norm(x, g, b):
    mu = jnp.mean(x, axis=-1, keepdims=True)
    xc = x - mu
    var = jnp.mean(xc * xc, axis=-1, keepdims=True)
    return xc * lax.rsqrt(var + LN_EPS) * g + b


def _const_spec(shape):
    nd = len(shape)
    return pl.BlockSpec(shape, lambda *_: (0,) * nd)


def _mod_kernel(c_ref, w_ref, b_ref, o_ref):
    o_ref[...] = _dot3(_silu(c_ref[...]), w_ref[...]) + b_ref[...]


def _modulation(cond, w_mod, b_mod):
    n_layers = w_mod.shape[0]
    tn = 1536
    return pl.pallas_call(
        _mod_kernel,
        grid=(n_layers, 6 * D_MODEL // tn),
        in_specs=[
            pl.BlockSpec((MOD_ROWS, D_MODEL), lambda l, j: (0, 0)),
            pl.BlockSpec((None, D_MODEL, tn), lambda l, j: (l, 0, j)),
            pl.BlockSpec((None, 1, tn), lambda l, j: (l, 0, j)),
        ],
        out_specs=pl.BlockSpec((None, MOD_ROWS, tn), lambda l, j: (l, 0, j)),
        out_shape=jax.ShapeDtypeStruct((n_layers, MOD_ROWS, 6 * D_MODEL), F32),
        compiler_params=_cparams(("parallel", "parallel")),
        name="modulation",
    )(cond, w_mod, b_mod.reshape(n_layers, 1, 6 * D_MODEL))


def _embed_kernel(n_ctx_tiles, xp_ref, xs_ref, pos_ref, o_ref):
    i = pl.program_id(0)

    @pl.when(i < n_ctx_tiles)
    def _():
        o_ref[...] = xp_ref[...]

    @pl.when(i >= n_ctx_tiles)
    def _():
        o_ref[...] = xs_ref[...] + pos_ref[...]


def _embed(xp, xs, pos, lat_tiles):
    n_ctx_tiles = xp.shape[0] // TILE
    n_lat_tiles = xs.shape[0] // TILE
    n_tiles = n_ctx_tiles + n_lat_tiles
    return pl.pallas_call(
        functools.partial(_embed_kernel, n_ctx_tiles),
        grid=(n_tiles,),
        in_specs=[
            pl.BlockSpec((TILE, D_MODEL), lambda i: (jnp.minimum(i, n_ctx_tiles - 1), 0)),
            pl.BlockSpec((TILE, D_MODEL), lambda i: (jnp.maximum(i - n_ctx_tiles, 0), 0)),
            pl.BlockSpec((TILE, D_MODEL), lambda i: (jnp.maximum(i - n_ctx_tiles, 0) % lat_tiles, 0)),
        ],
        out_specs=pl.BlockSpec((TILE, D_MODEL), lambda i: (i, 0)),
        out_shape=jax.ShapeDtypeStruct((n_tiles * TILE, D_MODEL), F32),
        compiler_params=_cparams(("parallel",)),
        name="embed",
    )(xp, xs, pos)


def _inproj_kernel(x_ref, mod_ref, g_ref, wa_ref, ba_ref, wm_ref, bm_ref, zr_ref, zm_ref):
    h = _norm_mod(x_ref[...], g_ref[...], mod_ref[:, 0:D_MODEL], mod_ref[:, D_MODEL:2 * D_MODEL])
    hb = h.astype(BF16)
    zr_ref[...] = _dot(hb, wa_ref[...]) + ba_ref[...]
    zm_ref[...] = (_dot(hb, wm_ref[...]) + bm_ref[...]).astype(BF16)


def _in_projection(x, mod_l, mod_row, norm_g, wa, ba, wm, bm):
    n = x.shape[0]
    tm = PROJ_TILE
    return pl.pallas_call(
        _inproj_kernel,
        grid=(n // tm,),
        in_specs=[
            pl.BlockSpec((tm, D_MODEL), lambda i: (i, 0)),
            pl.BlockSpec((None, 1, 6 * D_MODEL), lambda i: (mod_row(i), 0, 0)),
            _const_spec((1, D_MODEL)),
            _const_spec((D_MODEL, RW_COLS)),
            _const_spec((1, RW_COLS)),
            _const_spec((D_MODEL, MX_COLS)),
            _const_spec((1, MX_COLS)),
        ],
        out_specs=[
            pl.BlockSpec((tm, RW_COLS), lambda i: (i, 0)),
            pl.BlockSpec((tm, MX_COLS), lambda i: (i, 0)),
        ],
        out_shape=[
            jax.ShapeDtypeStruct((n, RW_COLS), F32),
            jax.ShapeDtypeStruct((n, MX_COLS), BF16),
        ],
        compiler_params=_cparams(("parallel",)),
        name="in_projection",
    )(x, mod_l, norm_g, wa, ba, wm, bm)


P_W0, P_A0, P_KK, P_KA, P_RK, P_GNG, P_GNB = 0, 2, 4, 5, 6, 7, 8
P_ROWS = 16


def _rep(xb, reps, mask):
    return jnp.concatenate([xb] * reps, axis=0) * mask


def _rwkv_kernel(t_len, y_all_ref, r_ref, k_ref, v_ref, lr_ref, s0_ref, par_ref, wup_ref, aup_ref, gup_ref,
                 bd_ref, cm_ref, tri_ref, eyec_ref, out_ref, sfin_ref, y_scr, st_scr):
    del y_all_ref
    c = SCAN_CHUNK
    n_chunks = t_len // c
    n_trips = n_chunks // CHUNKS_PER_TRIP
    lr_k = RWKV_DECAY_RANK + RWKV_ICLR_RANK
    probs = [(d, sl, g) for sl in range(CHUNKS_PER_TRIP) for d in (0, 1) for g in range(N_HG)]
    n_sq = int(math.log2(c))

    def lanes(g):
        return slice(g * HG_LANES, (g + 1) * HG_LANES)

    def rep(xb):
        return _rep(xb, HEADS_PER_HG, bd_ref[...])

    def seg_sums(xs, passes):
        n = xs[0].shape[0]
        if passes == 1:
            out = _dot(jnp.concatenate([x.astype(BF16) for x in xs], axis=0), bd_ref[...])
            return [out[i * n:(i + 1) * n] for i in range(len(xs))]
        parts = []
        for x in xs:
            parts.extend(_split2(x))
        out = _dot(jnp.concatenate(parts, axis=0), bd_ref[...])
        return [out[2 * i * n:(2 * i + 1) * n] + out[(2 * i + 1) * n:(2 * i + 2) * n] for i in range(len(xs))]

    def fold(x_bd):
        n = x_bd.shape[0] // HEADS_PER_HG
        return (x_bd[0:n] + x_bd[n:2 * n]) + (x_bd[2 * n:3 * n] + x_bd[3 * n:4 * n])

    def par(g, row):
        return par_ref[g, row:row + 1, :]

    st_scr[...] = s0_ref[...]

    def trip(j, first_touch):
        rows = {}
        for sl in range(CHUNKS_PER_TRIP):
            rows[0, sl] = pl.ds(pl.multiple_of((CHUNKS_PER_TRIP * j + sl) * c, c), c)
            rows[1, sl] = pl.ds(pl.multiple_of((n_chunks - 1 - CHUNKS_PER_TRIP * j - sl) * c, c), c)
        lr = {key: lr_ref[rw, 0:lr_k] for key, rw in rows.items()}
        th = {key: jnp.tanh(x).astype(BF16) for key, x in lr.items()}
        lrb = {key: x.astype(BF16) for key, x in lr.items()}
        ss = [dict(r=r_ref[rows[d, sl], lanes(g)], k=k_ref[rows[d, sl], lanes(g)], v=v_ref[rows[d, sl], lanes(g)])
              for d, sl, g in probs]

        for (d, sl, g), s in zip(probs, ss):
            s["w_pre"] = par(g, P_W0 + d) + _dot(th[d, sl], wup_ref[d, g])
            s["a_pre"] = par(g, P_A0 + d) + _dot(lrb[d, sl], aup_ref[d, g])
            s["kkr"] = s["k"] * par(g, P_KK)
        for s, ssq in zip(ss, seg_sums([s["kkr"] * s["kkr"] for s in ss], 1)):
            s["ssq"] = ssq
        for (d, sl, g), s in zip(probs, ss):
            s["lw"] = -DECAY_SCALE * _sigmoid(s["w_pre"])
            s["cum"] = _dot_exact_lhs(tri_ref[d], s["lw"])
        for (d, sl, g), s in zip(probs, ss):
            a = _sigmoid(s["a_pre"])
            kk = s["kkr"] / jnp.maximum(jnp.sqrt(s["ssq"]), 1e-12)
            kd = s["k"] * (1.0 + (a - 1.0) * par(g, P_KA))
            cum = s["cum"]
            e_incl = jnp.exp(cum)
            e_neg = jnp.exp(-cum)
            gam = e_incl[c - 1:c, :] if d == 0 else e_incl[0:1, :]
            s["at"] = (-kk * jnp.exp(cum - s["lw"])).astype(BF16)
            s["rt"] = s["r"] * e_incl
            bt = kk * a * e_neg
            kt = kd * e_neg
            s["vb"] = s["v"].astype(BF16)
            s["bonus"] = s["r"] * kd * par(g, P_RK)
            s["gam_rows"] = _dot_exact_rhs(eyec_ref[...] * gam, bd_ref[...])
            tr = jnp.concatenate([bt, kt], axis=0).T
            swapped = pltpu.roll(tr, c, axis=1)
            first_half = lax.broadcasted_iota(jnp.int32, tr.shape, 1) < c
            bt_t = jnp.where(first_half, tr, swapped).astype(BF16)
            kt_t = jnp.where(first_half, swapped, tr).astype(BF16)
            s["bt_bd"] = jnp.concatenate([bt_t, bt_t], axis=1) * bd_ref[...]
            s["kt_bd"] = jnp.concatenate([kt_t, kt_t], axis=1) * bd_ref[...]
            lhs2 = jnp.concatenate([s["at"], s["rt"].astype(BF16)], axis=0)
            s["m_b"] = _dot(lhs2, s["bt_bd"])
            s["m_k"] = _dot(lhs2, s["kt_bd"])
        for s, bsum in zip(ss, seg_sums([s["bonus"] for s in ss], 1)):
            s["bsum"] = bsum
        for (d, sl, g), s in zip(probs, ss):
            causal = cm_ref[d]
            m_b = s["m_b"] * causal
            s["x"] = m_b[0:c]
            s["a_rb"] = m_b[c:2 * c].astype(BF16)
            s["t"] = eyec_ref[...] + s["x"]
            s["bt_c"] = fold(s["bt_bd"])
            av = _dot(jnp.concatenate([(s["m_k"] * causal).astype(BF16), fold(s["kt_bd"])], axis=0), rep(s["vb"]))
            s["g1"] = av[0:c].astype(BF16)
            s["y0"] = av[c:2 * c] + s["bsum"] * s["v"]
            s["z"] = av[2 * c:2 * c + RWKV_HEAD_DIM]

        for step in range(1, n_sq):
            for s in ss:
                x_bd = rep(s["x"].astype(BF16))
                if step == 1:
                    s["x"] = _dot(s["x"].astype(BF16), x_bd)
                else:
                    both = _dot(jnp.concatenate([s["x"], s["t"]], axis=0).astype(BF16), x_bd)
                    s["x"], s["t"] = both[0:c], s["t"] + both[c:2 * c]
        for s in ss:
            s["t"] = (s["t"] + _dot(s["t"].astype(BF16), rep(s["x"].astype(BF16)))).astype(BF16)

        for s in ss:
            s["w"] = _dot(s["t"], rep(s["at"])).astype(BF16)
            s["u0"] = _dot(s["t"], rep(s["g1"])).astype(BF16)
        for s in ss:
            lhs = jnp.concatenate([s["a_rb"], s["bt_c"]], axis=0)
            with_w = _dot(lhs, rep(s["w"]))
            with_u0 = _dot(lhs, rep(s["u0"]))
            s["q"] = (s["rt"] + with_w[0:c]).astype(BF16)
            s["p"] = (eyec_ref[...] + with_w[c:c + RWKV_HEAD_DIM]).astype(BF16)
            s["y0"] = s["y0"] + with_u0[0:c]
            s["z"] = s["z"] + with_u0[c:c + RWKV_HEAD_DIM]

        for (d, sl, g), s in zip(probs, ss):
            both = _dot(jnp.concatenate([s["p"], s["q"]], axis=0), rep(st_scr[g, d].astype(BF16)))
            st_scr[g, d] = s["gam_rows"] * (both[0:RWKV_HEAD_DIM] + s["z"])
            y = both[RWKV_HEAD_DIM:RWKV_HEAD_DIM + c] + s["y0"]
            if first_touch:
                y_scr[rows[d, sl], lanes(g)] = y
            else:
                y_scr[rows[d, sl], lanes(g)] += y

    half = n_trips // 2
    lax.fori_loop(0, half, lambda j, carry: (trip(j, True), carry)[1], 0)
    lax.fori_loop(half, n_trips, lambda j, carry: (trip(j, False), carry)[1], 0)
    sfin_ref[...] = st_scr[...]

    inv_n = 1.0 / RWKV_HEAD_DIM

    def finish(j, carry):
        rows = pl.ds(pl.multiple_of(j * TILE, TILE), TILE)
        sg = _sigmoid(lr_ref[rows, lr_k:HG_LANES]).astype(BF16)
        ys = [y_scr[rows, lanes(g)] for g in range(N_HG)]
        mus = [m * inv_n for m in seg_sums(ys, 2)]
        ycs = [y - mu for y, mu in zip(ys, mus)]
        variances = [v * inv_n for v in seg_sums([yc * yc for yc in ycs], 2)]
        for g in range(N_HG):
            yn = ycs[g] * lax.rsqrt(variances[g] + RWKV_GN_EPS) * par(g, P_GNG) + par(g, P_GNB)
            out_ref[rows, lanes(g)] = (yn * _dot(sg, gup_ref[g])).astype(BF16)
        return carry

    lax.fori_loop(0, t_len // TILE, finish, 0)


def _rwkv_mixer(zr, s0, n_seq, t_len, row_off, y_all, par, wup, aup, gup, consts):
    lr_block = 3 * RWKV_WIDTH // HG_LANES
    state_spec = pl.BlockSpec((None, N_HG, 2, RWKV_HEAD_DIM, HG_LANES), lambda b: (b, 0, 0, 0, 0))
    weights = (par, wup, aup, gup)
    return pl.pallas_call(
        functools.partial(_rwkv_kernel, t_len),
        grid=(n_seq,),
        in_specs=[
            pl.BlockSpec(memory_space=pl.ANY),
            pl.BlockSpec((t_len, RWKV_WIDTH), lambda b: (b + row_off, 0)),
            pl.BlockSpec((t_len, RWKV_WIDTH), lambda b: (b + row_off, 1)),
            pl.BlockSpec((t_len, RWKV_WIDTH), lambda b: (b + row_off, 2)),
            pl.BlockSpec((t_len, HG_LANES), lambda b: (b + row_off, lr_block)),
            state_spec,
            *[_const_spec(a.shape) for a in weights],
            *[_const_spec(a.shape) for a in consts],
        ],
        out_specs=[pl.BlockSpec((t_len, RWKV_WIDTH), lambda b: (b + row_off, 0)), state_spec],
        out_shape=[
            jax.ShapeDtypeStruct((zr.shape[0], RWKV_WIDTH), BF16),
            jax.ShapeDtypeStruct((n_seq, N_HG, 2, RWKV_HEAD_DIM, HG_LANES), F32),
        ],
        scratch_shapes=[
            pltpu.VMEM((t_len, RWKV_WIDTH), F32),
            pltpu.VMEM((N_HG, 2, RWKV_HEAD_DIM, HG_LANES), F32),
        ],
        input_output_aliases={0: 0},
        compiler_params=_cparams(("parallel",), RWKV_VMEM_LIMIT),
        name="rwkv_mixer",
    )(y_all, zr, zr, zr, zr, s0, *weights, *consts)


def _rwkv_consts():
    c = SCAN_CHUNK
    ri = jnp.arange(HG_LANES)[:, None]
    ci = jnp.arange(HG_LANES)[None, :]
    bd = (ri // RWKV_HEAD_DIM == ci // RWKV_HEAD_DIM).astype(BF16)
    t = jnp.arange(c)[:, None]
    i = (jnp.arange(HEADS_PER_HG * c) % c)[None, :]
    cm = jnp.stack([jnp.concatenate([i < t, i <= t]), jnp.concatenate([i > t, i >= t])]).astype(F32)
    ti = jnp.arange(c)[None, :]
    tri = jnp.stack([ti <= t, ti >= t]).astype(BF16)
    eyec = (i == t).astype(F32)
    return bd, cm, tri, eyec


def _states_to_kernel(s):
    b = s.shape[0]
    st = s.reshape(b, 2, N_HG, HEADS_PER_HG, RWKV_HEAD_DIM, RWKV_HEAD_DIM)
    return jnp.transpose(st, (0, 2, 1, 5, 3, 4)).reshape(b, N_HG, 2, RWKV_HEAD_DIM, HG_LANES)


def _states_from_kernel(sk):
    b = sk.shape[0]
    st = sk.reshape(b, N_HG, 2, RWKV_HEAD_DIM, HEADS_PER_HG, RWKV_HEAD_DIM)
    return jnp.transpose(st, (0, 2, 1, 4, 5, 3)).reshape(b, 2, RWKV_HEADS, RWKV_HEAD_DIM, RWKV_HEAD_DIM)


def _pool_tables():
    wlen = TILE + 2 * HALO
    t_loc = jnp.arange(TILE)[:, None]
    s_rel = jnp.arange(wlen)[None, :] - HALO
    lane_g = jnp.arange(POOL_WIDTH)[None, :] // (POOL_WIDTH // POOL_GROUPS)
    bands, counts = [], []
    for first in (0, 1):
        for last in (0, 1):
            lo_seq = 0 if first else -HALO
            hi_seq = TILE if last else TILE + HALO
            band_groups = []
            cnt = jnp.zeros((TILE, POOL_WIDTH), F32)
            for gi, win in enumerate(POOL_WINDOWS):
                lo = jnp.maximum(t_loc - win // 2, lo_seq)
                hi = jnp.minimum(t_loc + win - win // 2, hi_seq)
                band_groups.append((s_rel >= lo) & (s_rel < hi))
                cnt = jnp.where(lane_g == gi, (hi - lo).astype(F32), cnt)
            bands.append(jnp.concatenate(band_groups, axis=1))
            counts.append(cnt)
    return jnp.stack(bands).astype(BF16), jnp.stack(counts)


def _mixers_kernel(ctx_tiles, seq_tiles_ctx, seq_tiles_lat, zc_ref, zp_ref, zn_ref, dw_ref, vec_ref, wcat_ref,
                   bs_ref, pw_ref, sgm_ref, plm_ref, band_ref, cnt_ref, out_ref, win_scr, shift_scr):
    i = pl.program_id(0)
    j_ctx = i % seq_tiles_ctx
    j_lat = jnp.maximum(i - ctx_tiles, 0) % seq_tiles_lat
    is_ctx = i < ctx_tiles
    first = jnp.where(is_ctx, j_ctx == 0, j_lat == 0)
    last = jnp.where(is_ctx, j_ctx == seq_tiles_ctx - 1, j_lat == seq_tiles_lat - 1)
    keep_prev = jnp.where(first, 0.0, 1.0)
    keep_next = jnp.where(last, 0.0, 1.0)
    case = 2 * first.astype(jnp.int32) + last.astype(jnp.int32)

    cw = CONV_WIDTH
    conv_b, cln_g, cln_b = vec_ref[0:1, :], vec_ref[1:2, :], vec_ref[2:3, :]
    sln_g, sln_b, pool_scale = vec_ref[3:4, :], vec_ref[4:5, :], vec_ref[5:6, :]

    glu = lambda z_ref: z_ref[:, 0:cw].astype(F32) * _sigmoid(z_ref[:, cw:2 * cw].astype(F32))
    win_scr[0:HALO, :] = glu(zp_ref) * keep_prev
    win_scr[HALO:HALO + TILE, :] = glu(zc_ref)
    win_scr[HALO + TILE:2 * HALO + TILE, :] = glu(zn_ref) * keep_next
    for r in range(1, SUBLANES):
        shift_scr[r] = win_scr[pl.ds(r, SHIFTED_ROWS), :]
    acc = jnp.zeros((TILE, cw), F32) + conv_b
    pad = CONV_KERNEL // 2
    for j in range(CONV_KERNEL):
        q, r = divmod(HALO - pad + j, SUBLANES)
        rows = pl.ds(q * SUBLANES, TILE)
        tap = win_scr[rows, :] if r == 0 else shift_scr[r, rows, :]
        acc = acc + tap * dw_ref[j:j + 1, :]
    out_ref[:, 0:cw] = _silu(_layer_norm(acc, cln_g, cln_b)).astype(BF16)

    su = zc_ref[:, 2 * cw:2 * cw + SGU_WIDTH].astype(F32)
    sv = zc_ref[:, 2 * cw + SGU_WIDTH:2 * cw + 2 * SGU_WIDTH].astype(F32)
    vn = _layer_norm(sv, sln_g, sln_b).astype(BF16)
    sgm = sgm_ref[...]
    for ch in range(TILE // SGU_CHUNK):
        rows = slice(ch * SGU_CHUNK, (ch + 1) * SGU_CHUNK)
        s = _dot(wcat_ref[...], _rep(vn[rows], SGU_GROUPS, sgm)) + bs_ref[...]
        out_ref[rows, cw:cw + SGU_WIDTH] = (su[rows] * s).astype(BF16)

    zoff = 2 * cw + 2 * SGU_WIDTH
    zcur = zc_ref[:, zoff:zoff + POOL_WIDTH]
    zw = jnp.concatenate([zp_ref[:, zoff:zoff + POOL_WIDTH], zcur, zn_ref[:, zoff:zoff + POOL_WIDTH]], axis=0)
    psum = _dot(band_ref[case], _rep(zw, POOL_GROUPS, plm_ref[...]))
    p = psum / cnt_ref[case] - zcur.astype(F32)
    pooled = _dot(p.astype(BF16), pw_ref[...]) * pool_scale
    out_ref[:, cw + SGU_WIDTH:cw + SGU_WIDTH + POOL_WIDTH] = pooled.astype(BF16)


def _mixers(zm, ctx_tiles, seq_tiles_ctx, seq_tiles_lat, *tables):
    n = zm.shape[0]
    n_tiles = n // TILE
    per = TILE // HALO
    n_halo = n // HALO
    return pl.pallas_call(
        functools.partial(_mixers_kernel, ctx_tiles, seq_tiles_ctx, seq_tiles_lat),
        grid=(n_tiles,),
        in_specs=[
            pl.BlockSpec((TILE, MX_COLS), lambda i: (i, 0)),
            pl.BlockSpec((HALO, MX_COLS), lambda i: (jnp.maximum(i * per - 1, 0), 0)),
            pl.BlockSpec((HALO, MX_COLS), lambda i: (jnp.minimum((i + 1) * per, n_halo - 1), 0)),
            *[_const_spec(a.shape) for a in tables],
        ],
        out_specs=pl.BlockSpec((TILE, MX_OUT), lambda i: (i, 0)),
        out_shape=jax.ShapeDtypeStruct((n, MX_OUT), BF16),
        scratch_shapes=[
            pltpu.VMEM((TILE + 2 * HALO, CONV_WIDTH), F32),
            pltpu.VMEM((SUBLANES, SHIFTED_ROWS, CONV_WIDTH), F32),
        ],
        compiler_params=_cparams(("parallel",)),
        name="mixers",
    )(zm, zm, zm, *tables)


def _route(logits_t):
    m = jnp.max(logits_t, axis=0, keepdims=True)
    e = jnp.exp(logits_t - m)
    p = e / jnp.sum(e, axis=0, keepdims=True)
    best_score = None
    best = None
    for g in range(N_EXPERT_GROUPS):
        rows = [p[g * EXPERTS_PER_GROUP + q:g * EXPERTS_PER_GROUP + q + 1] for q in range(EXPERTS_PER_GROUP)]
        score = None
        for a in range(EXPERTS_PER_GROUP):
            for b in range(a + 1, EXPERTS_PER_GROUP):
                pair = rows[a] + rows[b]
                score = pair if score is None else jnp.maximum(score, pair)
        if g == 0:
            best_score, best = score, jnp.zeros(score.shape, jnp.int32)
        else:
            upd = score > best_score
            best = jnp.where(upd, g, best)
            best_score = jnp.where(upd, score, best_score)
    eidx = lax.broadcasted_iota(jnp.int32, p.shape, 0)
    neg = -jnp.inf
    masked = jnp.where(eidx // EXPERTS_PER_GROUP == best, p, neg)
    m1 = jnp.max(masked, axis=0, keepdims=True)
    i1 = jnp.min(jnp.where(masked == m1, eidx, N_EXPERTS), axis=0, keepdims=True)
    masked2 = jnp.where(eidx == i1, neg, masked)
    m2 = jnp.max(masked2, axis=0, keepdims=True)
    i2 = jnp.min(jnp.where(masked2 == m2, eidx, N_EXPERTS), axis=0, keepdims=True)
    tot = m1 + m2
    return jnp.concatenate([i1, i2], axis=0), jnp.concatenate([m1 / tot, m2 / tot], axis=0)


def _merge_kernel(x_ref, mod_ref, n1_ref, n2_ref, ya_ref, mx_ref, wg_ref, bg_ref, wro_ref, wco_ref, wso_ref,
                  wpo_ref, wout_ref, rw_ref, rb_ref, x1_ref, h2_ref, ri_ref, rwt_ref):
    d = D_MODEL
    x = x_ref[...]
    hb = _norm_mod(x, n1_ref[...], mod_ref[:, 0:d], mod_ref[:, d:2 * d]).astype(BF16)
    cw = CONV_WIDTH
    branches = (
        (ya_ref[...], wro_ref),
        (mx_ref[:, 0:cw], wco_ref),
        (mx_ref[:, cw:cw + SGU_WIDTH], wso_ref),
        (mx_ref[:, cw + SGU_WIDTH:MX_OUT], wpo_ref),
    )
    merged = None
    for bi, (y_in, w_ref) in enumerate(branches):
        zg = _dot(hb, wg_ref[:, bi * d:(bi + 1) * d]) + bg_ref[:, bi * d:(bi + 1) * d]
        term = _sigmoid(zg) * _dot(y_in, w_ref[...])
        merged = term if merged is None else merged + term
    x1 = x + mod_ref[:, 2 * d:3 * d] * _dot(merged.astype(BF16), wout_ref[...])
    x1_ref[...] = x1
    h2 = _norm_mod(x1, n2_ref[...], mod_ref[:, 3 * d:4 * d], mod_ref[:, 4 * d:5 * d])
    h2_ref[...] = h2
    logits = _dot3(h2, rw_ref[...])
    ri_ref[...], rwt_ref[...] = _route(logits.T[0:N_EXPERTS] + rb_ref[...])


def _merge(x, mod_l, mod_row, n1, n2, ya, mx, wg, bg, wro, wco, wso, wpo, wout, rw_t, rb):
    n = x.shape[0]
    tm = TILE
    consts = (n1, n2)
    weights = (wg, bg, wro, wco, wso, wpo, wout, rw_t, rb)
    return pl.pallas_call(
        _merge_kernel,
        grid=(n // tm,),
        in_specs=[
            pl.BlockSpec((tm, D_MODEL), lambda i: (i, 0)),
            pl.BlockSpec((None, 1, 6 * D_MODEL), lambda i: (mod_row(i), 0, 0)),
            *[_const_spec(a.shape) for a in consts],
            pl.BlockSpec((tm, RWKV_WIDTH), lambda i: (i, 0)),
            pl.BlockSpec((tm, MX_OUT), lambda i: (i, 0)),
            *[_const_spec(a.shape) for a in weights],
        ],
        out_specs=[
            pl.BlockSpec((tm, D_MODEL), lambda i: (i, 0)),
            pl.BlockSpec((tm, D_MODEL), lambda i: (i, 0)),
            pl.BlockSpec((TOP_K, tm), lambda i: (0, i)),
            pl.BlockSpec((TOP_K, tm), lambda i: (0, i)),
        ],
        out_shape=[
            jax.ShapeDtypeStruct((n, D_MODEL), F32),
            jax.ShapeDtypeStruct((n, D_MODEL), F32),
            jax.ShapeDtypeStruct((TOP_K, n), jnp.int32),
            jax.ShapeDtypeStruct((TOP_K, n), F32),
        ],
        compiler_params=_cparams(("parallel",)),
        name="merge_router",
    )(x, mod_l, n1, n2, ya, mx, *weights)


PAIRS_PER_GROUP = EXPERTS_PER_GROUP * (EXPERTS_PER_GROUP - 1) // 2
N_PAIR_CLASSES = N_EXPERT_GROUPS * PAIRS_PER_GROUP
ROUTED_TILE = 256


def _dispatch_plan(route_i, route_w):
    n = route_i.shape[1]
    tm = ROUTED_TILE
    n_tiles = n // tm + N_PAIR_CLASSES
    i1, i2 = route_i[0], route_i[1]
    first_lower = i1 < i2
    lo, hi = jnp.minimum(i1, i2), jnp.maximum(i1, i2)
    w_lo = jnp.where(first_lower, route_w[0], route_w[1])
    w_hi = jnp.where(first_lower, route_w[1], route_w[0])
    a, b = lo % EXPERTS_PER_GROUP, hi % EXPERTS_PER_GROUP
    pair = a * (2 * EXPERTS_PER_GROUP - 1 - a) // 2 + (b - a - 1)
    cls = (lo // EXPERTS_PER_GROUP) * PAIRS_PER_GROUP + pair
    onehot = (cls[:, None] == jnp.arange(N_PAIR_CLASSES)[None, :]).astype(jnp.int32)
    counts = jnp.sum(onehot, axis=0)
    rank = jnp.sum((jnp.cumsum(onehot, axis=0) - onehot) * onehot, axis=1)
    padded = (counts + tm - 1) // tm * tm
    ends = jnp.cumsum(padded)
    dest = (ends - padded)[cls] + rank
    tile_row0 = jnp.arange(n_tiles, dtype=jnp.int32) * tm
    tile_valid = (tile_row0 < ends[-1]).astype(jnp.int32)
    tile_cls = jnp.minimum(jnp.searchsorted(ends, tile_row0, side="right"), N_PAIR_CLASSES - 1)
    stride = 1 << 15
    assert n + tm <= stride and N_PAIR_CLASSES * stride < (1 << 30)
    unused = jnp.iinfo(jnp.int32).max
    j = jnp.arange(tm, dtype=jnp.int32)[None, :]
    pad_keys = jnp.where(j < (padded - counts)[:, None], jnp.arange(N_PAIR_CLASSES)[:, None] * stride + n + j, unused)
    keys = jnp.concatenate([cls * stride + jnp.arange(n, dtype=jnp.int32), pad_keys.reshape(-1).astype(jnp.int32)])
    no_weight = jnp.zeros((N_PAIR_CLASSES * tm,), F32)
    keys, w_lo_sorted, w_hi_sorted = lax.sort(
        (keys, jnp.concatenate([w_lo, no_weight]), jnp.concatenate([w_hi, no_weight])), num_keys=1)
    token = keys % stride
    src = jnp.where((token < n) & (keys != unused), token, 0)
    w_sorted = jnp.stack([w_lo_sorted, w_hi_sorted], axis=1)
    cls_ids = jnp.arange(N_PAIR_CLASSES)
    pa = jnp.array([x for x in range(EXPERTS_PER_GROUP) for _ in range(x + 1, EXPERTS_PER_GROUP)], jnp.int32)
    pb = jnp.array([y for x in range(EXPERTS_PER_GROUP) for y in range(x + 1, EXPERTS_PER_GROUP)], jnp.int32)
    cls_lo = (cls_ids // PAIRS_PER_GROUP) * EXPERTS_PER_GROUP + pa[cls_ids % PAIRS_PER_GROUP]
    cls_hi = (cls_ids // PAIRS_PER_GROUP) * EXPERTS_PER_GROUP + pb[cls_ids % PAIRS_PER_GROUP]
    last_cls = tile_cls[jnp.maximum(ends[-1] // tm - 1, 0)]
    tile_cls = jnp.where(tile_valid == 1, tile_cls, last_cls)
    return dest.astype(jnp.int32), src, w_sorted, cls_lo[tile_cls].astype(jnp.int32), cls_hi[tile_cls].astype(jnp.int32)


def _start_row_gather(idx_ref, idx0, rows_hbm, buf, sem, n_rows):
    def start(r, carry):
        pltpu.make_async_copy(rows_hbm.at[pl.ds(idx_ref[idx0 + r], 1)], buf.at[pl.ds(r, 1)], sem).start()
        return carry
    lax.fori_loop(0, n_rows, start, 0, unroll=8)


def _wait_row_gather(rows_hbm, buf, sem, n_rows):
    pltpu.make_async_copy(rows_hbm.at[pl.ds(0, n_rows)], buf, sem).wait()


def _pipelined_gather(idx_ref, rows_hbm, bufs, sems, n_rows):
    i = pl.program_id(0)
    slot = i % 2

    @pl.when(i == 0)
    def _():
        _start_row_gather(idx_ref, 0, rows_hbm, bufs.at[0], sems.at[0], n_rows)

    @pl.when(i + 1 < pl.num_programs(0))
    def _():
        _start_row_gather(idx_ref, (i + 1) * n_rows, rows_hbm, bufs.at[1 - slot], sems.at[1 - slot], n_rows)

    _wait_row_gather(rows_hbm, bufs.at[slot], sems.at[slot], n_rows)
    return slot


EXPERT_GATHER_DEPTH = 2


def _experts_kernel(lo_ref, hi_ref, src_ref, w_ref, h_hbm, wg_lo, wu_lo, wd_lo, wg_hi, wu_hi, wd_hi,
                    o_ref, hbuf, sems):
    del lo_ref, hi_ref
    tm = ROUTED_TILE
    n_slots = EXPERT_GATHER_DEPTH + 1
    i = pl.program_id(0)
    n = pl.num_programs(0)
    slot = i % n_slots

    @pl.when(i == 0)
    def _():
        for t in range(EXPERT_GATHER_DEPTH):
            @pl.when(t < n)
            def _():
                _start_row_gather(src_ref, t * tm, h_hbm, hbuf.at[t], sems.at[t], tm)

    _wait_row_gather(h_hbm, hbuf.at[slot], sems.at[slot], tm)

    def tile(prefetch):
        ahead = (i + EXPERT_GATHER_DEPTH) % n_slots
        rows = iter(range(tm))

        def request(count):
            if not prefetch:
                return
            for r in (next(rows) for _ in range(count)):
                row = src_ref[(i + EXPERT_GATHER_DEPTH) * tm + r]
                pltpu.make_async_copy(h_hbm.at[pl.ds(row, 1)], hbuf.at[ahead, pl.ds(r, 1)], sems.at[ahead]).start()

        hb = hbuf[slot].astype(BF16)
        per_dot = tm // 6

        def expert(wg, wu, wd):
            request(per_dot)
            gate = _dot(hb, wg[...].astype(BF16))
            request(per_dot)
            up = _dot(hb, wu[...].astype(BF16))
            request(per_dot)
            return _dot((_silu(gate) * up).astype(BF16), wd[...].astype(BF16))

        w = w_ref[...]
        out = w[:, 0:1] * expert(wg_lo, wu_lo, wd_lo) + w[:, 1:2] * expert(wg_hi, wu_hi, wd_hi)
        request(tm - 6 * per_dot)
        o_ref[...] = out

    @pl.when(i + EXPERT_GATHER_DEPTH < n)
    def _():
        tile(True)

    @pl.when(i + EXPERT_GATHER_DEPTH >= n)
    def _():
        tile(False)


def _experts(h2, src, w_sorted, tile_lo, tile_hi, layer, wg, wu, wd):
    tm = ROUTED_TILE
    n_tiles = tile_lo.shape[0]
    n_slots = EXPERT_GATHER_DEPTH + 1
    up_spec = lambda sel: pl.BlockSpec((None, None, D_MODEL, D_EXPERT), lambda i, lo, hi, s: (layer, sel(lo, hi)[i], 0, 0))
    down_spec = lambda sel: pl.BlockSpec((None, None, D_EXPERT, D_MODEL), lambda i, lo, hi, s: (layer, sel(lo, hi)[i], 0, 0))
    pick_lo = lambda lo, hi: lo
    pick_hi = lambda lo, hi: hi
    return pl.pallas_call(
        _experts_kernel,
        grid_spec=pltpu.PrefetchScalarGridSpec(
            num_scalar_prefetch=3,
            grid=(n_tiles,),
            in_specs=[
                pl.BlockSpec((tm, TOP_K), lambda i, *_: (i, 0)),
                pl.BlockSpec(memory_space=pl.ANY),
                up_spec(pick_lo), up_spec(pick_lo), down_spec(pick_lo),
                up_spec(pick_hi), up_spec(pick_hi), down_spec(pick_hi),
            ],
            out_specs=pl.BlockSpec((tm, D_MODEL), lambda i, *_: (i, 0)),
            scratch_shapes=[pltpu.VMEM((n_slots, tm, D_MODEL), F32), pltpu.SemaphoreType.DMA((n_slots,))],
        ),
        out_shape=jax.ShapeDtypeStruct((n_tiles * tm, D_MODEL), F32),
        compiler_params=_cparams(("arbitrary",)),
        name="experts",
    )(tile_lo, tile_hi, src, w_sorted, h2, wg, wu, wd, wg, wu, wd)


def _combine_kernel(dest_ref, x_ref, mod_ref, ff_hbm, o_ref, fbuf, sems):
    slot = _pipelined_gather(dest_ref, ff_hbm, fbuf, sems, ROUTED_TILE)
    o_ref[...] = x_ref[...] + mod_ref[:, 5 * D_MODEL:6 * D_MODEL] * fbuf[slot]


def _combine(ff_sorted, dest, x1, mod_l, mod_row):
    n = x1.shape[0]
    tm = ROUTED_TILE
    return pl.pallas_call(
        _combine_kernel,
        grid_spec=pltpu.PrefetchScalarGridSpec(
            num_scalar_prefetch=1,
            grid=(n // tm,),
            in_specs=[
                pl.BlockSpec((tm, D_MODEL), lambda i, d: (i, 0)),
                pl.BlockSpec((None, 1, 6 * D_MODEL), lambda i, d: (mod_row(i), 0, 0)),
                pl.BlockSpec(memory_space=pl.ANY),
            ],
            out_specs=pl.BlockSpec((tm, D_MODEL), lambda i, d: (i, 0)),
            scratch_shapes=[pltpu.VMEM((2, tm, D_MODEL), F32), pltpu.SemaphoreType.DMA((2,))],
        ),
        out_shape=jax.ShapeDtypeStruct((n, D_MODEL), F32),
        compiler_params=_cparams(("arbitrary",)),
        name="moe_combine",
    )(dest, x1, mod_l, ff_sorted)


def _final_norm_kernel(x_ref, g_ref, o_ref):
    x = x_ref[...]
    o_ref[...] = x * lax.rsqrt(jnp.mean(x * x, axis=-1, keepdims=True) + NORM_EPS) * g_ref[...]


def _final_norm(x, g, tile0, n_tiles):
    return pl.pallas_call(
        _final_norm_kernel,
        grid=(n_tiles,),
        in_specs=[pl.BlockSpec((TILE, D_MODEL), lambda i: (i + tile0, 0)), _const_spec((1, D_MODEL))],
        out_specs=pl.BlockSpec((TILE, D_MODEL), lambda i: (i, 0)),
        out_shape=jax.ShapeDtypeStruct((n_tiles * TILE, D_MODEL), F32),
        compiler_params=_cparams(("parallel",)),
        name="final_norm",
    )(x, g)


def _grid_pos_embed(n_tokens):
    rows = n_tokens // GRID_W
    quarter = D_MODEL // 4
    half = D_MODEL // 2
    omega = 1.0 / (10000.0 ** (jnp.arange(quarter, dtype=F32) / quarter))
    ang_r = jnp.arange(rows, dtype=F32)[:, None] * omega
    ang_c = jnp.arange(GRID_W, dtype=F32)[:, None] * omega
    emb_r = jnp.concatenate([jnp.sin(ang_r), jnp.cos(ang_r)], axis=-1)
    emb_c = jnp.concatenate([jnp.sin(ang_c), jnp.cos(ang_c)], axis=-1)
    emb = jnp.concatenate([jnp.broadcast_to(emb_r[:, None, :], (rows, GRID_W, half)),
                           jnp.broadcast_to(emb_c[None, :, :], (rows, GRID_W, half))], axis=-1)
    return emb.reshape(rows * GRID_W, D_MODEL)


def _pad_rows(w, rows, offset):
    out = jnp.zeros(w.shape[:-2] + (rows, w.shape[-1]), w.dtype)
    return lax.dynamic_update_slice_in_dim(out, w, offset, axis=-2)


def _hg_cols(w):
    return jnp.moveaxis(w.reshape(w.shape[:-1] + (N_HG, HG_LANES)), -2, 0)


def kernel(x_prompt, x_sample, state_rwkv, c, c_ctx, norm1_g, norm2_g, w_mod, b_mod, w_in, b_in, rwkv_w0, rwkv_w_up,
           rwkv_a0, rwkv_a_up, rwkv_g_up, rwkv_k_k, rwkv_k_a, rwkv_r_k, rwkv_gn_g, rwkv_gn_b, rwkv_w_o, conv_dw,
           conv_dw_b, conv_ln_g, conv_ln_b, conv_w_o, sgu_ln_g, sgu_ln_b, sgu_w_s, sgu_b_s, sgu_w_o, pool_w,
           pool_scale, pool_w_o, w_out, moe_w_gate, moe_w_up, moe_w_down, router_w, router_b, final_norm_g):
    n_ctx, t_ctx, d = x_prompt.shape
    n_lat, t_lat, _ = x_sample.shape
    n_layers = w_in.shape[0]
    assert d == D_MODEL and t_ctx % TILE == 0 and t_lat % TILE == 0 and n_lat < MOD_ROWS
    ctx_rows, lat_rows = n_ctx * t_ctx, n_lat * t_lat
    assert ctx_rows % t_lat == 0 and t_lat % PROJ_TILE == 0 and ROUTED_TILE == TILE
    ctx_tiles = ctx_rows // TILE
    seq_tiles_ctx, seq_tiles_lat = t_ctx // TILE, t_lat // TILE

    def mod_row_for(tile_rows):
        ctx_t = ctx_rows // tile_rows
        per_lat = t_lat // tile_rows
        return lambda i: jnp.where(i < ctx_t, n_lat, jnp.maximum(i - ctx_t, 0) // per_lat)

    mod_row = mod_row_for(TILE)
    mod_row_proj = mod_row_for(PROJ_TILE)

    cond = jnp.zeros((MOD_ROWS, d), F32).at[:n_lat].set(c).at[n_lat].set(c_ctx)
    mod = _modulation(cond, w_mod, b_mod).reshape(n_layers, MOD_ROWS, 1, 6 * d)

    x = _embed(x_prompt.reshape(ctx_rows, d), x_sample.reshape(lat_rows, d), _grid_pos_embed(t_lat), seq_tiles_lat)

    n_rw = RW_COLS
    n_mx = MX_COLS
    wa = w_in[:, :, :n_rw].astype(BF16)
    ba = b_in[:, None, :n_rw]
    wm = w_in[:, :, n_rw:n_rw + n_mx].astype(BF16)
    bm = b_in[:, None, n_rw:n_rw + n_mx]
    wgt = w_in[:, :, n_rw + n_mx:].astype(BF16)
    bgt = b_in[:, None, n_rw + n_mx:]

    par = jnp.zeros((n_layers, P_ROWS, RWKV_WIDTH), F32)
    par = par.at[:, P_W0:P_W0 + 2].set(rwkv_w0).at[:, P_A0:P_A0 + 2].set(rwkv_a0)
    par = par.at[:, P_KK].set(rwkv_k_k).at[:, P_KA].set(rwkv_k_a)
    par = par.at[:, P_RK].set(rwkv_r_k.reshape(n_layers, RWKV_WIDTH))
    par = par.at[:, P_GNG].set(rwkv_gn_g).at[:, P_GNB].set(rwkv_gn_b)
    par = jnp.moveaxis(_hg_cols(par), 0, 1)
    lr_k = RWKV_DECAY_RANK + RWKV_ICLR_RANK
    wup = jnp.moveaxis(_hg_cols(_pad_rows(rwkv_w_up, lr_k, 0)), 0, 2).astype(BF16)
    aup = jnp.moveaxis(_hg_cols(_pad_rows(rwkv_a_up, lr_k, RWKV_DECAY_RANK)), 0, 2).astype(BF16)
    gup = jnp.moveaxis(_hg_cols(rwkv_g_up), 0, 1).astype(BF16)
    rw_consts = _rwkv_consts()

    mix_vecs = jnp.zeros((n_layers, 8, CONV_WIDTH), F32)
    for row, vec in enumerate((conv_dw_b, conv_ln_g, conv_ln_b, sgu_ln_g, sgu_ln_b, pool_scale)):
        mix_vecs = mix_vecs.at[:, row].set(vec)
    sgu_wcat = jnp.transpose(sgu_w_s, (0, 2, 1, 3)).reshape(n_layers, SGU_CHUNK, SGU_GROUPS * SGU_CHUNK).astype(BF16)
    sgu_bs = jnp.repeat(jnp.swapaxes(sgu_b_s, 1, 2), SGU_WIDTH // SGU_GROUPS, axis=2)
    pc = POOL_WIDTH // POOL_GROUPS
    pool_bd = (pool_w[:, :, :, None, :] * jnp.eye(POOL_GROUPS, dtype=F32)[None, :, None, :, None])
    pool_bd = pool_bd.reshape(n_layers, POOL_WIDTH, POOL_WIDTH).astype(BF16)
    lane_grp = jnp.arange(SGU_WIDTH)[None, :] // (SGU_WIDTH // SGU_GROUPS)
    sgm = (jnp.arange(SGU_GROUPS * SGU_CHUNK)[:, None] // SGU_CHUNK == lane_grp).astype(BF16)
    wlen = TILE + 2 * HALO
    plm = (jnp.arange(POOL_GROUPS * wlen)[:, None] // wlen == jnp.arange(POOL_WIDTH)[None, :] // pc).astype(BF16)
    pool_band, pool_cnt = _pool_tables()

    wro, wco, wso, wpo, wout = (w.astype(BF16) for w in (rwkv_w_o, conv_w_o, sgu_w_o, pool_w_o, w_out))
    rw_t = jnp.pad(router_w, ((0, 0), (0, ROUTER_LANES - N_EXPERTS)))
    rb = router_b[:, None]

    ctx_s0 = jnp.zeros((n_ctx, N_HG, 2, RWKV_HEAD_DIM, HG_LANES), F32)
    ctx_states = []
    for l in range(n_layers):
        zr, zm = _in_projection(x, mod[l], mod_row_proj, norm1_g[l][None], wa[l], ba[l], wm[l], bm[l])
        rw_args = (par[l], wup[l], aup[l], gup[l], rw_consts)
        ya = jnp.zeros((x.shape[0], RWKV_WIDTH), BF16)
        ya, s_fin = _rwkv_mixer(zr, ctx_s0, n_ctx, t_ctx, 0, ya, *rw_args)
        ya, _ = _rwkv_mixer(zr, _states_to_kernel(state_rwkv[:, l]), n_lat, t_lat, ctx_rows // t_lat, ya, *rw_args)
        ctx_states.append(_states_from_kernel(s_fin))
        mx = _mixers(zm, ctx_tiles, seq_tiles_ctx, seq_tiles_lat, conv_dw[l], mix_vecs[l], sgu_wcat[l], sgu_bs[l],
                     pool_bd[l], sgm, plm, pool_band, pool_cnt)
        x1, h2, route_i, route_w = _merge(x, mod[l], mod_row, norm1_g[l][None], norm2_g[l][None], ya, mx, wgt[l], bgt[l],
                                 wro[l], wco[l], wso[l], wpo[l], wout[l], rw_t, rb)
        dest, src, w_sorted, tile_lo, tile_hi = _dispatch_plan(route_i, route_w)
        ff = _experts(h2, src, w_sorted, tile_lo, tile_hi, l, moe_w_gate, moe_w_up, moe_w_down)
        x = _combine(ff, dest, x1, mod[l], mod_row)

    g_fin = final_norm_g[None]
    y_prompt = _final_norm(x, g_fin, 0, ctx_tiles).reshape(n_ctx, t_ctx, d)
    y_sample = _final_norm(x, g_fin, ctx_tiles, lat_rows // TILE).reshape(n_lat, t_lat, d)
    new_state = jnp.stack(ctx_states, axis=1).astype(x_prompt.dtype)
    return (y_prompt, y_sample, new_state)
```

```python
import functools
import math

import jax
import jax.numpy as jnp
from jax import lax
from jax.experimental import pallas as pl
from jax.experimental.pallas import tpu as pltpu

F32 = jnp.float32
BF16 = jnp.bfloat16

D_MODEL = 1024
GRID_W = 64
RWKV_HEADS = 8
RWKV_HEAD_DIM = 64
RWKV_WIDTH = RWKV_HEADS * RWKV_HEAD_DIM
RWKV_DECAY_RANK = 64
RWKV_ICLR_RANK = 64
RWKV_GATE_RANK = 128
RWKV_GN_EPS = 64e-5
CONV_WIDTH = 256
CONV_KERNEL = 31
SGU_GROUPS = 4
SGU_WIDTH = 256
SGU_CHUNK = 128
POOL_GROUPS = 4
POOL_WIDTH = 256
POOL_WINDOWS = (2, 4, 8, 16)
N_BRANCHES = 4
N_EXPERTS = 16
TOP_K = 2
N_EXPERT_GROUPS = 4
EXPERTS_PER_GROUP = N_EXPERTS // N_EXPERT_GROUPS
D_EXPERT = 512
NORM_EPS = 1e-6
LN_EPS = 1e-5

RW_COLS = 3 * RWKV_WIDTH + RWKV_DECAY_RANK + RWKV_ICLR_RANK + RWKV_GATE_RANK
MX_COLS = 2 * CONV_WIDTH + 2 * SGU_WIDTH + POOL_WIDTH
MX_OUT = CONV_WIDTH + SGU_WIDTH + POOL_WIDTH

TILE = 256
PROJ_TILE = 512
HALO = 16
SUBLANES = 8
SHIFTED_ROWS = TILE + 2 * HALO - SUBLANES
HG_LANES = 256
HEADS_PER_HG = HG_LANES // RWKV_HEAD_DIM
N_HG = RWKV_WIDTH // HG_LANES
SCAN_CHUNK = 64
assert SCAN_CHUNK == RWKV_HEAD_DIM
CHUNKS_PER_TRIP = 2
DECAY_SCALE = math.exp(-0.5)
MOD_ROWS = 16
ROUTER_LANES = 128

VMEM_LIMIT = 48 * 1024 * 1024
RWKV_VMEM_LIMIT = 56 * 1024 * 1024


def _cparams(sem, vmem_limit=VMEM_LIMIT):
    return pltpu.CompilerParams(dimension_semantics=sem, vmem_limit_bytes=vmem_limit)


def _dot(a, b):
    return jnp.dot(a, b, preferred_element_type=F32)


def _split2(x):
    hi = x.astype(BF16)
    lo = (x - hi.astype(F32)).astype(BF16)
    return hi, lo


def _dot3(a, b):
    a_hi, a_lo = _split2(a)
    b_hi, b_lo = _split2(b)
    return _dot(a_hi, b_hi) + (_dot(a_lo, b_hi) + _dot(a_hi, b_lo))


def _dot_exact_rhs(x, rhs):
    hi, lo = _split2(x)
    return _dot(hi, rhs) + _dot(lo, rhs)


def _dot_exact_lhs(lhs, x):
    hi, lo = _split2(x)
    return _dot(lhs, hi) + _dot(lhs, lo)


def _sigmoid(x):
    return 1.0 / (1.0 + jnp.exp(-x))


def _silu(x):
    return x * _sigmoid(x)


def _norm_mod(x, g, shift, scale):
    y = x * lax.rsqrt(jnp.mean(x * x, axis=-1, keepdims=True) + NORM_EPS) * g
    return y * (1.0 + scale) + shift


def _layer_norm(x, g, b):
    mu = jnp.mean(x, axis=-1, keepdims=True)
    xc = x - mu
    var = jnp.mean(xc * xc, axis=-1, keepdims=True)
    return xc * lax.rsqrt(var + LN_EPS) * g + b


def _const_spec(shape):
    nd = len(shape)
    return pl.BlockSpec(shape, lambda *_: (0,) * nd)


def _mod_kernel(c_ref, w_ref, b_ref, o_ref):
    o_ref[...] = _dot3(_silu(c_ref[...]), w_ref[...]) + b_ref[...]


def _modulation(cond, w_mod, b_mod):
    n_layers = w_mod.shape[0]
    tn = 1536
    return pl.pallas_call(
        _mod_kernel,
        grid=(n_layers, 6 * D_MODEL // tn),
        in_specs=[
            pl.BlockSpec((MOD_ROWS, D_MODEL), lambda l, j: (0, 0)),
            pl.BlockSpec((None, D_MODEL, tn), lambda l, j: (l, 0, j)),
            pl.BlockSpec((None, 1, tn), lambda l, j: (l, 0, j)),
        ],
        out_specs=pl.BlockSpec((None, MOD_ROWS, tn), lambda l, j: (l, 0, j)),
        out_shape=jax.ShapeDtypeStruct((n_layers, MOD_ROWS, 6 * D_MODEL), F32),
        compiler_params=_cparams(("parallel", "parallel")),
        name="modulation",
    )(cond, w_mod, b_mod.reshape(n_layers, 1, 6 * D_MODEL))


def _embed_kernel(n_ctx_tiles, xp_ref, xs_ref, pos_ref, o_ref):
    i = pl.program_id(0)

    @pl.when(i < n_ctx_tiles)
    def _():
        o_ref[...] = xp_ref[...]

    @pl.when(i >= n_ctx_tiles)
    def _():
        o_ref[...] = xs_ref[...] + pos_ref[...]


def _embed(xp, xs, pos, lat_tiles):
    n_ctx_tiles = xp.shape[0] // TILE
    n_lat_tiles = xs.shape[0] // TILE
    n_tiles = n_ctx_tiles + n_lat_tiles
    return pl.pallas_call(
        functools.partial(_embed_kernel, n_ctx_tiles),
        grid=(n_tiles,),
        in_specs=[
            pl.BlockSpec((TILE, D_MODEL), lambda i: (jnp.minimum(i, n_ctx_tiles - 1), 0)),
            pl.BlockSpec((TILE, D_MODEL), lambda i: (jnp.maximum(i - n_ctx_tiles, 0), 0)),
            pl.BlockSpec((TILE, D_MODEL), lambda i: (jnp.maximum(i - n_ctx_tiles, 0) % lat_tiles, 0)),
        ],
        out_specs=pl.BlockSpec((TILE, D_MODEL), lambda i: (i, 0)),
        out_shape=jax.ShapeDtypeStruct((n_tiles * TILE, D_MODEL), F32),
        compiler_params=_cparams(("parallel",)),
        name="embed",
    )(xp, xs, pos)


def _inproj_kernel(x_ref, mod_ref, g_ref, wa_ref, ba_ref, wm_ref, bm_ref, zr_ref, zm_ref):
    h = _norm_mod(x_ref[...], g_ref[...], mod_ref[:, 0:D_MODEL], mod_ref[:, D_MODEL:2 * D_MODEL])
    hb = h.astype(BF16)
    zr_ref[...] = _dot(hb, wa_ref[...]) + ba_ref[...]
    zm_ref[...] = (_dot(hb, wm_ref[...]) + bm_ref[...]).astype(BF16)


def _in_projection(x, mod_l, mod_row, norm_g, wa, ba, wm, bm):
    n = x.shape[0]
    tm = PROJ_TILE
    return pl.pallas_call(
        _inproj_kernel,
        grid=(n // tm,),
        in_specs=[
            pl.BlockSpec((tm, D_MODEL), lambda i: (i, 0)),
            pl.BlockSpec((None, 1, 6 * D_MODEL), lambda i: (mod_row(i), 0, 0)),
            _const_spec((1, D_MODEL)),
            _const_spec((D_MODEL, RW_COLS)),
            _const_spec((1, RW_COLS)),
            _const_spec((D_MODEL, MX_COLS)),
            _const_spec((1, MX_COLS)),
        ],
        out_specs=[
            pl.BlockSpec((tm, RW_COLS), lambda i: (i, 0)),
            pl.BlockSpec((tm, MX_COLS), lambda i: (i, 0)),
        ],
        out_shape=[
            jax.ShapeDtypeStruct((n, RW_COLS), F32),
            jax.ShapeDtypeStruct((n, MX_COLS), BF16),
        ],
        compiler_params=_cparams(("parallel",)),
        name="in_projection",
    )(x, mod_l, norm_g, wa, ba, wm, bm)


P_W0, P_A0, P_KK, P_KA, P_RK, P_GNG, P_GNB = 0, 2, 4, 5, 6, 7, 8
P_ROWS = 16


def _rep(xb, reps, mask):
    return jnp.concatenate([xb] * reps, axis=0) * mask


def _rwkv_kernel(t_len, y_all_ref, r_ref, k_ref, v_ref, lr_ref, s0_ref, par_ref, wup_ref, aup_ref, gup_ref,
                 bd_ref, cm_ref, tri_ref, eyec_ref, out_ref, sfin_ref, y_scr, st_scr):
    del y_all_ref
    c = SCAN_CHUNK
    n_chunks = t_len // c
    n_trips = n_chunks // CHUNKS_PER_TRIP
    lr_k = RWKV_DECAY_RANK + RWKV_ICLR_RANK
    probs = [(d, sl, g) for sl in range(CHUNKS_PER_TRIP) for d in (0, 1) for g in range(N_HG)]
    n_sq = int(math.log2(c))

    def lanes(g):
        return slice(g * HG_LANES, (g + 1) * HG_LANES)

    def rep(xb):
        return _rep(xb, HEADS_PER_HG, bd_ref[...])

    def seg_sums(xs, passes):
        n = xs[0].shape[0]
        if passes == 1:
            out = _dot(jnp.concatenate([x.astype(BF16) for x in xs], axis=0), bd_ref[...])
            return [out[i * n:(i + 1) * n] for i in range(len(xs))]
        parts = []
        for x in xs:
            parts.extend(_split2(x))
        out = _dot(jnp.concatenate(parts, axis=0), bd_ref[...])
        return [out[2 * i * n:(2 * i + 1) * n] + out[(2 * i + 1) * n:(2 * i + 2) * n] for i in range(len(xs))]

    def fold(x_bd):
        n = x_bd.shape[0] // HEADS_PER_HG
        return (x_bd[0:n] + x_bd[n:2 * n]) + (x_bd[2 * n:3 * n] + x_bd[3 * n:4 * n])

    def par(g, row):
        return par_ref[g, row:row + 1, :]

    st_scr[...] = s0_ref[...]

    def trip(j, first_touch):
        rows = {}
        for sl in range(CHUNKS_PER_TRIP):
            rows[0, sl] = pl.ds(pl.multiple_of((CHUNKS_PER_TRIP * j + sl) * c, c), c)
            rows[1, sl] = pl.ds(pl.multiple_of((n_chunks - 1 - CHUNKS_PER_TRIP * j - sl) * c, c), c)
        lr = {key: lr_ref[rw, 0:lr_k] for key, rw in rows.items()}
        th = {key: jnp.tanh(x).astype(BF16) for key, x in lr.items()}
        lrb = {key: x.astype(BF16) for key, x in lr.items()}
        ss = [dict(r=r_ref[rows[d, sl], lanes(g)], k=k_ref[rows[d, sl], lanes(g)], v=v_ref[rows[d, sl], lanes(g)])
              for d, sl, g in probs]

        for (d, sl, g), s in zip(probs, ss):
            s["w_pre"] = par(g, P_W0 + d) + _dot(th[d, sl], wup_ref[d, g])
            s["a_pre"] = par(g, P_A0 + d) + _dot(lrb[d, sl], aup_ref[d, g])
            s["kkr"] = s["k"] * par(g, P_KK)
        for s, ssq in zip(ss, seg_sums([s["kkr"] * s["kkr"] for s in ss], 1)):
            s["ssq"] = ssq
        for (d, sl, g), s in zip(probs, ss):
            s["lw"] = -DECAY_SCALE * _sigmoid(s["w_pre"])
            s["cum"] = _dot_exact_lhs(tri_ref[d], s["lw"])
        for (d, sl, g), s in zip(probs, ss):
            a = _sigmoid(s["a_pre"])
            kk = s["kkr"] / jnp.maximum(jnp.sqrt(s["ssq"]), 1e-12)
            kd = s["k"] * (1.0 + (a - 1.0) * par(g, P_KA))
            cum = s["cum"]
            e_incl = jnp.exp(cum)
            e_neg = jnp.exp(-cum)
            gam = e_incl[c - 1:c, :] if d == 0 else e_incl[0:1, :]
            s["at"] = (-kk * jnp.exp(cum - s["lw"])).astype(BF16)
            s["rt"] = s["r"] * e_incl
            bt = kk * a * e_neg
            kt = kd * e_neg
            s["vb"] = s["v"].astype(BF16)
            s["bonus"] = s["r"] * kd * par(g, P_RK)
            s["gam_rows"] = _dot_exact_rhs(eyec_ref[...] * gam, bd_ref[...])
            tr = jnp.concatenate([bt, kt], axis=0).T
            swapped = pltpu.roll(tr, c, axis=1)
            first_half = lax.broadcasted_iota(jnp.int32, tr.shape, 1) < c
            bt_t = jnp.where(first_half, tr, swapped).astype(BF16)
            kt_t = jnp.where(first_half, swapped, tr).astype(BF16)
            s["bt_bd"] = jnp.concatenate([bt_t, bt_t], axis=1) * bd_ref[...]
            s["kt_bd"] = jnp.concatenate([kt_t, kt_t], axis=1) * bd_ref[...]
            lhs2 = jnp.concatenate([s["at"], s["rt"].astype(BF16)], axis=0)
            s["m_b"] = _dot(lhs2, s["bt_bd"])
            s["m_k"] = _dot(lhs2, s["kt_bd"])
        for s, bsum in zip(ss, seg_sums([s["bonus"] for s in ss], 1)):
            s["bsum"] = bsum
        for (d, sl, g), s in zip(probs, ss):
            causal = cm_ref[d]
            m_b = s["m_b"] * causal
            s["x"] = m_b[0:c]
            s["a_rb"] = m_b[c:2 * c].astype(BF16)
            s["t"] = eyec_ref[...] + s["x"]
            s["bt_c"] = fold(s["bt_bd"])
            av = _dot(jnp.concatenate([(s["m_k"] * causal).astype(BF16), fold(s["kt_bd"])], axis=0), rep(s["vb"]))
            s["g1"] = av[0:c].astype(BF16)
            s["y0"] = av[c:2 * c] + s["bsum"] * s["v"]
            s["z"] = av[2 * c:2 * c + RWKV_HEAD_DIM]

        for step in range(1, n_sq):
            for s in ss:
                x_bd = rep(s["x"].astype(BF16))
                if step == 1:
                    s["x"] = _dot(s["x"].astype(BF16), x_bd)
                else:
                    both = _dot(jnp.concatenate([s["x"], s["t"]], axis=0).astype(BF16), x_bd)
                    s["x"], s["t"] = both[0:c], s["t"] + both[c:2 * c]
        for s in ss:
            s["t"] = (s["t"] + _dot(s["t"].astype(BF16), rep(s["x"].astype(BF16)))).astype(BF16)

        for s in ss:
            s["w"] = _dot(s["t"], rep(s["at"])).astype(BF16)
            s["u0"] = _dot(s["t"], rep(s["g1"])).astype(BF16)
        for s in ss:
            lhs = jnp.concatenate([s["a_rb"], s["bt_c"]], axis=0)
            with_w = _dot(lhs, rep(s["w"]))
            with_u0 = _dot(lhs, rep(s["u0"]))
            s["q"] = (s["rt"] + with_w[0:c]).astype(BF16)
            s["p"] = (eyec_ref[...] + with_w[c:c + RWKV_HEAD_DIM]).astype(BF16)
            s["y0"] = s["y0"] + with_u0[0:c]
            s["z"] = s["z"] + with_u0[c:c + RWKV_HEAD_DIM]

        for (d, sl, g), s in zip(probs, ss):
            both = _dot(jnp.concatenate([s["p"], s["q"]], axis=0), rep(st_scr[g, d].astype(BF16)))
            st_scr[g, d] = s["gam_rows"] * (both[0:RWKV_HEAD_DIM] + s["z"])
            y = both[RWKV_HEAD_DIM:RWKV_HEAD_DIM + c] + s["y0"]
            if first_touch:
                y_scr[rows[d, sl], lanes(g)] = y
            else:
                y_scr[rows[d, sl], lanes(g)] += y

    half = n_trips // 2
    lax.fori_loop(0, half, lambda j, carry: (trip(j, True), carry)[1], 0)
    lax.fori_loop(half, n_trips, lambda j, carry: (trip(j, False), carry)[1], 0)
    sfin_ref[...] = st_scr[...]

    inv_n = 1.0 / RWKV_HEAD_DIM

    def finish(j, carry):
        rows = pl.ds(pl.multiple_of(j * TILE, TILE), TILE)
        sg = _sigmoid(lr_ref[rows, lr_k:HG_LANES]).astype(BF16)
        ys = [y_scr[rows, lanes(g)] for g in range(N_HG)]
        mus = [m * inv_n for m in seg_sums(ys, 2)]
        ycs = [y - mu for y, mu in zip(ys, mus)]
        variances = [v * inv_n for v in seg_sums([yc * yc for yc in ycs], 2)]
        for g in range(N_HG):
            yn = ycs[g] * lax.rsqrt(variances[g] + RWKV_GN_EPS) * par(g, P_GNG) + par(g, P_GNB)
            out_ref[rows, lanes(g)] = (yn * _dot(sg, gup_ref[g])).astype(BF16)
        return carry

    lax.fori_loop(0, t_len // TILE, finish, 0)


def _rwkv_mixer(zr, s0, n_seq, t_len, row_off, y_all, par, wup, aup, gup, consts):
    lr_block = 3 * RWKV_WIDTH // HG_LANES
    state_spec = pl.BlockSpec((None, N_HG, 2, RWKV_HEAD_DIM, HG_LANES), lambda b: (b, 0, 0, 0, 0))
    weights = (par, wup, aup, gup)
    return pl.pallas_call(
        functools.partial(_rwkv_kernel, t_len),
        grid=(n_seq,),
        in_specs=[
            pl.BlockSpec(memory_space=pl.ANY),
            pl.BlockSpec((t_len, RWKV_WIDTH), lambda b: (b + row_off, 0)),
            pl.BlockSpec((t_len, RWKV_WIDTH), lambda b: (b + row_off, 1)),
            pl.BlockSpec((t_len, RWKV_WIDTH), lambda b: (b + row_off, 2)),
            pl.BlockSpec((t_len, HG_LANES), lambda b: (b + row_off, lr_block)),
            state_spec,
            *[_const_spec(a.shape) for a in weights],
            *[_const_spec(a.shape) for a in consts],
        ],
        out_specs=[pl.BlockSpec((t_len, RWKV_WIDTH), lambda b: (b + row_off, 0)), state_spec],
        out_shape=[
            jax.ShapeDtypeStruct((zr.shape[0], RWKV_WIDTH), BF16),
            jax.ShapeDtypeStruct((n_seq, N_HG, 2, RWKV_HEAD_DIM, HG_LANES), F32),
        ],
        scratch_shapes=[
            pltpu.VMEM((t_len, RWKV_WIDTH), F32),
            pltpu.VMEM((N_HG, 2, RWKV_HEAD_DIM, HG_LANES), F32),
        ],
        input_output_aliases={0: 0},
        compiler_params=_cparams(("parallel",), RWKV_VMEM_LIMIT),
        name="rwkv_mixer",
    )(y_all, zr, zr, zr, zr, s0, *weights, *consts)


def _rwkv_consts():
    c = SCAN_CHUNK
    ri = jnp.arange(HG_LANES)[:, None]
    ci = jnp.arange(HG_LANES)[None, :]
    bd = (ri // RWKV_HEAD_DIM == ci // RWKV_HEAD_DIM).astype(BF16)
    t = jnp.arange(c)[:, None]
    i = (jnp.arange(HEADS_PER_HG * c) % c)[None, :]
    cm = jnp.stack([jnp.concatenate([i < t, i <= t]), jnp.concatenate([i > t, i >= t])]).astype(F32)
    ti = jnp.arange(c)[None, :]
    tri = jnp.stack([ti <= t, ti >= t]).astype(BF16)
    eyec = (i == t).astype(F32)
    return bd, cm, tri, eyec


def _states_to_kernel(s):
    b = s.shape[0]
    st = s.reshape(b, 2, N_HG, HEADS_PER_HG, RWKV_HEAD_DIM, RWKV_HEAD_DIM)
    return jnp.transpose(st, (0, 2, 1, 5, 3, 4)).reshape(b, N_HG, 2, RWKV_HEAD_DIM, HG_LANES)


def _states_from_kernel(sk):
    b = sk.shape[0]
    st = sk.reshape(b, N_HG, 2, RWKV_HEAD_DIM, HEADS_PER_HG, RWKV_HEAD_DIM)
    return jnp.transpose(st, (0, 2, 1, 4, 5, 3)).reshape(b, 2, RWKV_HEADS, RWKV_HEAD_DIM, RWKV_HEAD_DIM)


def _pool_tables():
    wlen = TILE + 2 * HALO
    t_loc = jnp.arange(TILE)[:, None]
    s_rel = jnp.arange(wlen)[None, :] - HALO
    lane_g = jnp.arange(POOL_WIDTH)[None, :] // (POOL_WIDTH // POOL_GROUPS)
    bands, counts = [], []
    for first in (0, 1):
        for last in (0, 1):
            lo_seq = 0 if first else -HALO
            hi_seq = TILE if last else TILE + HALO
            band_groups = []
            cnt = jnp.zeros((TILE, POOL_WIDTH), F32)
            for gi, win in enumerate(POOL_WINDOWS):
                lo = jnp.maximum(t_loc - win // 2, lo_seq)
                hi = jnp.minimum(t_loc + win - win // 2, hi_seq)
                band_groups.append((s_rel >= lo) & (s_rel < hi))
                cnt = jnp.where(lane_g == gi, (hi - lo).astype(F32), cnt)
            bands.append(jnp.concatenate(band_groups, axis=1))
            counts.append(cnt)
    return jnp.stack(bands).astype(BF16), jnp.stack(counts)


def _mixers_kernel(ctx_tiles, seq_tiles_ctx, seq_tiles_lat, zc_ref, zp_ref, zn_ref, dw_ref, vec_ref, wcat_ref,
                   bs_ref, pw_ref, sgm_ref, plm_ref, band_ref, cnt_ref, out_ref, win_scr, shift_scr):
    i = pl.program_id(0)
    j_ctx = i % seq_tiles_ctx
    j_lat = jnp.maximum(i - ctx_tiles, 0) % seq_tiles_lat
    is_ctx = i < ctx_tiles
    first = jnp.where(is_ctx, j_ctx == 0, j_lat == 0)
    last = jnp.where(is_ctx, j_ctx == seq_tiles_ctx - 1, j_lat == seq_tiles_lat - 1)
    keep_prev = jnp.where(first, 0.0, 1.0)
    keep_next = jnp.where(last, 0.0, 1.0)
    case = 2 * first.astype(jnp.int32) + last.astype(jnp.int32)

    cw = CONV_WIDTH
    conv_b, cln_g, cln_b = vec_ref[0:1, :], vec_ref[1:2, :], vec_ref[2:3, :]
    sln_g, sln_b, pool_scale = vec_ref[3:4, :], vec_ref[4:5, :], vec_ref[5:6, :]

    glu = lambda z_ref: z_ref[:, 0:cw].astype(F32) * _sigmoid(z_ref[:, cw:2 * cw].astype(F32))
    win_scr[0:HALO, :] = glu(zp_ref) * keep_prev
    win_scr[HALO:HALO + TILE, :] = glu(zc_ref)
    win_scr[HALO + TILE:2 * HALO + TILE, :] = glu(zn_ref) * keep_next
    for r in range(1, SUBLANES):
        shift_scr[r] = win_scr[pl.ds(r, SHIFTED_ROWS), :]
    acc = jnp.zeros((TILE, cw), F32) + conv_b
    pad = CONV_KERNEL // 2
    for j in range(CONV_KERNEL):
        q, r = divmod(HALO - pad + j, SUBLANES)
        rows = pl.ds(q * SUBLANES, TILE)
        tap = win_scr[rows, :] if r == 0 else shift_scr[r, rows, :]
        acc = acc + tap * dw_ref[j:j + 1, :]
    out_ref[:, 0:cw] = _silu(_layer_norm(acc, cln_g, cln_b)).astype(BF16)

    su = zc_ref[:, 2 * cw:2 * cw + SGU_WIDTH].astype(F32)
    sv = zc_ref[:, 2 * cw + SGU_WIDTH:2 * cw + 2 * SGU_WIDTH].astype(F32)
    vn = _layer_norm(sv, sln_g, sln_b).astype(BF16)
    sgm = sgm_ref[...]
    for ch in range(TILE // SGU_CHUNK):
        rows = slice(ch * SGU_CHUNK, (ch + 1) * SGU_CHUNK)
        s = _dot(wcat_ref[...], _rep(vn[rows], SGU_GROUPS, sgm)) + bs_ref[...]
        out_ref[rows, cw:cw + SGU_WIDTH] = (su[rows] * s).astype(BF16)

    zoff = 2 * cw + 2 * SGU_WIDTH
    zcur = zc_ref[:, zoff:zoff + POOL_WIDTH]
    zw = jnp.concatenate([zp_ref[:, zoff:zoff + POOL_WIDTH], zcur, zn_ref[:, zoff:zoff + POOL_WIDTH]], axis=0)
    psum = _dot(band_ref[case], _rep(zw, POOL_GROUPS, plm_ref[...]))
    p = psum / cnt_ref[case] - zcur.astype(F32)
    pooled = _dot(p.astype(BF16), pw_ref[...]) * pool_scale
    out_ref[:, cw + SGU_WIDTH:cw + SGU_WIDTH + POOL_WIDTH] = pooled.astype(BF16)


def _mixers(zm, ctx_tiles, seq_tiles_ctx, seq_tiles_lat, *tables):
    n = zm.shape[0]
    n_tiles = n // TILE
    per = TILE // HALO
    n_halo = n // HALO
    return pl.pallas_call(
        functools.partial(_mixers_kernel, ctx_tiles, seq_tiles_ctx, seq_tiles_lat),
        grid=(n_tiles,),
        in_specs=[
            pl.BlockSpec((TILE, MX_COLS), lambda i: (i, 0)),
            pl.BlockSpec((HALO, MX_COLS), lambda i: (jnp.maximum(i * per - 1, 0), 0)),
            pl.BlockSpec((HALO, MX_COLS), lambda i: (jnp.minimum((i + 1) * per, n_halo - 1), 0)),
            *[_const_spec(a.shape) for a in tables],
        ],
        out_specs=pl.BlockSpec((TILE, MX_OUT), lambda i: (i, 0)),
        out_shape=jax.ShapeDtypeStruct((n, MX_OUT), BF16),
        scratch_shapes=[
            pltpu.VMEM((TILE + 2 * HALO, CONV_WIDTH), F32),
            pltpu.VMEM((SUBLANES, SHIFTED_ROWS, CONV_WIDTH), F32),
        ],
        compiler_params=_cparams(("parallel",)),
        name="mixers",
    )(zm, zm, zm, *tables)


def _route(logits_t):
    m = jnp.max(logits_t, axis=0, keepdims=True)
    e = jnp.exp(logits_t - m)
    p = e / jnp.sum(e, axis=0, keepdims=True)
    best_score = None
    best = None
    for g in range(N_EXPERT_GROUPS):
        rows = [p[g * EXPERTS_PER_GROUP + q:g * EXPERTS_PER_GROUP + q + 1] for q in range(EXPERTS_PER_GROUP)]
        score = None
        for a in range(EXPERTS_PER_GROUP):
            for b in range(a + 1, EXPERTS_PER_GROUP):
                pair = rows[a] + rows[b]
                score = pair if score is None else jnp.maximum(score, pair)
        if g == 0:
            best_score, best = score, jnp.zeros(score.shape, jnp.int32)
        else:
            upd = score > best_score
            best = jnp.where(upd, g, best)
            best_score = jnp.where(upd, score, best_score)
    eidx = lax.broadcasted_iota(jnp.int32, p.shape, 0)
    neg = -jnp.inf
    masked = jnp.where(eidx // EXPERTS_PER_GROUP == best, p, neg)
    m1 = jnp.max(masked, axis=0, keepdims=True)
    i1 = jnp.min(jnp.where(masked == m1, eidx, N_EXPERTS), axis=0, keepdims=True)
    masked2 = jnp.where(eidx == i1, neg, masked)
    m2 = jnp.max(masked2, axis=0, keepdims=True)
    i2 = jnp.min(jnp.where(masked2 == m2, eidx, N_EXPERTS), axis=0, keepdims=True)
    tot = m1 + m2
    return jnp.concatenate([i1, i2], axis=0), jnp.concatenate([m1 / tot, m2 / tot], axis=0)


def _merge_kernel(x_ref, mod_ref, n1_ref, n2_ref, ya_ref, mx_ref, wg_ref, bg_ref, wro_ref, wco_ref, wso_ref,
                  wpo_ref, wout_ref, rw_ref, rb_ref, x1_ref, h2_ref, ri_ref, rwt_ref):
    d = D_MODEL
    x = x_ref[...]
    hb = _norm_mod(x, n1_ref[...], mod_ref[:, 0:d], mod_ref[:, d:2 * d]).astype(BF16)
    cw = CONV_WIDTH
    branches = (
        (ya_ref[...], wro_ref),
        (mx_ref[:, 0:cw], wco_ref),
        (mx_ref[:, cw:cw + SGU_WIDTH], wso_ref),
        (mx_ref[:, cw + SGU_WIDTH:MX_OUT], wpo_ref),
    )
    merged = None
    for bi, (y_in, w_ref) in enumerate(branches):
        zg = _dot(hb, wg_ref[:, bi * d:(bi + 1) * d]) + bg_ref[:, bi * d:(bi + 1) * d]
        term = _sigmoid(zg) * _dot(y_in, w_ref[...])
        merged = term if merged is None else merged + term
    x1 = x + mod_ref[:, 2 * d:3 * d] * _dot(merged.astype(BF16), wout_ref[...])
    x1_ref[...] = x1
    h2 = _norm_mod(x1, n2_ref[...], mod_ref[:, 3 * d:4 * d], mod_ref[:, 4 * d:5 * d])
    h2_ref[...] = h2
    logits = _dot3(h2, rw_ref[...])
    ri_ref[...], rwt_ref[...] = _route(logits.T[0:N_EXPERTS] + rb_ref[...])


def _merge(x, mod_l, mod_row, n1, n2, ya, mx, wg, bg, wro, wco, wso, wpo, wout, rw_t, rb):
    n = x.shape[0]
    tm = TILE
    consts = (n1, n2)
    weights = (wg, bg, wro, wco, wso, wpo, wout, rw_t, rb)
    return pl.pallas_call(
        _merge_kernel,
        grid=(n // tm,),
        in_specs=[
            pl.BlockSpec((tm, D_MODEL), lambda i: (i, 0)),
            pl.BlockSpec((None, 1, 6 * D_MODEL), lambda i: (mod_row(i), 0, 0)),
            *[_const_spec(a.shape) for a in consts],
            pl.BlockSpec((tm, RWKV_WIDTH), lambda i: (i, 0)),
            pl.BlockSpec((tm, MX_OUT), lambda i: (i, 0)),
            *[_const_spec(a.shape) for a in weights],
        ],
        out_specs=[
            pl.BlockSpec((tm, D_MODEL), lambda i: (i, 0)),
            pl.BlockSpec((tm, D_MODEL), lambda i: (i, 0)),
            pl.BlockSpec((TOP_K, tm), lambda i: (0, i)),
            pl.BlockSpec((TOP_K, tm), lambda i: (0, i)),
        ],
        out_shape=[
            jax.ShapeDtypeStruct((n, D_MODEL), F32),
            jax.ShapeDtypeStruct((n, D_MODEL), F32),
            jax.ShapeDtypeStruct((TOP_K, n), jnp.int32),
            jax.ShapeDtypeStruct((TOP_K, n), F32),
        ],
        compiler_params=_cparams(("parallel",)),
        name="merge_router",
    )(x, mod_l, n1, n2, ya, mx, *weights)


PAIRS_PER_GROUP = EXPERTS_PER_GROUP * (EXPERTS_PER_GROUP - 1) // 2
N_PAIR_CLASSES = N_EXPERT_GROUPS * PAIRS_PER_GROUP
ROUTED_TILE = 256


def _dispatch_plan(route_i, route_w):
    n = route_i.shape[1]
    tm = ROUTED_TILE
    n_tiles = n // tm + N_PAIR_CLASSES
    i1, i2 = route_i[0], route_i[1]
    first_lower = i1 < i2
    lo, hi = jnp.minimum(i1, i2), jnp.maximum(i1, i2)
    w_lo = jnp.where(first_lower, route_w[0], route_w[1])
    w_hi = jnp.where(first_lower, route_w[1], route_w[0])
    a, b = lo % EXPERTS_PER_GROUP, hi % EXPERTS_PER_GROUP
    assert EXPERTS_PER_GROUP == 4
    pair = a * (2 * EXPERTS_PER_GROUP - 1 - a) // 2 + (b - a - 1)
    pair = jnp.where(pair == 3, 4, jnp.where(pair == 4, 3, pair))
    cls = (lo // EXPERTS_PER_GROUP) * PAIRS_PER_GROUP + pair
    onehot = (cls[:, None] == jnp.arange(N_PAIR_CLASSES)[None, :]).astype(jnp.int32)
    counts = jnp.sum(onehot, axis=0)
    rank = jnp.sum((jnp.cumsum(onehot, axis=0) - onehot) * onehot, axis=1)
    padded = (counts + tm - 1) // tm * tm
    ends = jnp.cumsum(padded)
    dest = (ends - padded)[cls] + rank
    tile_row0 = jnp.arange(n_tiles, dtype=jnp.int32) * tm
    tile_valid = (tile_row0 < ends[-1]).astype(jnp.int32)
    tile_cls = jnp.minimum(jnp.searchsorted(ends, tile_row0, side="right"), N_PAIR_CLASSES - 1)
    stride = 1 << 15
    assert n + tm <= stride and N_PAIR_CLASSES * stride < (1 << 30)
    unused = jnp.iinfo(jnp.int32).max
    j = jnp.arange(tm, dtype=jnp.int32)[None, :]
    pad_keys = jnp.where(j < (padded - counts)[:, None], jnp.arange(N_PAIR_CLASSES)[:, None] * stride + n + j, unused)
    keys = jnp.concatenate([cls * stride + jnp.arange(n, dtype=jnp.int32), pad_keys.reshape(-1).astype(jnp.int32)])
    no_weight = jnp.zeros((N_PAIR_CLASSES * tm,), F32)
    keys, w_lo_sorted, w_hi_sorted = lax.sort(
        (keys, jnp.concatenate([w_lo, no_weight]), jnp.concatenate([w_hi, no_weight])), num_keys=1)
    token = keys % stride
    src = jnp.where((token < n) & (keys != unused), token, 0)
    w_sorted = jnp.stack([w_lo_sorted, w_hi_sorted], axis=1)
    cls_ids = jnp.arange(N_PAIR_CLASSES)
    pa = jnp.array([0, 0, 0, 1, 1, 2], jnp.int32)
    pb = jnp.array([1, 2, 3, 3, 2, 3], jnp.int32)
    cls_lo = (cls_ids // PAIRS_PER_GROUP) * EXPERTS_PER_GROUP + pa[cls_ids % PAIRS_PER_GROUP]
    cls_hi = (cls_ids // PAIRS_PER_GROUP) * EXPERTS_PER_GROUP + pb[cls_ids % PAIRS_PER_GROUP]
    last_cls = tile_cls[jnp.maximum(ends[-1] // tm - 1, 0)]
    tile_cls = jnp.where(tile_valid == 1, tile_cls, last_cls)
    return dest.astype(jnp.int32), src, w_sorted, cls_lo[tile_cls].astype(jnp.int32), cls_hi[tile_cls].astype(jnp.int32)


def _start_row_gather(idx_ref, idx0, rows_hbm, buf, sem, n_rows):
    for r in range(n_rows):
        pltpu.make_async_copy(rows_hbm.at[pl.ds(idx_ref[idx0 + r], 1)], buf.at[pl.ds(r, 1)], sem).start()


def _wait_row_gather(rows_hbm, buf, sem, n_rows):
    pltpu.make_async_copy(rows_hbm.at[pl.ds(0, n_rows)], buf, sem).wait()


def _pipelined_gather(idx_ref, rows_hbm, bufs, sems, n_rows):
    i = pl.program_id(0)
    slot = i % 2

    @pl.when(i == 0)
    def _():
        _start_row_gather(idx_ref, 0, rows_hbm, bufs.at[0], sems.at[0], n_rows)

    @pl.when(i + 1 < pl.num_programs(0))
    def _():
        _start_row_gather(idx_ref, (i + 1) * n_rows, rows_hbm, bufs.at[1 - slot], sems.at[1 - slot], n_rows)

    _wait_row_gather(rows_hbm, bufs.at[slot], sems.at[slot], n_rows)
    return slot


EXPERT_GATHER_DEPTH = 2


def _experts_kernel(lo_ref, hi_ref, src_ref, w_ref, h_hbm, wg_lo, wu_lo, wd_lo, wg_hi, wu_hi, wd_hi,
                    o_ref, hbuf, sems):
    del lo_ref, hi_ref
    tm = ROUTED_TILE
    n_slots = EXPERT_GATHER_DEPTH + 1
    i = pl.program_id(0)
    n = pl.num_programs(0)
    slot = i % n_slots

    @pl.when(i == 0)
    def _():
        for t in range(EXPERT_GATHER_DEPTH):
            @pl.when(t < n)
            def _():
                _start_row_gather(src_ref, t * tm, h_hbm, hbuf.at[t], sems.at[t], tm)

    _wait_row_gather(h_hbm, hbuf.at[slot], sems.at[slot], tm)

    def tile(prefetch):
        ahead = (i + EXPERT_GATHER_DEPTH) % n_slots
        rows = iter(range(tm))

        def request(count):
            if not prefetch:
                return
            for r in (next(rows) for _ in range(count)):
                row = src_ref[(i + EXPERT_GATHER_DEPTH) * tm + r]
                pltpu.make_async_copy(h_hbm.at[pl.ds(row, 1)], hbuf.at[ahead, pl.ds(r, 1)], sems.at[ahead]).start()

        hb = hbuf[slot].astype(BF16)
        per_dot = tm // 6

        def expert(wg, wu, wd):
            request(per_dot)
            gate = _dot(hb, wg[...].astype(BF16))
            request(per_dot)
            up = _dot(hb, wu[...].astype(BF16))
            request(per_dot)
            return _dot((_silu(gate) * up).astype(BF16), wd[...].astype(BF16))

        w = w_ref[...]
        out = w[:, 0:1] * expert(wg_lo, wu_lo, wd_lo) + w[:, 1:2] * expert(wg_hi, wu_hi, wd_hi)
        request(tm - 6 * per_dot)
        o_ref[...] = out

    @pl.when(i + EXPERT_GATHER_DEPTH < n)
    def _():
        tile(True)

    @pl.when(i + EXPERT_GATHER_DEPTH >= n)
    def _():
        tile(False)


def _experts(h2, src, w_sorted, tile_lo, tile_hi, layer, wg, wu, wd):
    tm = ROUTED_TILE
    n_tiles = tile_lo.shape[0]
    n_slots = EXPERT_GATHER_DEPTH + 1
    up_spec = lambda sel: pl.BlockSpec((None, None, D_MODEL, D_EXPERT), lambda i, lo, hi, s: (layer, sel(lo, hi)[i], 0, 0))
    down_spec = lambda sel: pl.BlockSpec((None, None, D_EXPERT, D_MODEL), lambda i, lo, hi, s: (layer, sel(lo, hi)[i], 0, 0))
    pick_lo = lambda lo, hi: lo
    pick_hi = lambda lo, hi: hi
    return pl.pallas_call(
        _experts_kernel,
        grid_spec=pltpu.PrefetchScalarGridSpec(
            num_scalar_prefetch=3,
            grid=(n_tiles,),
            in_specs=[
                pl.BlockSpec((tm, TOP_K), lambda i, *_: (i, 0)),
                pl.BlockSpec(memory_space=pl.ANY),
                up_spec(pick_lo), up_spec(pick_lo), down_spec(pick_lo),
                up_spec(pick_hi), up_spec(pick_hi), down_spec(pick_hi),
            ],
            out_specs=pl.BlockSpec((tm, D_MODEL), lambda i, *_: (i, 0)),
            scratch_shapes=[pltpu.VMEM((n_slots, tm, D_MODEL), F32), pltpu.SemaphoreType.DMA((n_slots,))],
        ),
        out_shape=jax.ShapeDtypeStruct((n_tiles * tm, D_MODEL), F32),
        compiler_params=_cparams(("arbitrary",)),
        name="experts",
    )(tile_lo, tile_hi, src, w_sorted, h2, wg, wu, wd, wg, wu, wd)


def _combine_kernel(dest_ref, x_ref, mod_ref, ff_hbm, o_ref, fbuf, sems):
    slot = _pipelined_gather(dest_ref, ff_hbm, fbuf, sems, ROUTED_TILE)
    o_ref[...] = x_ref[...] + mod_ref[:, 5 * D_MODEL:6 * D_MODEL] * fbuf[slot]


def _combine(ff_sorted, dest, x1, mod_l, mod_row):
    n = x1.shape[0]
    tm = ROUTED_TILE
    return pl.pallas_call(
        _combine_kernel,
        grid_spec=pltpu.PrefetchScalarGridSpec(
            num_scalar_prefetch=1,
            grid=(n // tm,),
            in_specs=[
                pl.BlockSpec((tm, D_MODEL), lambda i, d: (i, 0)),
                pl.BlockSpec((None, 1, 6 * D_MODEL), lambda i, d: (mod_row(i), 0, 0)),
                pl.BlockSpec(memory_space=pl.ANY),
            ],
            out_specs=pl.BlockSpec((tm, D_MODEL), lambda i, d: (i, 0)),
            scratch_shapes=[pltpu.VMEM((2, tm, D_MODEL), F32), pltpu.SemaphoreType.DMA((2,))],
        ),
        out_shape=jax.ShapeDtypeStruct((n, D_MODEL), F32),
        compiler_params=_cparams(("arbitrary",)),
        name="moe_combine",
    )(dest, x1, mod_l, ff_sorted)


def _final_norm_kernel(x_ref, g_ref, o_ref):
    x = x_ref[...]
    o_ref[...] = x * lax.rsqrt(jnp.mean(x * x, axis=-1, keepdims=True) + NORM_EPS) * g_ref[...]


def _final_norm(x, g, tile0, n_tiles):
    return pl.pallas_call(
        _final_norm_kernel,
        grid=(n_tiles,),
        in_specs=[pl.BlockSpec((TILE, D_MODEL), lambda i: (i + tile0, 0)), _const_spec((1, D_MODEL))],
        out_specs=pl.BlockSpec((TILE, D_MODEL), lambda i: (i, 0)),
        out_shape=jax.ShapeDtypeStruct((n_tiles * TILE, D_MODEL), F32),
        compiler_params=_cparams(("parallel",)),
        name="final_norm",
    )(x, g)


def _grid_pos_embed(n_tokens):
    rows = n_tokens // GRID_W
    quarter = D_MODEL // 4
    half = D_MODEL // 2
    omega = 1.0 / (10000.0 ** (jnp.arange(quarter, dtype=F32) / quarter))
    ang_r = jnp.arange(rows, dtype=F32)[:, None] * omega
    ang_c = jnp.arange(GRID_W, dtype=F32)[:, None] * omega
    emb_r = jnp.concatenate([jnp.sin(ang_r), jnp.cos(ang_r)], axis=-1)
    emb_c = jnp.concatenate([jnp.sin(ang_c), jnp.cos(ang_c)], axis=-1)
    emb = jnp.concatenate([jnp.broadcast_to(emb_r[:, None, :], (rows, GRID_W, half)),
                           jnp.broadcast_to(emb_c[None, :, :], (rows, GRID_W, half))], axis=-1)
    return emb.reshape(rows * GRID_W, D_MODEL)


def _pad_rows(w, rows, offset):
    out = jnp.zeros(w.shape[:-2] + (rows, w.shape[-1]), w.dtype)
    return lax.dynamic_update_slice_in_dim(out, w, offset, axis=-2)


def _hg_cols(w):
    return jnp.moveaxis(w.reshape(w.shape[:-1] + (N_HG, HG_LANES)), -2, 0)


def kernel(x_prompt, x_sample, state_rwkv, c, c_ctx, norm1_g, norm2_g, w_mod, b_mod, w_in, b_in, rwkv_w0, rwkv_w_up,
           rwkv_a0, rwkv_a_up, rwkv_g_up, rwkv_k_k, rwkv_k_a, rwkv_r_k, rwkv_gn_g, rwkv_gn_b, rwkv_w_o, conv_dw,
           conv_dw_b, conv_ln_g, conv_ln_b, conv_w_o, sgu_ln_g, sgu_ln_b, sgu_w_s, sgu_b_s, sgu_w_o, pool_w,
           pool_scale, pool_w_o, w_out, moe_w_gate, moe_w_up, moe_w_down, router_w, router_b, final_norm_g):
    n_ctx, t_ctx, d = x_prompt.shape
    n_lat, t_lat, _ = x_sample.shape
    n_layers = w_in.shape[0]
    assert d == D_MODEL and t_ctx % TILE == 0 and t_lat % TILE == 0 and n_lat < MOD_ROWS
    ctx_rows, lat_rows = n_ctx * t_ctx, n_lat * t_lat
    assert ctx_rows % t_lat == 0 and t_lat % PROJ_TILE == 0 and ROUTED_TILE == TILE
    ctx_tiles = ctx_rows // TILE
    seq_tiles_ctx, seq_tiles_lat = t_ctx // TILE, t_lat // TILE

    def mod_row_for(tile_rows):
        ctx_t = ctx_rows // tile_rows
        per_lat = t_lat // tile_rows
        return lambda i: jnp.where(i < ctx_t, n_lat, jnp.maximum(i - ctx_t, 0) // per_lat)

    mod_row = mod_row_for(TILE)
    mod_row_proj = mod_row_for(PROJ_TILE)

    cond = jnp.zeros((MOD_ROWS, d), F32).at[:n_lat].set(c).at[n_lat].set(c_ctx)
    mod = _modulation(cond, w_mod, b_mod).reshape(n_layers, MOD_ROWS, 1, 6 * d)

    x = _embed(x_prompt.reshape(ctx_rows, d), x_sample.reshape(lat_rows, d), _grid_pos_embed(t_lat), seq_tiles_lat)

    n_rw = RW_COLS
    n_mx = MX_COLS
    wa = w_in[:, :, :n_rw].astype(BF16)
    ba = b_in[:, None, :n_rw]
    wm = w_in[:, :, n_rw:n_rw + n_mx].astype(BF16)
    bm = b_in[:, None, n_rw:n_rw + n_mx]
    wgt = w_in[:, :, n_rw + n_mx:].astype(BF16)
    bgt = b_in[:, None, n_rw + n_mx:]

    par = jnp.zeros((n_layers, P_ROWS, RWKV_WIDTH), F32)
    par = par.at[:, P_W0:P_W0 + 2].set(rwkv_w0).at[:, P_A0:P_A0 + 2].set(rwkv_a0)
    par = par.at[:, P_KK].set(rwkv_k_k).at[:, P_KA].set(rwkv_k_a)
    par = par.at[:, P_RK].set(rwkv_r_k.reshape(n_layers, RWKV_WIDTH))
    par = par.at[:, P_GNG].set(rwkv_gn_g).at[:, P_GNB].set(rwkv_gn_b)
    par = jnp.moveaxis(_hg_cols(par), 0, 1)
    lr_k = RWKV_DECAY_RANK + RWKV_ICLR_RANK
    wup = jnp.moveaxis(_hg_cols(_pad_rows(rwkv_w_up, lr_k, 0)), 0, 2).astype(BF16)
    aup = jnp.moveaxis(_hg_cols(_pad_rows(rwkv_a_up, lr_k, RWKV_DECAY_RANK)), 0, 2).astype(BF16)
    gup = jnp.moveaxis(_hg_cols(rwkv_g_up), 0, 1).astype(BF16)
    rw_consts = _rwkv_consts()

    mix_vecs = jnp.zeros((n_layers, 8, CONV_WIDTH), F32)
    for row, vec in enumerate((conv_dw_b, conv_ln_g, conv_ln_b, sgu_ln_g, sgu_ln_b, pool_scale)):
        mix_vecs = mix_vecs.at[:, row].set(vec)
    sgu_wcat = jnp.transpose(sgu_w_s, (0, 2, 1, 3)).reshape(n_layers, SGU_CHUNK, SGU_GROUPS * SGU_CHUNK).astype(BF16)
    sgu_bs = jnp.repeat(jnp.swapaxes(sgu_b_s, 1, 2), SGU_WIDTH // SGU_GROUPS, axis=2)
    pc = POOL_WIDTH // POOL_GROUPS
    pool_bd = (pool_w[:, :, :, None, :] * jnp.eye(POOL_GROUPS, dtype=F32)[None, :, None, :, None])
    pool_bd = pool_bd.reshape(n_layers, POOL_WIDTH, POOL_WIDTH).astype(BF16)
    lane_grp = jnp.arange(SGU_WIDTH)[None, :] // (SGU_WIDTH // SGU_GROUPS)
    sgm = (jnp.arange(SGU_GROUPS * SGU_CHUNK)[:, None] // SGU_CHUNK == lane_grp).astype(BF16)
    wlen = TILE + 2 * HALO
    plm = (jnp.arange(POOL_GROUPS * wlen)[:, None] // wlen == jnp.arange(POOL_WIDTH)[None, :] // pc).astype(BF16)
    pool_band, pool_cnt = _pool_tables()

    wro, wco, wso, wpo, wout = (w.astype(BF16) for w in (rwkv_w_o, conv_w_o, sgu_w_o, pool_w_o, w_out))
    rw_t = jnp.pad(router_w, ((0, 0), (0, ROUTER_LANES - N_EXPERTS)))
    rb = router_b[:, None]

    ctx_s0 = jnp.zeros((n_ctx, N_HG, 2, RWKV_HEAD_DIM, HG_LANES), F32)
    ctx_states = []
    for l in range(n_layers):
        zr, zm = _in_projection(x, mod[l], mod_row_proj, norm1_g[l][None], wa[l], ba[l], wm[l], bm[l])
        rw_args = (par[l], wup[l], aup[l], gup[l], rw_consts)
        ya = jnp.zeros((x.shape[0], RWKV_WIDTH), BF16)
        ya, s_fin = _rwkv_mixer(zr, ctx_s0, n_ctx, t_ctx, 0, ya, *rw_args)
        ya, _ = _rwkv_mixer(zr, _states_to_kernel(state_rwkv[:, l]), n_lat, t_lat, ctx_rows // t_lat, ya, *rw_args)
        ctx_states.append(_states_from_kernel(s_fin))
        mx = _mixers(zm, ctx_tiles, seq_tiles_ctx, seq_tiles_lat, conv_dw[l], mix_vecs[l], sgu_wcat[l], sgu_bs[l],
                     pool_bd[l], sgm, plm, pool_band, pool_cnt)
        x1, h2, route_i, route_w = _merge(x, mod[l], mod_row, norm1_g[l][None], norm2_g[l][None], ya, mx, wgt[l], bgt[l],
                                 wro[l], wco[l], wso[l], wpo[l], wout[l], rw_t, rb)
        dest, src, w_sorted, tile_lo, tile_hi = _dispatch_plan(route_i, route_w)
        ff = _experts(h2, src, w_sorted, tile_lo, tile_hi, l, moe_w_gate, moe_w_up, moe_w_down)
        x = _combine(ff, dest, x1, mod[l], mod_row)

    g_fin = final_norm_g[None]
    y_prompt = _final_norm(x, g_fin, 0, ctx_tiles).reshape(n_ctx, t_ctx, d)
    y_sample = _final_norm(x, g_fin, ctx_tiles, lat_rows // TILE).reshape(n_lat, t_lat, d)
    new_state = jnp.stack(ctx_states, axis=1).astype(x_prompt.dtype)
    return (y_prompt, y_sample, new_state)
```

```python
import functools
import math

import jax
import jax.numpy as jnp
from jax import lax
from jax.experimental import pallas as pl
from jax.experimental.pallas import tpu as pltpu

F32 = jnp.float32
BF16 = jnp.bfloat16

D_MODEL = 1024
GRID_W = 64
RWKV_HEADS = 8
RWKV_HEAD_DIM = 64
RWKV_WIDTH = RWKV_HEADS * RWKV_HEAD_DIM
RWKV_DECAY_RANK = 64
RWKV_ICLR_RANK = 64
RWKV_GATE_RANK = 128
RWKV_GN_EPS = 64e-5
CONV_WIDTH = 256
CONV_KERNEL = 31
SGU_GROUPS = 4
SGU_WIDTH = 256
SGU_CHUNK = 128
POOL_GROUPS = 4
POOL_WIDTH = 256
POOL_WINDOWS = (2, 4, 8, 16)
N_BRANCHES = 4
N_EXPERTS = 16
TOP_K = 2
N_EXPERT_GROUPS = 4
EXPERTS_PER_GROUP = N_EXPERTS // N_EXPERT_GROUPS
D_EXPERT = 512
NORM_EPS = 1e-6
LN_EPS = 1e-5

RW_COLS = 3 * RWKV_WIDTH + RWKV_DECAY_RANK + RWKV_ICLR_RANK + RWKV_GATE_RANK
MX_COLS = 2 * CONV_WIDTH + 2 * SGU_WIDTH + POOL_WIDTH
MX_OUT = CONV_WIDTH + SGU_WIDTH + POOL_WIDTH

TILE = 256
PROJ_TILE = 512
HALO = 16
SUBLANES = 8
SHIFTED_ROWS = TILE + 2 * HALO - SUBLANES
HG_LANES = 256
HEADS_PER_HG = HG_LANES // RWKV_HEAD_DIM
N_HG = RWKV_WIDTH // HG_LANES
SCAN_CHUNK = 64
assert SCAN_CHUNK == RWKV_HEAD_DIM
CHUNKS_PER_TRIP = 2
DECAY_SCALE = math.exp(-0.5)
MOD_ROWS = 16
ROUTER_LANES = 128

VMEM_LIMIT = 48 * 1024 * 1024
RWKV_VMEM_LIMIT = 56 * 1024 * 1024


def _cparams(sem, vmem_limit=VMEM_LIMIT):
    return pltpu.CompilerParams(dimension_semantics=sem, vmem_limit_bytes=vmem_limit)


def _dot(a, b):
    return jnp.dot(a, b, preferred_element_type=F32)


def _split2(x):
    hi = x.astype(BF16)
    lo = (x - hi.astype(F32)).astype(BF16)
    return hi, lo


def _dot3(a, b):
    a_hi, a_lo = _split2(a)
    b_hi, b_lo = _split2(b)
    return _dot(a_hi, b_hi) + (_dot(a_lo, b_hi) + _dot(a_hi, b_lo))


def _dot_exact_rhs(x, rhs):
    hi, lo = _split2(x)
    return _dot(hi, rhs) + _dot(lo, rhs)


def _dot_exact_lhs(lhs, x):
    hi, lo = _split2(x)
    return _dot(lhs, hi) + _dot(lhs, lo)


def _sigmoid(x):
    return 1.0 / (1.0 + jnp.exp(-x))


def _silu(x):
    return x * _sigmoid(x)


def _norm_mod(x, g, shift, scale):
    y = x * lax.rsqrt(jnp.mean(x * x, axis=-1, keepdims=True) + NORM_EPS) * g
    return y * (1.0 + scale) + shift


def _layer_norm(x, g, b):
    mu = jnp.mean(x, axis=-1, keepdims=True)
    xc = x - mu
    var = jnp.mean(xc * xc, axis=-1, keepdims=True)
    return xc * lax.rsqrt(var + LN_EPS) * g + b


def _const_spec(shape):
    nd = len(shape)
    return pl.BlockSpec(shape, lambda *_: (0,) * nd)


def _mod_kernel(c_ref, w_ref, b_ref, o_ref):
    o_ref[...] = _dot3(_silu(c_ref[...]), w_ref[...]) + b_ref[...]


def _modulation(cond, w_mod, b_mod):
    n_layers = w_mod.shape[0]
    tn = 1536
    return pl.pallas_call(
        _mod_kernel,
        grid=(n_layers, 6 * D_MODEL // tn),
        in_specs=[
            pl.BlockSpec((MOD_ROWS, D_MODEL), lambda l, j: (0, 0)),
            pl.BlockSpec((None, D_MODEL, tn), lambda l, j: (l, 0, j)),
            pl.BlockSpec((None, 1, tn), lambda l, j: (l, 0, j)),
        ],
        out_specs=pl.BlockSpec((None, MOD_ROWS, tn), lambda l, j: (l, 0, j)),
        out_shape=jax.ShapeDtypeStruct((n_layers, MOD_ROWS, 6 * D_MODEL), F32),
        compiler_params=_cparams(("parallel", "parallel")),
        name="modulation",
    )(cond, w_mod, b_mod.reshape(n_layers, 1, 6 * D_MODEL))


def _embed_kernel(n_ctx_tiles, xp_ref, xs_ref, pos_ref, o_ref):
    i = pl.program_id(0)

    @pl.when(i < n_ctx_tiles)
    def _():
        o_ref[...] = xp_ref[...]

    @pl.when(i >= n_ctx_tiles)
    def _():
        o_ref[...] = xs_ref[...] + pos_ref[...]


def _embed(xp, xs, pos, lat_tiles):
    n_ctx_tiles = xp.shape[0] // TILE
    n_lat_tiles = xs.shape[0] // TILE
    n_tiles = n_ctx_tiles + n_lat_tiles
    return pl.pallas_call(
        functools.partial(_embed_kernel, n_ctx_tiles),
        grid=(n_tiles,),
        in_specs=[
            pl.BlockSpec((TILE, D_MODEL), lambda i: (jnp.minimum(i, n_ctx_tiles - 1), 0)),
            pl.BlockSpec((TILE, D_MODEL), lambda i: (jnp.maximum(i - n_ctx_tiles, 0), 0)),
            pl.BlockSpec((TILE, D_MODEL), lambda i: (jnp.maximum(i - n_ctx_tiles, 0) % lat_tiles, 0)),
        ],
        out_specs=pl.BlockSpec((TILE, D_MODEL), lambda i: (i, 0)),
        out_shape=jax.ShapeDtypeStruct((n_tiles * TILE, D_MODEL), F32),
        compiler_params=_cparams(("parallel",)),
        name="embed",
    )(xp, xs, pos)


def _inproj_kernel(x_ref, mod_ref, g_ref, wa_ref, ba_ref, wm_ref, bm_ref, zr_ref, zm_ref):
    h = _norm_mod(x_ref[...], g_ref[...], mod_ref[:, 0:D_MODEL], mod_ref[:, D_MODEL:2 * D_MODEL])
    hb = h.astype(BF16)
    zr_ref[...] = _dot(hb, wa_ref[...]) + ba_ref[...]
    zm_ref[...] = (_dot(hb, wm_ref[...]) + bm_ref[...]).astype(BF16)


def _in_projection(x, mod_l, mod_row, norm_g, wa, ba, wm, bm):
    n = x.shape[0]
    tm = PROJ_TILE
    return pl.pallas_call(
        _inproj_kernel,
        grid=(n // tm,),
        in_specs=[
            pl.BlockSpec((tm, D_MODEL), lambda i: (i, 0)),
            pl.BlockSpec((None, 1, 6 * D_MODEL), lambda i: (mod_row(i), 0, 0)),
            _const_spec((1, D_MODEL)),
            _const_spec((D_MODEL, RW_COLS)),
            _const_spec((1, RW_COLS)),
            _const_spec((D_MODEL, MX_COLS)),
            _const_spec((1, MX_COLS)),
        ],
        out_specs=[
            pl.BlockSpec((tm, RW_COLS), lambda i: (i, 0)),
            pl.BlockSpec((tm, MX_COLS), lambda i: (i, 0)),
        ],
        out_shape=[
            jax.ShapeDtypeStruct((n, RW_COLS), F32),
            jax.ShapeDtypeStruct((n, MX_COLS), BF16),
        ],
        compiler_params=_cparams(("parallel",)),
        name="in_projection",
    )(x, mod_l, norm_g, wa, ba, wm, bm)


P_W0, P_A0, P_KK, P_KA, P_RK, P_GNG, P_GNB = 0, 2, 4, 5, 6, 7, 8
P_ROWS = 16


def _rep(xb, reps, mask):
    return jnp.concatenate([xb] * reps, axis=0) * mask


def _rwkv_kernel(t_len, y_all_ref, r_ref, k_ref, v_ref, lr_ref, s0_ref, par_ref, wup_ref, aup_ref, gup_ref,
                 bd_ref, cm_ref, tri_ref, eyec_ref, out_ref, sfin_ref, y_scr, st_scr):
    del y_all_ref
    c = SCAN_CHUNK
    n_chunks = t_len // c
    n_trips = n_chunks // CHUNKS_PER_TRIP
    lr_k = RWKV_DECAY_RANK + RWKV_ICLR_RANK
    probs = [(d, sl, g) for sl in range(CHUNKS_PER_TRIP) for d in (0, 1) for g in range(N_HG)]
    n_sq = int(math.log2(c))

    def lanes(g):
        return slice(g * HG_LANES, (g + 1) * HG_LANES)

    def rep(xb):
        return _rep(xb, HEADS_PER_HG, bd_ref[...])

    def seg_sums(xs, passes):
        n = xs[0].shape[0]
        if passes == 1:
            out = _dot(jnp.concatenate([x.astype(BF16) for x in xs], axis=0), bd_ref[...])
            return [out[i * n:(i + 1) * n] for i in range(len(xs))]
        parts = []
        for x in xs:
            parts.extend(_split2(x))
        out = _dot(jnp.concatenate(parts, axis=0), bd_ref[...])
        return [out[2 * i * n:(2 * i + 1) * n] + out[(2 * i + 1) * n:(2 * i + 2) * n] for i in range(len(xs))]

    def fold(x_bd):
        n = x_bd.shape[0] // HEADS_PER_HG
        return (x_bd[0:n] + x_bd[n:2 * n]) + (x_bd[2 * n:3 * n] + x_bd[3 * n:4 * n])

    def par(g, row):
        return par_ref[g, row:row + 1, :]

    st_scr[...] = s0_ref[...]

    def trip(j, first_touch):
        rows = {}
        for sl in range(CHUNKS_PER_TRIP):
            rows[0, sl] = pl.ds(pl.multiple_of((CHUNKS_PER_TRIP * j + sl) * c, c), c)
            rows[1, sl] = pl.ds(pl.multiple_of((n_chunks - 1 - CHUNKS_PER_TRIP * j - sl) * c, c), c)
        lr = {key: lr_ref[rw, 0:lr_k] for key, rw in rows.items()}
        th = {key: jnp.tanh(x).astype(BF16) for key, x in lr.items()}
        lrb = {key: x.astype(BF16) for key, x in lr.items()}
        ss = [dict(r=r_ref[rows[d, sl], lanes(g)], k=k_ref[rows[d, sl], lanes(g)], v=v_ref[rows[d, sl], lanes(g)])
              for d, sl, g in probs]

        for (d, sl, g), s in zip(probs, ss):
            s["w_pre"] = par(g, P_W0 + d) + _dot(th[d, sl], wup_ref[d, g])
            s["a_pre"] = par(g, P_A0 + d) + _dot(lrb[d, sl], aup_ref[d, g])
            s["kkr"] = s["k"] * par(g, P_KK)
        for s, ssq in zip(ss, seg_sums([s["kkr"] * s["kkr"] for s in ss], 1)):
            s["ssq"] = ssq
        for (d, sl, g), s in zip(probs, ss):
            s["lw"] = -DECAY_SCALE * _sigmoid(s["w_pre"])
            s["cum"] = _dot_exact_lhs(tri_ref[d], s["lw"])
        for (d, sl, g), s in zip(probs, ss):
            a = _sigmoid(s["a_pre"])
            kk = s["kkr"] / jnp.maximum(jnp.sqrt(s["ssq"]), 1e-12)
            kd = s["k"] * (1.0 + (a - 1.0) * par(g, P_KA))
            cum = s["cum"]
            e_incl = jnp.exp(cum)
            e_neg = jnp.exp(-cum)
            gam = e_incl[c - 1:c, :] if d == 0 else e_incl[0:1, :]
            s["at"] = (-kk * jnp.exp(cum - s["lw"])).astype(BF16)
            s["rt"] = s["r"] * e_incl
            bt = kk * a * e_neg
            kt = kd * e_neg
            s["vb"] = s["v"].astype(BF16)
            s["bonus"] = s["r"] * kd * par(g, P_RK)
            s["gam_rows"] = _dot_exact_rhs(eyec_ref[...] * gam, bd_ref[...])
            tr = jnp.concatenate([bt, kt], axis=0).T
            swapped = pltpu.roll(tr, c, axis=1)
            first_half = lax.broadcasted_iota(jnp.int32, tr.shape, 1) < c
            bt_t = jnp.where(first_half, tr, swapped).astype(BF16)
            kt_t = jnp.where(first_half, swapped, tr).astype(BF16)
            s["bt_bd"] = jnp.concatenate([bt_t, bt_t], axis=1) * bd_ref[...]
            s["kt_bd"] = jnp.concatenate([kt_t, kt_t], axis=1) * bd_ref[...]
            lhs2 = jnp.concatenate([s["at"], s["rt"].astype(BF16)], axis=0)
            s["m_b"] = _dot(lhs2, s["bt_bd"])
            s["m_k"] = _dot(lhs2, s["kt_bd"])
        for s, bsum in zip(ss, seg_sums([s["bonus"] for s in ss], 1)):
            s["bsum"] = bsum
        for (d, sl, g), s in zip(probs, ss):
            causal = cm_ref[d]
            m_b = s["m_b"] * causal
            s["x"] = m_b[0:c]
            s["a_rb"] = m_b[c:2 * c].astype(BF16)
            s["t"] = eyec_ref[...] + s["x"]
            s["bt_c"] = fold(s["bt_bd"])
            av = _dot(jnp.concatenate([(s["m_k"] * causal).astype(BF16), fold(s["kt_bd"])], axis=0), rep(s["vb"]))
            s["g1"] = av[0:c].astype(BF16)
            s["y0"] = av[c:2 * c] + s["bsum"] * s["v"]
            s["z"] = av[2 * c:2 * c + RWKV_HEAD_DIM]

        for step in range(1, n_sq):
            for s in ss:
                x_bd = rep(s["x"].astype(BF16))
                if step == 1:
                    s["x"] = _dot(s["x"].astype(BF16), x_bd)
                else:
                    both = _dot(jnp.concatenate([s["x"], s["t"]], axis=0).astype(BF16), x_bd)
                    s["x"], s["t"] = both[0:c], s["t"] + both[c:2 * c]
        for s in ss:
            s["t"] = (s["t"] + _dot(s["t"].astype(BF16), rep(s["x"].astype(BF16)))).astype(BF16)

        for s in ss:
            s["w"] = _dot(s["t"], rep(s["at"])).astype(BF16)
            s["u0"] = _dot(s["t"], rep(s["g1"])).astype(BF16)
        for s in ss:
            lhs = jnp.concatenate([s["a_rb"], s["bt_c"]], axis=0)
            with_w = _dot(lhs, rep(s["w"]))
            with_u0 = _dot(lhs, rep(s["u0"]))
            s["q"] = (s["rt"] + with_w[0:c]).astype(BF16)
            s["p"] = (eyec_ref[...] + with_w[c:c + RWKV_HEAD_DIM]).astype(BF16)
            s["y0"] = s["y0"] + with_u0[0:c]
            s["z"] = s["z"] + with_u0[c:c + RWKV_HEAD_DIM]

        for (d, sl, g), s in zip(probs, ss):
            both = _dot(jnp.concatenate([s["p"], s["q"]], axis=0), rep(st_scr[g, d].astype(BF16)))
            st_scr[g, d] = s["gam_rows"] * (both[0:RWKV_HEAD_DIM] + s["z"])
            y = both[RWKV_HEAD_DIM:RWKV_HEAD_DIM + c] + s["y0"]
            if first_touch:
                y_scr[rows[d, sl], lanes(g)] = y
            else:
                y_scr[rows[d, sl], lanes(g)] += y

    half = n_trips // 2
    lax.fori_loop(0, half, lambda j, carry: (trip(j, True), carry)[1], 0)
    lax.fori_loop(half, n_trips, lambda j, carry: (trip(j, False), carry)[1], 0)
    sfin_ref[...] = st_scr[...]

    inv_n = 1.0 / RWKV_HEAD_DIM

    def finish(j, carry):
        rows = pl.ds(pl.multiple_of(j * TILE, TILE), TILE)
        sg = _sigmoid(lr_ref[rows, lr_k:HG_LANES]).astype(BF16)
        ys = [y_scr[rows, lanes(g)] for g in range(N_HG)]
        mus = [m * inv_n for m in seg_sums(ys, 2)]
        ycs = [y - mu for y, mu in zip(ys, mus)]
        variances = [v * inv_n for v in seg_sums([yc * yc for yc in ycs], 2)]
        for g in range(N_HG):
            yn = ycs[g] * lax.rsqrt(variances[g] + RWKV_GN_EPS) * par(g, P_GNG) + par(g, P_GNB)
            out_ref[rows, lanes(g)] = (yn * _dot(sg, gup_ref[g])).astype(BF16)
        return carry

    lax.fori_loop(0, t_len // TILE, finish, 0)


def _rwkv_mixer(zr, s0, n_seq, t_len, row_off, y_all, par, wup, aup, gup, consts):
    lr_block = 3 * RWKV_WIDTH // HG_LANES
    state_spec = pl.BlockSpec((None, N_HG, 2, RWKV_HEAD_DIM, HG_LANES), lambda b: (b, 0, 0, 0, 0))
    weights = (par, wup, aup, gup)
    return pl.pallas_call(
        functools.partial(_rwkv_kernel, t_len),
        grid=(n_seq,),
        in_specs=[
            pl.BlockSpec(memory_space=pl.ANY),
            pl.BlockSpec((t_len, RWKV_WIDTH), lambda b: (b + row_off, 0)),
            pl.BlockSpec((t_len, RWKV_WIDTH), lambda b: (b + row_off, 1)),
            pl.BlockSpec((t_len, RWKV_WIDTH), lambda b: (b + row_off, 2)),
            pl.BlockSpec((t_len, HG_LANES), lambda b: (b + row_off, lr_block)),
            state_spec,
            *[_const_spec(a.shape) for a in weights],
            *[_const_spec(a.shape) for a in consts],
        ],
        out_specs=[pl.BlockSpec((t_len, RWKV_WIDTH), lambda b: (b + row_off, 0)), state_spec],
        out_shape=[
            jax.ShapeDtypeStruct((zr.shape[0], RWKV_WIDTH), BF16),
            jax.ShapeDtypeStruct((n_seq, N_HG, 2, RWKV_HEAD_DIM, HG_LANES), F32),
        ],
        scratch_shapes=[
            pltpu.VMEM((t_len, RWKV_WIDTH), F32),
            pltpu.VMEM((N_HG, 2, RWKV_HEAD_DIM, HG_LANES), F32),
        ],
        input_output_aliases={0: 0},
        compiler_params=_cparams(("parallel",), RWKV_VMEM_LIMIT),
        name="rwkv_mixer",
    )(y_all, zr, zr, zr, zr, s0, *weights, *consts)


def _rwkv_consts():
    c = SCAN_CHUNK
    ri = jnp.arange(HG_LANES)[:, None]
    ci = jnp.arange(HG_LANES)[None, :]
    bd = (ri // RWKV_HEAD_DIM == ci // RWKV_HEAD_DIM).astype(BF16)
    t = jnp.arange(c)[:, None]
    i = (jnp.arange(HEADS_PER_HG * c) % c)[None, :]
    cm = jnp.stack([jnp.concatenate([i < t, i <= t]), jnp.concatenate([i > t, i >= t])]).astype(F32)
    ti = jnp.arange(c)[None, :]
    tri = jnp.stack([ti <= t, ti >= t]).astype(BF16)
    eyec = (i == t).astype(F32)
    return bd, cm, tri, eyec


def _states_to_kernel(s):
    b = s.shape[0]
    st = s.reshape(b, 2, N_HG, HEADS_PER_HG, RWKV_HEAD_DIM, RWKV_HEAD_DIM)
    return jnp.transpose(st, (0, 2, 1, 5, 3, 4)).reshape(b, N_HG, 2, RWKV_HEAD_DIM, HG_LANES)


def _states_from_kernel(sk):
    b = sk.shape[0]
    st = sk.reshape(b, N_HG, 2, RWKV_HEAD_DIM, HEADS_PER_HG, RWKV_HEAD_DIM)
    return jnp.transpose(st, (0, 2, 1, 4, 5, 3)).reshape(b, 2, RWKV_HEADS, RWKV_HEAD_DIM, RWKV_HEAD_DIM)


def _pool_tables():
    wlen = TILE + 2 * HALO
    t_loc = jnp.arange(TILE)[:, None]
    s_rel = jnp.arange(wlen)[None, :] - HALO
    lane_g = jnp.arange(POOL_WIDTH)[None, :] // (POOL_WIDTH // POOL_GROUPS)
    bands, counts = [], []
    for first in (0, 1):
        for last in (0, 1):
            lo_seq = 0 if first else -HALO
            hi_seq = TILE if last else TILE + HALO
            band_groups = []
            cnt = jnp.zeros((TILE, POOL_WIDTH), F32)
            for gi, win in enumerate(POOL_WINDOWS):
                lo = jnp.maximum(t_loc - win // 2, lo_seq)
                hi = jnp.minimum(t_loc + win - win // 2, hi_seq)
                band_groups.append((s_rel >= lo) & (s_rel < hi))
                cnt = jnp.where(lane_g == gi, (hi - lo).astype(F32), cnt)
            bands.append(jnp.concatenate(band_groups, axis=1))
            counts.append(cnt)
    return jnp.stack(bands).astype(BF16), jnp.stack(counts)


def _mixers_kernel(ctx_tiles, seq_tiles_ctx, seq_tiles_lat, zc_ref, zp_ref, zn_ref, dw_ref, vec_ref, wcat_ref,
                   bs_ref, pw_ref, sgm_ref, plm_ref, band_ref, cnt_ref, out_ref, win_scr, shift_scr):
    i = pl.program_id(0)
    j_ctx = i % seq_tiles_ctx
    j_lat = jnp.maximum(i - ctx_tiles, 0) % seq_tiles_lat
    is_ctx = i < ctx_tiles
    first = jnp.where(is_ctx, j_ctx == 0, j_lat == 0)
    last = jnp.where(is_ctx, j_ctx == seq_tiles_ctx - 1, j_lat == seq_tiles_lat - 1)
    keep_prev = jnp.where(first, 0.0, 1.0)
    keep_next = jnp.where(last, 0.0, 1.0)
    case = 2 * first.astype(jnp.int32) + last.astype(jnp.int32)

    cw = CONV_WIDTH
    conv_b, cln_g, cln_b = vec_ref[0:1, :], vec_ref[1:2, :], vec_ref[2:3, :]
    sln_g, sln_b, pool_scale = vec_ref[3:4, :], vec_ref[4:5, :], vec_ref[5:6, :]

    glu = lambda z_ref: z_ref[:, 0:cw].astype(F32) * _sigmoid(z_ref[:, cw:2 * cw].astype(F32))
    win_scr[0:HALO, :] = glu(zp_ref) * keep_prev
    win_scr[HALO:HALO + TILE, :] = glu(zc_ref)
    win_scr[HALO + TILE:2 * HALO + TILE, :] = glu(zn_ref) * keep_next
    for r in range(1, SUBLANES):
        shift_scr[r] = win_scr[pl.ds(r, SHIFTED_ROWS), :]
    acc = jnp.zeros((TILE, cw), F32) + conv_b
    pad = CONV_KERNEL // 2
    for j in range(CONV_KERNEL):
        q, r = divmod(HALO - pad + j, SUBLANES)
        rows = pl.ds(q * SUBLANES, TILE)
        tap = win_scr[rows, :] if r == 0 else shift_scr[r, rows, :]
        acc = acc + tap * dw_ref[j:j + 1, :]
    out_ref[:, 0:cw] = _silu(_layer_norm(acc, cln_g, cln_b)).astype(BF16)

    su = zc_ref[:, 2 * cw:2 * cw + SGU_WIDTH].astype(F32)
    sv = zc_ref[:, 2 * cw + SGU_WIDTH:2 * cw + 2 * SGU_WIDTH].astype(F32)
    vn = _layer_norm(sv, sln_g, sln_b).astype(BF16)
    sgm = sgm_ref[...]
    for ch in range(TILE // SGU_CHUNK):
        rows = slice(ch * SGU_CHUNK, (ch + 1) * SGU_CHUNK)
        s = _dot(wcat_ref[...], _rep(vn[rows], SGU_GROUPS, sgm)) + bs_ref[...]
        out_ref[rows, cw:cw + SGU_WIDTH] = (su[rows] * s).astype(BF16)

    zoff = 2 * cw + 2 * SGU_WIDTH
    zcur = zc_ref[:, zoff:zoff + POOL_WIDTH]
    zw = jnp.concatenate([zp_ref[:, zoff:zoff + POOL_WIDTH], zcur, zn_ref[:, zoff:zoff + POOL_WIDTH]], axis=0)
    psum = _dot(band_ref[case], _rep(zw, POOL_GROUPS, plm_ref[...]))
    p = psum / cnt_ref[case] - zcur.astype(F32)
    pooled = _dot(p.astype(BF16), pw_ref[...]) * pool_scale
    out_ref[:, cw + SGU_WIDTH:cw + SGU_WIDTH + POOL_WIDTH] = pooled.astype(BF16)


def _mixers(zm, ctx_tiles, seq_tiles_ctx, seq_tiles_lat, *tables):
    n = zm.shape[0]
    n_tiles = n // TILE
    per = TILE // HALO
    n_halo = n // HALO
    return pl.pallas_call(
        functools.partial(_mixers_kernel, ctx_tiles, seq_tiles_ctx, seq_tiles_lat),
        grid=(n_tiles,),
        in_specs=[
            pl.BlockSpec((TILE, MX_COLS), lambda i: (i, 0)),
            pl.BlockSpec((HALO, MX_COLS), lambda i: (jnp.maximum(i * per - 1, 0), 0)),
            pl.BlockSpec((HALO, MX_COLS), lambda i: (jnp.minimum((i + 1) * per, n_halo - 1), 0)),
            *[_const_spec(a.shape) for a in tables],
        ],
        out_specs=pl.BlockSpec((TILE, MX_OUT), lambda i: (i, 0)),
        out_shape=jax.ShapeDtypeStruct((n, MX_OUT), BF16),
        scratch_shapes=[
            pltpu.VMEM((TILE + 2 * HALO, CONV_WIDTH), F32),
            pltpu.VMEM((SUBLANES, SHIFTED_ROWS, CONV_WIDTH), F32),
        ],
        compiler_params=_cparams(("parallel",)),
        name="mixers",
    )(zm, zm, zm, *tables)


def _route(logits_t):
    m = jnp.max(logits_t, axis=0, keepdims=True)
    e = jnp.exp(logits_t - m)
    p = e / jnp.sum(e, axis=0, keepdims=True)
    best_score = None
    best = None
    for g in range(N_EXPERT_GROUPS):
        rows = [p[g * EXPERTS_PER_GROUP + q:g * EXPERTS_PER_GROUP + q + 1] for q in range(EXPERTS_PER_GROUP)]
        score = None
        for a in range(EXPERTS_PER_GROUP):
            for b in range(a + 1, EXPERTS_PER_GROUP):
                pair = rows[a] + rows[b]
                score = pair if score is None else jnp.maximum(score, pair)
        if g == 0:
            best_score, best = score, jnp.zeros(score.shape, jnp.int32)
        else:
            upd = score > best_score
            best = jnp.where(upd, g, best)
            best_score = jnp.where(upd, score, best_score)
    eidx = lax.broadcasted_iota(jnp.int32, p.shape, 0)
    neg = -jnp.inf
    masked = jnp.where(eidx // EXPERTS_PER_GROUP == best, p, neg)
    m1 = jnp.max(masked, axis=0, keepdims=True)
    i1 = jnp.min(jnp.where(masked == m1, eidx, N_EXPERTS), axis=0, keepdims=True)
    masked2 = jnp.where(eidx == i1, neg, masked)
    m2 = jnp.max(masked2, axis=0, keepdims=True)
    i2 = jnp.min(jnp.where(masked2 == m2, eidx, N_EXPERTS), axis=0, keepdims=True)
    tot = m1 + m2
    return jnp.concatenate([i1, i2], axis=0), jnp.concatenate([m1 / tot, m2 / tot], axis=0)


def _merge_kernel(x_ref, mod_ref, n1_ref, n2_ref, ya_ref, mx_ref, wg_ref, bg_ref, wro_ref, wco_ref, wso_ref,
                  wpo_ref, wout_ref, rw_ref, rb_ref, x1_ref, h2_ref, ri_ref, rwt_ref):
    d = D_MODEL
    x = x_ref[...]
    hb = _norm_mod(x, n1_ref[...], mod_ref[:, 0:d], mod_ref[:, d:2 * d]).astype(BF16)
    cw = CONV_WIDTH
    branches = (
        (ya_ref[...], wro_ref),
        (mx_ref[:, 0:cw], wco_ref),
        (mx_ref[:, cw:cw + SGU_WIDTH], wso_ref),
        (mx_ref[:, cw + SGU_WIDTH:MX_OUT], wpo_ref),
    )
    merged = None
    for bi, (y_in, w_ref) in enumerate(branches):
        zg = _dot(hb, wg_ref[:, bi * d:(bi + 1) * d]) + bg_ref[:, bi * d:(bi + 1) * d]
        term = _sigmoid(zg) * _dot(y_in, w_ref[...])
        merged = term if merged is None else merged + term
    x1 = x + mod_ref[:, 2 * d:3 * d] * _dot(merged.astype(BF16), wout_ref[...])
    x1_ref[...] = x1
    h2 = _norm_mod(x1, n2_ref[...], mod_ref[:, 3 * d:4 * d], mod_ref[:, 4 * d:5 * d])
    h2_ref[...] = h2
    logits = _dot3(h2, rw_ref[...])
    ri_ref[...], rwt_ref[...] = _route(logits.T[0:N_EXPERTS] + rb_ref[...])


def _merge(x, mod_l, mod_row, n1, n2, ya, mx, wg, bg, wro, wco, wso, wpo, wout, rw_t, rb):
    n = x.shape[0]
    tm = TILE
    consts = (n1, n2)
    weights = (wg, bg, wro, wco, wso, wpo, wout, rw_t, rb)
    return pl.pallas_call(
        _merge_kernel,
        grid=(n // tm,),
        in_specs=[
            pl.BlockSpec((tm, D_MODEL), lambda i: (i, 0)),
            pl.BlockSpec((None, 1, 6 * D_MODEL), lambda i: (mod_row(i), 0, 0)),
            *[_const_spec(a.shape) for a in consts],
            pl.BlockSpec((tm, RWKV_WIDTH), lambda i: (i, 0)),
            pl.BlockSpec((tm, MX_OUT), lambda i: (i, 0)),
            *[_const_spec(a.shape) for a in weights],
        ],
        out_specs=[
            pl.BlockSpec((tm, D_MODEL), lambda i: (i, 0)),
            pl.BlockSpec((tm, D_MODEL), lambda i: (i, 0)),
            pl.BlockSpec((TOP_K, tm), lambda i: (0, i)),
            pl.BlockSpec((TOP_K, tm), lambda i: (0, i)),
        ],
        out_shape=[
            jax.ShapeDtypeStruct((n, D_MODEL), F32),
            jax.ShapeDtypeStruct((n, D_MODEL), F32),
            jax.ShapeDtypeStruct((TOP_K, n), jnp.int32),
            jax.ShapeDtypeStruct((TOP_K, n), F32),
        ],
        compiler_params=_cparams(("parallel",)),
        name="merge_router",
    )(x, mod_l, n1, n2, ya, mx, *weights)


PAIRS_PER_GROUP = EXPERTS_PER_GROUP * (EXPERTS_PER_GROUP - 1) // 2
N_PAIR_CLASSES = N_EXPERT_GROUPS * PAIRS_PER_GROUP
ROUTED_TILE = 256


def _dispatch_plan(route_i, route_w):
    n = route_i.shape[1]
    tm = ROUTED_TILE
    n_tiles = n // tm + N_PAIR_CLASSES
    i1, i2 = route_i[0], route_i[1]
    first_lower = i1 < i2
    lo, hi = jnp.minimum(i1, i2), jnp.maximum(i1, i2)
    w_lo = jnp.where(first_lower, route_w[0], route_w[1])
    w_hi = jnp.where(first_lower, route_w[1], route_w[0])
    a, b = lo % EXPERTS_PER_GROUP, hi % EXPERTS_PER_GROUP
    assert EXPERTS_PER_GROUP == 4
    pair = a * (2 * EXPERTS_PER_GROUP - 1 - a) // 2 + (b - a - 1)
    pair = jnp.where(pair == 3, 4, jnp.where(pair == 4, 3, pair))
    cls = (lo // EXPERTS_PER_GROUP) * PAIRS_PER_GROUP + pair
    onehot = (cls[:, None] == jnp.arange(N_PAIR_CLASSES)[None, :]).astype(jnp.int32)
    counts = jnp.sum(onehot, axis=0)
    rank = jnp.sum((jnp.cumsum(onehot, axis=0) - onehot) * onehot, axis=1)
    padded = (counts + tm - 1) // tm * tm
    ends = jnp.cumsum(padded)
    dest = (ends - padded)[cls] + rank
    tile_row0 = jnp.arange(n_tiles, dtype=jnp.int32) * tm
    tile_valid = (tile_row0 < ends[-1]).astype(jnp.int32)
    tile_cls = jnp.minimum(jnp.searchsorted(ends, tile_row0, side="right"), N_PAIR_CLASSES - 1)
    stride = 1 << 15
    assert n + tm <= stride and N_PAIR_CLASSES * stride < (1 << 30)
    unused = jnp.iinfo(jnp.int32).max
    j = jnp.arange(tm, dtype=jnp.int32)[None, :]
    pad_keys = jnp.where(j < (padded - counts)[:, None], jnp.arange(N_PAIR_CLASSES)[:, None] * stride + n + j, unused)
    keys = jnp.concatenate([cls * stride + jnp.arange(n, dtype=jnp.int32), pad_keys.reshape(-1).astype(jnp.int32)])
    no_weight = jnp.zeros((N_PAIR_CLASSES * tm,), F32)
    keys, w_lo_sorted, w_hi_sorted = lax.sort(
        (keys, jnp.concatenate([w_lo, no_weight]), jnp.concatenate([w_hi, no_weight])), num_keys=1)
    token = keys % stride
    src = jnp.where((token < n) & (keys != unused), token, 0)
    w_sorted = jnp.stack([w_lo_sorted, w_hi_sorted], axis=1)
    cls_ids = jnp.arange(N_PAIR_CLASSES)
    pa = jnp.array([0, 0, 0, 1, 1, 2], jnp.int32)
    pb = jnp.array([1, 2, 3, 3, 2, 3], jnp.int32)
    cls_lo = (cls_ids // PAIRS_PER_GROUP) * EXPERTS_PER_GROUP + pa[cls_ids % PAIRS_PER_GROUP]
    cls_hi = (cls_ids // PAIRS_PER_GROUP) * EXPERTS_PER_GROUP + pb[cls_ids % PAIRS_PER_GROUP]
    last_cls = tile_cls[jnp.maximum(ends[-1] // tm - 1, 0)]
    tile_cls = jnp.where(tile_valid == 1, tile_cls, last_cls)
    tiles_used = (ends[-1:] // tm).astype(jnp.int32)
    return (dest.astype(jnp.int32), src, w_sorted, cls_lo[tile_cls].astype(jnp.int32),
            cls_hi[tile_cls].astype(jnp.int32), tiles_used)


def _start_row_gather(idx_ref, idx0, rows_hbm, buf, sem, n_rows):
    for r in range(n_rows):
        pltpu.make_async_copy(rows_hbm.at[pl.ds(idx_ref[idx0 + r], 1)], buf.at[pl.ds(r, 1)], sem).start()


def _wait_row_gather(rows_hbm, buf, sem, n_rows):
    pltpu.make_async_copy(rows_hbm.at[pl.ds(0, n_rows)], buf, sem).wait()


def _pipelined_gather(idx_ref, rows_hbm, bufs, sems, n_rows):
    i = pl.program_id(0)
    slot = i % 2

    @pl.when(i == 0)
    def _():
        _start_row_gather(idx_ref, 0, rows_hbm, bufs.at[0], sems.at[0], n_rows)

    @pl.when(i + 1 < pl.num_programs(0))
    def _():
        _start_row_gather(idx_ref, (i + 1) * n_rows, rows_hbm, bufs.at[1 - slot], sems.at[1 - slot], n_rows)

    _wait_row_gather(rows_hbm, bufs.at[slot], sems.at[slot], n_rows)
    return slot


EXPERT_GATHER_DEPTH = 2


def _experts_kernel(lo_ref, hi_ref, used_ref, src_ref, w_ref, h_hbm, wg_lo, wu_lo, wd_lo, wg_hi, wu_hi, wd_hi,
                    o_ref, hbuf, sems):
    del lo_ref, hi_ref
    tm = ROUTED_TILE
    n_slots = EXPERT_GATHER_DEPTH + 1
    i = pl.program_id(0)
    n = used_ref[0]
    slot = i % n_slots

    @pl.when(i == 0)
    def _():
        for t in range(EXPERT_GATHER_DEPTH):
            @pl.when(t < n)
            def _():
                _start_row_gather(src_ref, t * tm, h_hbm, hbuf.at[t], sems.at[t], tm)

    def tile(prefetch):
        _wait_row_gather(h_hbm, hbuf.at[slot], sems.at[slot], tm)
        ahead = (i + EXPERT_GATHER_DEPTH) % n_slots
        rows = iter(range(tm))

        def request(count):
            if not prefetch:
                return
            for r in (next(rows) for _ in range(count)):
                row = src_ref[(i + EXPERT_GATHER_DEPTH) * tm + r]
                pltpu.make_async_copy(h_hbm.at[pl.ds(row, 1)], hbuf.at[ahead, pl.ds(r, 1)], sems.at[ahead]).start()

        hb = hbuf[slot].astype(BF16)
        per_dot = tm // 6

        def expert(wg, wu, wd):
            request(per_dot)
            gate = _dot(hb, wg[...].astype(BF16))
            request(per_dot)
            up = _dot(hb, wu[...].astype(BF16))
            request(per_dot)
            return _dot((_silu(gate) * up).astype(BF16), wd[...].astype(BF16))

        w = w_ref[...]
        out = w[:, 0:1] * expert(wg_lo, wu_lo, wd_lo) + w[:, 1:2] * expert(wg_hi, wu_hi, wd_hi)
        request(tm - 6 * per_dot)
        o_ref[...] = out

    @pl.when(i + EXPERT_GATHER_DEPTH < n)
    def _():
        tile(True)

    @pl.when((i < n) & (i + EXPERT_GATHER_DEPTH >= n))
    def _():
        tile(False)

    @pl.when(i >= n)
    def _():
        o_ref[...] = jnp.zeros(o_ref.shape, F32)


def _experts(h2, src, w_sorted, tile_lo, tile_hi, tiles_used, layer, wg, wu, wd):
    tm = ROUTED_TILE
    n_tiles = tile_lo.shape[0]
    n_slots = EXPERT_GATHER_DEPTH + 1
    up_spec = lambda sel: pl.BlockSpec((None, None, D_MODEL, D_EXPERT), lambda i, lo, hi, u, s: (layer, sel(lo, hi)[i], 0, 0))
    down_spec = lambda sel: pl.BlockSpec((None, None, D_EXPERT, D_MODEL), lambda i, lo, hi, u, s: (layer, sel(lo, hi)[i], 0, 0))
    pick_lo = lambda lo, hi: lo
    pick_hi = lambda lo, hi: hi
    return pl.pallas_call(
        _experts_kernel,
        grid_spec=pltpu.PrefetchScalarGridSpec(
            num_scalar_prefetch=4,
            grid=(n_tiles,),
            in_specs=[
                pl.BlockSpec((tm, TOP_K), lambda i, *_: (i, 0)),
                pl.BlockSpec(memory_space=pl.ANY),
                up_spec(pick_lo), up_spec(pick_lo), down_spec(pick_lo),
                up_spec(pick_hi), up_spec(pick_hi), down_spec(pick_hi),
            ],
            out_specs=pl.BlockSpec((tm, D_MODEL), lambda i, *_: (i, 0)),
            scratch_shapes=[pltpu.VMEM((n_slots, tm, D_MODEL), F32), pltpu.SemaphoreType.DMA((n_slots,))],
        ),
        out_shape=jax.ShapeDtypeStruct((n_tiles * tm, D_MODEL), F32),
        compiler_params=_cparams(("arbitrary",)),
        name="experts",
    )(tile_lo, tile_hi, tiles_used, src, w_sorted, h2, wg, wu, wd, wg, wu, wd)


def _combine_kernel(dest_ref, x_ref, mod_ref, ff_hbm, o_ref, fbuf, sems):
    slot = _pipelined_gather(dest_ref, ff_hbm, fbuf, sems, ROUTED_TILE)
    o_ref[...] = x_ref[...] + mod_ref[:, 5 * D_MODEL:6 * D_MODEL] * fbuf[slot]


def _combine(ff_sorted, dest, x1, mod_l, mod_row):
    n = x1.shape[0]
    tm = ROUTED_TILE
    return pl.pallas_call(
        _combine_kernel,
        grid_spec=pltpu.PrefetchScalarGridSpec(
            num_scalar_prefetch=1,
            grid=(n // tm,),
            in_specs=[
                pl.BlockSpec((tm, D_MODEL), lambda i, d: (i, 0)),
                pl.BlockSpec((None, 1, 6 * D_MODEL), lambda i, d: (mod_row(i), 0, 0)),
                pl.BlockSpec(memory_space=pl.ANY),
            ],
            out_specs=pl.BlockSpec((tm, D_MODEL), lambda i, d: (i, 0)),
            scratch_shapes=[pltpu.VMEM((2, tm, D_MODEL), F32), pltpu.SemaphoreType.DMA((2,))],
        ),
        out_shape=jax.ShapeDtypeStruct((n, D_MODEL), F32),
        compiler_params=_cparams(("arbitrary",)),
        name="moe_combine",
    )(dest, x1, mod_l, ff_sorted)


def _final_norm_kernel(x_ref, g_ref, o_ref):
    x = x_ref[...]
    o_ref[...] = x * lax.rsqrt(jnp.mean(x * x, axis=-1, keepdims=True) + NORM_EPS) * g_ref[...]


def _final_norm(x, g, tile0, n_tiles):
    return pl.pallas_call(
        _final_norm_kernel,
        grid=(n_tiles,),
        in_specs=[pl.BlockSpec((TILE, D_MODEL), lambda i: (i + tile0, 0)), _const_spec((1, D_MODEL))],
        out_specs=pl.BlockSpec((TILE, D_MODEL), lambda i: (i, 0)),
        out_shape=jax.ShapeDtypeStruct((n_tiles * TILE, D_MODEL), F32),
        compiler_params=_cparams(("parallel",)),
        name="final_norm",
    )(x, g)


def _grid_pos_embed(n_tokens):
    rows = n_tokens // GRID_W
    quarter = D_MODEL // 4
    half = D_MODEL // 2
    omega = 1.0 / (10000.0 ** (jnp.arange(quarter, dtype=F32) / quarter))
    ang_r = jnp.arange(rows, dtype=F32)[:, None] * omega
    ang_c = jnp.arange(GRID_W, dtype=F32)[:, None] * omega
    emb_r = jnp.concatenate([jnp.sin(ang_r), jnp.cos(ang_r)], axis=-1)
    emb_c = jnp.concatenate([jnp.sin(ang_c), jnp.cos(ang_c)], axis=-1)
    emb = jnp.concatenate([jnp.broadcast_to(emb_r[:, None, :], (rows, GRID_W, half)),
                           jnp.broadcast_to(emb_c[None, :, :], (rows, GRID_W, half))], axis=-1)
    return emb.reshape(rows * GRID_W, D_MODEL)


def _pad_rows(w, rows, offset):
    out = jnp.zeros(w.shape[:-2] + (rows, w.shape[-1]), w.dtype)
    return lax.dynamic_update_slice_in_dim(out, w, offset, axis=-2)


def _hg_cols(w):
    return jnp.moveaxis(w.reshape(w.shape[:-1] + (N_HG, HG_LANES)), -2, 0)


def kernel(x_prompt, x_sample, state_rwkv, c, c_ctx, norm1_g, norm2_g, w_mod, b_mod, w_in, b_in, rwkv_w0, rwkv_w_up,
           rwkv_a0, rwkv_a_up, rwkv_g_up, rwkv_k_k, rwkv_k_a, rwkv_r_k, rwkv_gn_g, rwkv_gn_b, rwkv_w_o, conv_dw,
           conv_dw_b, conv_ln_g, conv_ln_b, conv_w_o, sgu_ln_g, sgu_ln_b, sgu_w_s, sgu_b_s, sgu_w_o, pool_w,
           pool_scale, pool_w_o, w_out, moe_w_gate, moe_w_up, moe_w_down, router_w, router_b, final_norm_g):
    n_ctx, t_ctx, d = x_prompt.shape
    n_lat, t_lat, _ = x_sample.shape
    n_layers = w_in.shape[0]
    assert d == D_MODEL and t_ctx % TILE == 0 and t_lat % TILE == 0 and n_lat < MOD_ROWS
    ctx_rows, lat_rows = n_ctx * t_ctx, n_lat * t_lat
    assert ctx_rows % t_lat == 0 and t_lat % PROJ_TILE == 0 and ROUTED_TILE == TILE
    ctx_tiles = ctx_rows // TILE
    seq_tiles_ctx, seq_tiles_lat = t_ctx // TILE, t_lat // TILE

    def mod_row_for(tile_rows):
        ctx_t = ctx_rows // tile_rows
        per_lat = t_lat // tile_rows
        return lambda i: jnp.where(i < ctx_t, n_lat, jnp.maximum(i - ctx_t, 0) // per_lat)

    mod_row = mod_row_for(TILE)
    mod_row_proj = mod_row_for(PROJ_TILE)

    cond = jnp.zeros((MOD_ROWS, d), F32).at[:n_lat].set(c).at[n_lat].set(c_ctx)
    mod = _modulation(cond, w_mod, b_mod).reshape(n_layers, MOD_ROWS, 1, 6 * d)

    x = _embed(x_prompt.reshape(ctx_rows, d), x_sample.reshape(lat_rows, d), _grid_pos_embed(t_lat), seq_tiles_lat)

    n_rw = RW_COLS
    n_mx = MX_COLS
    wa = w_in[:, :, :n_rw].astype(BF16)
    ba = b_in[:, None, :n_rw]
    wm = w_in[:, :, n_rw:n_rw + n_mx].astype(BF16)
    bm = b_in[:, None, n_rw:n_rw + n_mx]
    wgt = w_in[:, :, n_rw + n_mx:].astype(BF16)
    bgt = b_in[:, None, n_rw + n_mx:]

    par = jnp.zeros((n_layers, P_ROWS, RWKV_WIDTH), F32)
    par = par.at[:, P_W0:P_W0 + 2].set(rwkv_w0).at[:, P_A0:P_A0 + 2].set(rwkv_a0)
    par = par.at[:, P_KK].set(rwkv_k_k).at[:, P_KA].set(rwkv_k_a)
    par = par.at[:, P_RK].set(rwkv_r_k.reshape(n_layers, RWKV_WIDTH))
    par = par.at[:, P_GNG].set(rwkv_gn_g).at[:, P_GNB].set(rwkv_gn_b)
    par = jnp.moveaxis(_hg_cols(par), 0, 1)
    lr_k = RWKV_DECAY_RANK + RWKV_ICLR_RANK
    wup = jnp.moveaxis(_hg_cols(_pad_rows(rwkv_w_up, lr_k, 0)), 0, 2).astype(BF16)
    aup = jnp.moveaxis(_hg_cols(_pad_rows(rwkv_a_up, lr_k, RWKV_DECAY_RANK)), 0, 2).astype(BF16)
    gup = jnp.moveaxis(_hg_cols(rwkv_g_up), 0, 1).astype(BF16)
    rw_consts = _rwkv_consts()

    mix_vecs = jnp.zeros((n_layers, 8, CONV_WIDTH), F32)
    for row, vec in enumerate((conv_dw_b, conv_ln_g, conv_ln_b, sgu_ln_g, sgu_ln_b, pool_scale)):
        mix_vecs = mix_vecs.at[:, row].set(vec)
    sgu_wcat = jnp.transpose(sgu_w_s, (0, 2, 1, 3)).reshape(n_layers, SGU_CHUNK, SGU_GROUPS * SGU_CHUNK).astype(BF16)
    sgu_bs = jnp.repeat(jnp.swapaxes(sgu_b_s, 1, 2), SGU_WIDTH // SGU_GROUPS, axis=2)
    pc = POOL_WIDTH // POOL_GROUPS
    pool_bd = (pool_w[:, :, :, None, :] * jnp.eye(POOL_GROUPS, dtype=F32)[None, :, None, :, None])
    pool_bd = pool_bd.reshape(n_layers, POOL_WIDTH, POOL_WIDTH).astype(BF16)
    lane_grp = jnp.arange(SGU_WIDTH)[None, :] // (SGU_WIDTH // SGU_GROUPS)
    sgm = (jnp.arange(SGU_GROUPS * SGU_CHUNK)[:, None] // SGU_CHUNK == lane_grp).astype(BF16)
    wlen = TILE + 2 * HALO
    plm = (jnp.arange(POOL_GROUPS * wlen)[:, None] // wlen == jnp.arange(POOL_WIDTH)[None, :] // pc).astype(BF16)
    pool_band, pool_cnt = _pool_tables()

    wro, wco, wso, wpo, wout = (w.astype(BF16) for w in (rwkv_w_o, conv_w_o, sgu_w_o, pool_w_o, w_out))
    rw_t = jnp.pad(router_w, ((0, 0), (0, ROUTER_LANES - N_EXPERTS)))
    rb = router_b[:, None]

    ctx_s0 = jnp.zeros((n_ctx, N_HG, 2, RWKV_HEAD_DIM, HG_LANES), F32)
    ctx_states = []
    for l in range(n_layers):
        zr, zm = _in_projection(x, mod[l], mod_row_proj, norm1_g[l][None], wa[l], ba[l], wm[l], bm[l])
        rw_args = (par[l], wup[l], aup[l], gup[l], rw_consts)
        ya = jnp.zeros((x.shape[0], RWKV_WIDTH), BF16)
        ya, s_fin = _rwkv_mixer(zr, ctx_s0, n_ctx, t_ctx, 0, ya, *rw_args)
        ya, _ = _rwkv_mixer(zr, _states_to_kernel(state_rwkv[:, l]), n_lat, t_lat, ctx_rows // t_lat, ya, *rw_args)
        ctx_states.append(_states_from_kernel(s_fin))
        mx = _mixers(zm, ctx_tiles, seq_tiles_ctx, seq_tiles_lat, conv_dw[l], mix_vecs[l], sgu_wcat[l], sgu_bs[l],
                     pool_bd[l], sgm, plm, pool_band, pool_cnt)
        x1, h2, route_i, route_w = _merge(x, mod[l], mod_row, norm1_g[l][None], norm2_g[l][None], ya, mx, wgt[l], bgt[l],
                                 wro[l], wco[l], wso[l], wpo[l], wout[l], rw_t, rb)
        dest, src, w_sorted, tile_lo, tile_hi, tiles_used = _dispatch_plan(route_i, route_w)
        ff = _experts(h2, src, w_sorted, tile_lo, tile_hi, tiles_used, l, moe_w_gate, moe_w_up, moe_w_down)
        x = _combine(ff, dest, x1, mod[l], mod_row)

    g_fin = final_norm_g[None]
    y_prompt = _final_norm(x, g_fin, 0, ctx_tiles).reshape(n_ctx, t_ctx, d)
    y_sample = _final_norm(x, g_fin, ctx_tiles, lat_rows // TILE).reshape(n_lat, t_lat, d)
    new_state = jnp.stack(ctx_states, axis=1).astype(x_prompt.dtype)
    return (y_prompt, y_sample, new_state)
```

```python
import functools
import math

import jax
import jax.numpy as jnp
from jax import lax
from jax.experimental import pallas as pl
from jax.experimental.pallas import tpu as pltpu

F32 = jnp.float32
BF16 = jnp.bfloat16

D_MODEL = 1024
GRID_W = 64
RWKV_HEADS = 8
RWKV_HEAD_DIM = 64
RWKV_WIDTH = RWKV_HEADS * RWKV_HEAD_DIM
RWKV_DECAY_RANK = 64
RWKV_ICLR_RANK = 64
RWKV_GATE_RANK = 128
RWKV_GN_EPS = 64e-5
CONV_WIDTH = 256
CONV_KERNEL = 31
SGU_GROUPS = 4
SGU_WIDTH = 256
SGU_CHUNK = 128
POOL_GROUPS = 4
POOL_WIDTH = 256
POOL_WINDOWS = (2, 4, 8, 16)
N_BRANCHES = 4
N_EXPERTS = 16
TOP_K = 2
N_EXPERT_GROUPS = 4
EXPERTS_PER_GROUP = N_EXPERTS // N_EXPERT_GROUPS
D_EXPERT = 512
NORM_EPS = 1e-6
LN_EPS = 1e-5

RW_COLS = 3 * RWKV_WIDTH + RWKV_DECAY_RANK + RWKV_ICLR_RANK + RWKV_GATE_RANK
MX_COLS = 2 * CONV_WIDTH + 2 * SGU_WIDTH + POOL_WIDTH
MX_OUT = CONV_WIDTH + SGU_WIDTH + POOL_WIDTH

TILE = 256
PROJ_TILE = 512
HALO = 16
SUBLANES = 8
SHIFTED_ROWS = TILE + 2 * HALO - SUBLANES
HG_LANES = 256
HEADS_PER_HG = HG_LANES // RWKV_HEAD_DIM
N_HG = RWKV_WIDTH // HG_LANES
SCAN_CHUNK = 64
assert SCAN_CHUNK == RWKV_HEAD_DIM
CHUNKS_PER_TRIP = 2
DECAY_SCALE = math.exp(-0.5)
MOD_ROWS = 16
ROUTER_LANES = 128

VMEM_LIMIT = 48 * 1024 * 1024
RWKV_VMEM_LIMIT = 56 * 1024 * 1024


def _cparams(sem, vmem_limit=VMEM_LIMIT):
    return pltpu.CompilerParams(dimension_semantics=sem, vmem_limit_bytes=vmem_limit)


def _dot(a, b):
    return jnp.dot(a, b, preferred_element_type=F32)


def _split2(x):
    hi = x.astype(BF16)
    lo = (x - hi.astype(F32)).astype(BF16)
    return hi, lo


def _dot3(a, b):
    a_hi, a_lo = _split2(a)
    b_hi, b_lo = _split2(b)
    return _dot(a_hi, b_hi) + (_dot(a_lo, b_hi) + _dot(a_hi, b_lo))


def _dot_exact_rhs(x, rhs):
    hi, lo = _split2(x)
    return _dot(hi, rhs) + _dot(lo, rhs)


def _dot_exact_lhs(lhs, x):
    hi, lo = _split2(x)
    return _dot(lhs, hi) + _dot(lhs, lo)


def _sigmoid(x):
    return 1.0 / (1.0 + jnp.exp(-x))


def _silu(x):
    return x * _sigmoid(x)


def _norm_mod(x, g, shift, scale):
    y = x * lax.rsqrt(jnp.mean(x * x, axis=-1, keepdims=True) + NORM_EPS) * g
    return y * (1.0 + scale) + shift


def _layer_norm(x, g, b):
    mu = jnp.mean(x, axis=-1, keepdims=True)
    xc = x - mu
    var = jnp.mean(xc * xc, axis=-1, keepdims=True)
    return xc * lax.rsqrt(var + LN_EPS) * g + b


def _const_spec(shape):
    nd = len(shape)
    return pl.BlockSpec(shape, lambda *_: (0,) * nd)


def _mod_kernel(c_ref, w_ref, b_ref, o_ref):
    o_ref[...] = _dot3(_silu(c_ref[...]), w_ref[...]) + b_ref[...]


def _modulation(cond, w_mod, b_mod):
    n_layers = w_mod.shape[0]
    tn = 1536
    return pl.pallas_call(
        _mod_kernel,
        grid=(n_layers, 6 * D_MODEL // tn),
        in_specs=[
            pl.BlockSpec((MOD_ROWS, D_MODEL), lambda l, j: (0, 0)),
            pl.BlockSpec((None, D_MODEL, tn), lambda l, j: (l, 0, j)),
            pl.BlockSpec((None, 1, tn), lambda l, j: (l, 0, j)),
        ],
        out_specs=pl.BlockSpec((None, MOD_ROWS, tn), lambda l, j: (l, 0, j)),
        out_shape=jax.ShapeDtypeStruct((n_layers, MOD_ROWS, 6 * D_MODEL), F32),
        compiler_params=_cparams(("parallel", "parallel")),
        name="modulation",
    )(cond, w_mod, b_mod.reshape(n_layers, 1, 6 * D_MODEL))


def _embed_kernel(n_ctx_tiles, xp_ref, xs_ref, pos_ref, o_ref):
    i = pl.program_id(0)

    @pl.when(i < n_ctx_tiles)
    def _():
        o_ref[...] = xp_ref[...]

    @pl.when(i >= n_ctx_tiles)
    def _():
        o_ref[...] = xs_ref[...] + pos_ref[...]


def _embed(xp, xs, pos, lat_tiles):
    n_ctx_tiles = xp.shape[0] // TILE
    n_lat_tiles = xs.shape[0] // TILE
    n_tiles = n_ctx_tiles + n_lat_tiles
    return pl.pallas_call(
        functools.partial(_embed_kernel, n_ctx_tiles),
        grid=(n_tiles,),
        in_specs=[
            pl.BlockSpec((TILE, D_MODEL), lambda i: (jnp.minimum(i, n_ctx_tiles - 1), 0)),
            pl.BlockSpec((TILE, D_MODEL), lambda i: (jnp.maximum(i - n_ctx_tiles, 0), 0)),
            pl.BlockSpec((TILE, D_MODEL), lambda i: (jnp.maximum(i - n_ctx_tiles, 0) % lat_tiles, 0)),
        ],
        out_specs=pl.BlockSpec((TILE, D_MODEL), lambda i: (i, 0)),
        out_shape=jax.ShapeDtypeStruct((n_tiles * TILE, D_MODEL), F32),
        compiler_params=_cparams(("parallel",)),
        name="embed",
    )(xp, xs, pos)


def _inproj_kernel(x_ref, mod_ref, g_ref, wa_ref, ba_ref, wm_ref, bm_ref, zr_ref, zm_ref):
    h = _norm_mod(x_ref[...], g_ref[...], mod_ref[:, 0:D_MODEL], mod_ref[:, D_MODEL:2 * D_MODEL])
    hb = h.astype(BF16)
    zr_ref[...] = _dot(hb, wa_ref[...]) + ba_ref[...]
    zm_ref[...] = (_dot(hb, wm_ref[...]) + bm_ref[...]).astype(BF16)


def _in_projection(x, mod_l, mod_row, norm_g, wa, ba, wm, bm):
    n = x.shape[0]
    tm = PROJ_TILE
    return pl.pallas_call(
        _inproj_kernel,
        grid=(n // tm,),
        in_specs=[
            pl.BlockSpec((tm, D_MODEL), lambda i: (i, 0)),
            pl.BlockSpec((None, 1, 6 * D_MODEL), lambda i: (mod_row(i), 0, 0)),
            _const_spec((1, D_MODEL)),
            _const_spec((D_MODEL, RW_COLS)),
            _const_spec((1, RW_COLS)),
            _const_spec((D_MODEL, MX_COLS)),
            _const_spec((1, MX_COLS)),
        ],
        out_specs=[
            pl.BlockSpec((tm, RW_COLS), lambda i: (i, 0)),
            pl.BlockSpec((tm, MX_COLS), lambda i: (i, 0)),
        ],
        out_shape=[
            jax.ShapeDtypeStruct((n, RW_COLS), F32),
            jax.ShapeDtypeStruct((n, MX_COLS), BF16),
        ],
        compiler_params=_cparams(("parallel",)),
        name="in_projection",
    )(x, mod_l, norm_g, wa, ba, wm, bm)


P_W0, P_A0, P_KK, P_KA, P_RK, P_GNG, P_GNB = 0, 2, 4, 5, 6, 7, 8
P_ROWS = 16


def _rep(xb, reps, mask):
    return jnp.concatenate([xb] * reps, axis=0) * mask


def _rwkv_kernel(t_len, y_all_ref, r_ref, k_ref, v_ref, lr_ref, s0_ref, par_ref, wup_ref, aup_ref, gup_ref,
                 bd_ref, cm_ref, tri_ref, eyec_ref, out_ref, sfin_ref, y_scr, st_scr):
    del y_all_ref
    c = SCAN_CHUNK
    n_chunks = t_len // c
    n_trips = n_chunks // CHUNKS_PER_TRIP
    lr_k = RWKV_DECAY_RANK + RWKV_ICLR_RANK
    probs = [(d, sl, g) for sl in range(CHUNKS_PER_TRIP) for d in (0, 1) for g in range(N_HG)]
    n_sq = int(math.log2(c))

    def lanes(g):
        return slice(g * HG_LANES, (g + 1) * HG_LANES)

    def rep(xb):
        return _rep(xb, HEADS_PER_HG, bd_ref[...])

    def seg_sums(xs, passes):
        n = xs[0].shape[0]
        if passes == 1:
            out = _dot(jnp.concatenate([x.astype(BF16) for x in xs], axis=0), bd_ref[...])
            return [out[i * n:(i + 1) * n] for i in range(len(xs))]
        parts = []
        for x in xs:
            parts.extend(_split2(x))
        out = _dot(jnp.concatenate(parts, axis=0), bd_ref[...])
        return [out[2 * i * n:(2 * i + 1) * n] + out[(2 * i + 1) * n:(2 * i + 2) * n] for i in range(len(xs))]

    def fold(x_bd):
        n = x_bd.shape[0] // HEADS_PER_HG
        return (x_bd[0:n] + x_bd[n:2 * n]) + (x_bd[2 * n:3 * n] + x_bd[3 * n:4 * n])

    def par(g, row):
        return par_ref[g, row:row + 1, :]

    st_scr[...] = s0_ref[...]

    def trip(j, first_touch):
        rows = {}
        for sl in range(CHUNKS_PER_TRIP):
            rows[0, sl] = pl.ds(pl.multiple_of((CHUNKS_PER_TRIP * j + sl) * c, c), c)
            rows[1, sl] = pl.ds(pl.multiple_of((n_chunks - 1 - CHUNKS_PER_TRIP * j - sl) * c, c), c)
        lr = {key: lr_ref[rw, 0:lr_k] for key, rw in rows.items()}
        th = {key: jnp.tanh(x).astype(BF16) for key, x in lr.items()}
        lrb = {key: x.astype(BF16) for key, x in lr.items()}
        ss = [dict(r=r_ref[rows[d, sl], lanes(g)], k=k_ref[rows[d, sl], lanes(g)], v=v_ref[rows[d, sl], lanes(g)])
              for d, sl, g in probs]

        for (d, sl, g), s in zip(probs, ss):
            s["w_pre"] = par(g, P_W0 + d) + _dot(th[d, sl], wup_ref[d, g])
            s["a_pre"] = par(g, P_A0 + d) + _dot(lrb[d, sl], aup_ref[d, g])
            s["kkr"] = s["k"] * par(g, P_KK)
        for s, ssq in zip(ss, seg_sums([s["kkr"] * s["kkr"] for s in ss], 1)):
            s["ssq"] = ssq
        for (d, sl, g), s in zip(probs, ss):
            s["lw"] = -DECAY_SCALE * _sigmoid(s["w_pre"])
            s["cum"] = _dot_exact_lhs(tri_ref[d], s["lw"])
        for (d, sl, g), s in zip(probs, ss):
            a = _sigmoid(s["a_pre"])
            kk = s["kkr"] / jnp.maximum(jnp.sqrt(s["ssq"]), 1e-12)
            kd = s["k"] * (1.0 + (a - 1.0) * par(g, P_KA))
            cum = s["cum"]
            e_incl = jnp.exp(cum)
            e_neg = jnp.exp(-cum)
            gam = e_incl[c - 1:c, :] if d == 0 else e_incl[0:1, :]
            s["at"] = (-kk * jnp.exp(cum - s["lw"])).astype(BF16)
            s["rt"] = s["r"] * e_incl
            bt = kk * a * e_neg
            kt = kd * e_neg
            s["vb"] = s["v"].astype(BF16)
            s["bonus"] = s["r"] * kd * par(g, P_RK)
            s["gam_rows"] = _dot_exact_rhs(eyec_ref[...] * gam, bd_ref[...])
            tr = jnp.concatenate([bt, kt], axis=0).T
            swapped = pltpu.roll(tr, c, axis=1)
            first_half = lax.broadcasted_iota(jnp.int32, tr.shape, 1) < c
            bt_t = jnp.where(first_half, tr, swapped).astype(BF16)
            kt_t = jnp.where(first_half, swapped, tr).astype(BF16)
            s["bt_bd"] = jnp.concatenate([bt_t, bt_t], axis=1) * bd_ref[...]
            s["kt_bd"] = jnp.concatenate([kt_t, kt_t], axis=1) * bd_ref[...]
            lhs2 = jnp.concatenate([s["at"], s["rt"].astype(BF16)], axis=0)
            s["m_b"] = _dot(lhs2, s["bt_bd"])
            s["m_k"] = _dot(lhs2, s["kt_bd"])
        for s, bsum in zip(ss, seg_sums([s["bonus"] for s in ss], 1)):
            s["bsum"] = bsum
        for (d, sl, g), s in zip(probs, ss):
            causal = cm_ref[d]
            m_b = s["m_b"] * causal
            s["x"] = m_b[0:c]
            s["a_rb"] = m_b[c:2 * c].astype(BF16)
            s["t"] = eyec_ref[...] + s["x"]
            s["bt_c"] = fold(s["bt_bd"])
            av = _dot(jnp.concatenate([(s["m_k"] * causal).astype(BF16), fold(s["kt_bd"])], axis=0), rep(s["vb"]))
            s["g1"] = av[0:c].astype(BF16)
            s["y0"] = av[c:2 * c] + s["bsum"] * s["v"]
            s["z"] = av[2 * c:2 * c + RWKV_HEAD_DIM]

        for step in range(1, n_sq):
            for s in ss:
                x_bd = rep(s["x"].astype(BF16))
                if step == 1:
                    s["x"] = _dot(s["x"].astype(BF16), x_bd)
                else:
                    both = _dot(jnp.concatenate([s["x"], s["t"]], axis=0).astype(BF16), x_bd)
                    s["x"], s["t"] = both[0:c], s["t"] + both[c:2 * c]
        for s in ss:
            s["t"] = (s["t"] + _dot(s["t"].astype(BF16), rep(s["x"].astype(BF16)))).astype(BF16)

        for s in ss:
            s["w"] = _dot(s["t"], rep(s["at"])).astype(BF16)
            s["u0"] = _dot(s["t"], rep(s["g1"])).astype(BF16)
        for s in ss:
            lhs = jnp.concatenate([s["a_rb"], s["bt_c"]], axis=0)
            with_w = _dot(lhs, rep(s["w"]))
            with_u0 = _dot(lhs, rep(s["u0"]))
            s["q"] = (s["rt"] + with_w[0:c]).astype(BF16)
            s["p"] = (eyec_ref[...] + with_w[c:c + RWKV_HEAD_DIM]).astype(BF16)
            s["y0"] = s["y0"] + with_u0[0:c]
            s["z"] = s["z"] + with_u0[c:c + RWKV_HEAD_DIM]

        for (d, sl, g), s in zip(probs, ss):
            both = _dot(jnp.concatenate([s["p"], s["q"]], axis=0), rep(st_scr[g, d].astype(BF16)))
            st_scr[g, d] = s["gam_rows"] * (both[0:RWKV_HEAD_DIM] + s["z"])
            y = both[RWKV_HEAD_DIM:RWKV_HEAD_DIM + c] + s["y0"]
            if first_touch:
                y_scr[rows[d, sl], lanes(g)] = y
            else:
                y_scr[rows[d, sl], lanes(g)] += y

    half = n_trips // 2
    lax.fori_loop(0, half, lambda j, carry: (trip(j, True), carry)[1], 0)
    lax.fori_loop(half, n_trips, lambda j, carry: (trip(j, False), carry)[1], 0)
    sfin_ref[...] = st_scr[...]

    inv_n = 1.0 / RWKV_HEAD_DIM

    def finish(j, carry):
        rows = pl.ds(pl.multiple_of(j * TILE, TILE), TILE)
        sg = _sigmoid(lr_ref[rows, lr_k:HG_LANES]).astype(BF16)
        ys = [y_scr[rows, lanes(g)] for g in range(N_HG)]
        mus = [m * inv_n for m in seg_sums(ys, 2)]
        ycs = [y - mu for y, mu in zip(ys, mus)]
        variances = [v * inv_n for v in seg_sums([yc * yc for yc in ycs], 2)]
        for g in range(N_HG):
            yn = ycs[g] * lax.rsqrt(variances[g] + RWKV_GN_EPS) * par(g, P_GNG) + par(g, P_GNB)
            out_ref[rows, lanes(g)] = (yn * _dot(sg, gup_ref[g])).astype(BF16)
        return carry

    lax.fori_loop(0, t_len // TILE, finish, 0)


def _rwkv_mixer(zr, s0, n_seq, t_len, row_off, y_all, par, wup, aup, gup, consts):
    lr_block = 3 * RWKV_WIDTH // HG_LANES
    state_spec = pl.BlockSpec((None, N_HG, 2, RWKV_HEAD_DIM, HG_LANES), lambda b: (b, 0, 0, 0, 0))
    weights = (par, wup, aup, gup)
    return pl.pallas_call(
        functools.partial(_rwkv_kernel, t_len),
        grid=(n_seq,),
        in_specs=[
            pl.BlockSpec(memory_space=pl.ANY),
            pl.BlockSpec((t_len, RWKV_WIDTH), lambda b: (b + row_off, 0)),
            pl.BlockSpec((t_len, RWKV_WIDTH), lambda b: (b + row_off, 1)),
            pl.BlockSpec((t_len, RWKV_WIDTH), lambda b: (b + row_off, 2)),
            pl.BlockSpec((t_len, HG_LANES), lambda b: (b + row_off, lr_block)),
            state_spec,
            *[_const_spec(a.shape) for a in weights],
            *[_const_spec(a.shape) for a in consts],
        ],
        out_specs=[pl.BlockSpec((t_len, RWKV_WIDTH), lambda b: (b + row_off, 0)), state_spec],
        out_shape=[
            jax.ShapeDtypeStruct((zr.shape[0], RWKV_WIDTH), BF16),
            jax.ShapeDtypeStruct((n_seq, N_HG, 2, RWKV_HEAD_DIM, HG_LANES), F32),
        ],
        scratch_shapes=[
            pltpu.VMEM((t_len, RWKV_WIDTH), F32),
            pltpu.VMEM((N_HG, 2, RWKV_HEAD_DIM, HG_LANES), F32),
        ],
        input_output_aliases={0: 0},
        compiler_params=_cparams(("parallel",), RWKV_VMEM_LIMIT),
        name="rwkv_mixer",
    )(y_all, zr, zr, zr, zr, s0, *weights, *consts)


def _rwkv_consts():
    c = SCAN_CHUNK
    ri = jnp.arange(HG_LANES)[:, None]
    ci = jnp.arange(HG_LANES)[None, :]
    bd = (ri // RWKV_HEAD_DIM == ci // RWKV_HEAD_DIM).astype(BF16)
    t = jnp.arange(c)[:, None]
    i = (jnp.arange(HEADS_PER_HG * c) % c)[None, :]
    cm = jnp.stack([jnp.concatenate([i < t, i <= t]), jnp.concatenate([i > t, i >= t])]).astype(F32)
    ti = jnp.arange(c)[None, :]
    tri = jnp.stack([ti <= t, ti >= t]).astype(BF16)
    eyec = (i == t).astype(F32)
    return bd, cm, tri, eyec


def _states_to_kernel(s):
    b = s.shape[0]
    st = s.reshape(b, 2, N_HG, HEADS_PER_HG, RWKV_HEAD_DIM, RWKV_HEAD_DIM)
    return jnp.transpose(st, (0, 2, 1, 5, 3, 4)).reshape(b, N_HG, 2, RWKV_HEAD_DIM, HG_LANES)


def _states_from_kernel(sk):
    b = sk.shape[0]
    st = sk.reshape(b, N_HG, 2, RWKV_HEAD_DIM, HEADS_PER_HG, RWKV_HEAD_DIM)
    return jnp.transpose(st, (0, 2, 1, 4, 5, 3)).reshape(b, 2, RWKV_HEADS, RWKV_HEAD_DIM, RWKV_HEAD_DIM)


def _pool_tables():
    wlen = TILE + 2 * HALO
    t_loc = jnp.arange(TILE)[:, None]
    s_rel = jnp.arange(wlen)[None, :] - HALO
    lane_g = jnp.arange(POOL_WIDTH)[None, :] // (POOL_WIDTH // POOL_GROUPS)
    bands, counts = [], []
    for first in (0, 1):
        for last in (0, 1):
            lo_seq = 0 if first else -HALO
            hi_seq = TILE if last else TILE + HALO
            band_groups = []
            cnt = jnp.zeros((TILE, POOL_WIDTH), F32)
            for gi, win in enumerate(POOL_WINDOWS):
                lo = jnp.maximum(t_loc - win // 2, lo_seq)
                hi = jnp.minimum(t_loc + win - win // 2, hi_seq)
                band_groups.append((s_rel >= lo) & (s_rel < hi))
                cnt = jnp.where(lane_g == gi, (hi - lo).astype(F32), cnt)
            bands.append(jnp.concatenate(band_groups, axis=1))
            counts.append(cnt)
    return jnp.stack(bands).astype(BF16), jnp.stack(counts)


def _mixers_kernel(ctx_tiles, seq_tiles_ctx, seq_tiles_lat, zc_ref, zp_ref, zn_ref, dw_ref, vec_ref, wcat_ref,
                   bs_ref, pw_ref, sgm_ref, plm_ref, band_ref, cnt_ref, out_ref, win_scr, shift_scr):
    i = pl.program_id(0)
    j_ctx = i % seq_tiles_ctx
    j_lat = jnp.maximum(i - ctx_tiles, 0) % seq_tiles_lat
    is_ctx = i < ctx_tiles
    first = jnp.where(is_ctx, j_ctx == 0, j_lat == 0)
    last = jnp.where(is_ctx, j_ctx == seq_tiles_ctx - 1, j_lat == seq_tiles_lat - 1)
    keep_prev = jnp.where(first, 0.0, 1.0)
    keep_next = jnp.where(last, 0.0, 1.0)
    case = 2 * first.astype(jnp.int32) + last.astype(jnp.int32)

    cw = CONV_WIDTH
    conv_b, cln_g, cln_b = vec_ref[0:1, :], vec_ref[1:2, :], vec_ref[2:3, :]
    sln_g, sln_b, pool_scale = vec_ref[3:4, :], vec_ref[4:5, :], vec_ref[5:6, :]

    glu = lambda z_ref: z_ref[:, 0:cw].astype(F32) * _sigmoid(z_ref[:, cw:2 * cw].astype(F32))
    win_scr[0:HALO, :] = glu(zp_ref) * keep_prev
    win_scr[HALO:HALO + TILE, :] = glu(zc_ref)
    win_scr[HALO + TILE:2 * HALO + TILE, :] = glu(zn_ref) * keep_next
    for r in range(1, SUBLANES):
        shift_scr[r] = win_scr[pl.ds(r, SHIFTED_ROWS), :]
    acc = jnp.zeros((TILE, cw), F32) + conv_b
    pad = CONV_KERNEL // 2
    for j in range(CONV_KERNEL):
        q, r = divmod(HALO - pad + j, SUBLANES)
        rows = pl.ds(q * SUBLANES, TILE)
        tap = win_scr[rows, :] if r == 0 else shift_scr[r, rows, :]
        acc = acc + tap * dw_ref[j:j + 1, :]
    out_ref[:, 0:cw] = _silu(_layer_norm(acc, cln_g, cln_b)).astype(BF16)

    su = zc_ref[:, 2 * cw:2 * cw + SGU_WIDTH].astype(F32)
    sv = zc_ref[:, 2 * cw + SGU_WIDTH:2 * cw + 2 * SGU_WIDTH].astype(F32)
    vn = _layer_norm(sv, sln_g, sln_b).astype(BF16)
    sgm = sgm_ref[...]
    for ch in range(TILE // SGU_CHUNK):
        rows = slice(ch * SGU_CHUNK, (ch + 1) * SGU_CHUNK)
        s = _dot(wcat_ref[...], _rep(vn[rows], SGU_GROUPS, sgm)) + bs_ref[...]
        out_ref[rows, cw:cw + SGU_WIDTH] = (su[rows] * s).astype(BF16)

    zoff = 2 * cw + 2 * SGU_WIDTH
    zcur = zc_ref[:, zoff:zoff + POOL_WIDTH]
    zw = jnp.concatenate([zp_ref[:, zoff:zoff + POOL_WIDTH], zcur, zn_ref[:, zoff:zoff + POOL_WIDTH]], axis=0)
    psum = _dot(band_ref[case], _rep(zw, POOL_GROUPS, plm_ref[...]))
    p = psum / cnt_ref[case] - zcur.astype(F32)
    pooled = _dot(p.astype(BF16), pw_ref[...]) * pool_scale
    out_ref[:, cw + SGU_WIDTH:cw + SGU_WIDTH + POOL_WIDTH] = pooled.astype(BF16)


def _mixers(zm, ctx_tiles, seq_tiles_ctx, seq_tiles_lat, *tables):
    n = zm.shape[0]
    n_tiles = n // TILE
    per = TILE // HALO
    n_halo = n // HALO
    return pl.pallas_call(
        functools.partial(_mixers_kernel, ctx_tiles, seq_tiles_ctx, seq_tiles_lat),
        grid=(n_tiles,),
        in_specs=[
            pl.BlockSpec((TILE, MX_COLS), lambda i: (i, 0)),
            pl.BlockSpec((HALO, MX_COLS), lambda i: (jnp.maximum(i * per - 1, 0), 0)),
            pl.BlockSpec((HALO, MX_COLS), lambda i: (jnp.minimum((i + 1) * per, n_halo - 1), 0)),
            *[_const_spec(a.shape) for a in tables],
        ],
        out_specs=pl.BlockSpec((TILE, MX_OUT), lambda i: (i, 0)),
        out_shape=jax.ShapeDtypeStruct((n, MX_OUT), BF16),
        scratch_shapes=[
            pltpu.VMEM((TILE + 2 * HALO, CONV_WIDTH), F32),
            pltpu.VMEM((SUBLANES, SHIFTED_ROWS, CONV_WIDTH), F32),
        ],
        compiler_params=_cparams(("parallel",)),
        name="mixers",
    )(zm, zm, zm, *tables)


def _route(logits_t):
    m = jnp.max(logits_t, axis=0, keepdims=True)
    e = jnp.exp(logits_t - m)
    p = e / jnp.sum(e, axis=0, keepdims=True)
    best_score = None
    best = None
    for g in range(N_EXPERT_GROUPS):
        rows = [p[g * EXPERTS_PER_GROUP + q:g * EXPERTS_PER_GROUP + q + 1] for q in range(EXPERTS_PER_GROUP)]
        score = None
        for a in range(EXPERTS_PER_GROUP):
            for b in range(a + 1, EXPERTS_PER_GROUP):
                pair = rows[a] + rows[b]
                score = pair if score is None else jnp.maximum(score, pair)
        if g == 0:
            best_score, best = score, jnp.zeros(score.shape, jnp.int32)
        else:
            upd = score > best_score
            best = jnp.where(upd, g, best)
            best_score = jnp.where(upd, score, best_score)
    eidx = lax.broadcasted_iota(jnp.int32, p.shape, 0)
    neg = -jnp.inf
    masked = jnp.where(eidx // EXPERTS_PER_GROUP == best, p, neg)
    m1 = jnp.max(masked, axis=0, keepdims=True)
    i1 = jnp.min(jnp.where(masked == m1, eidx, N_EXPERTS), axis=0, keepdims=True)
    masked2 = jnp.where(eidx == i1, neg, masked)
    m2 = jnp.max(masked2, axis=0, keepdims=True)
    i2 = jnp.min(jnp.where(masked2 == m2, eidx, N_EXPERTS), axis=0, keepdims=True)
    tot = m1 + m2
    return jnp.concatenate([i1, i2], axis=0), jnp.concatenate([m1 / tot, m2 / tot], axis=0)


def _merge_kernel(x_ref, mod_ref, n1_ref, n2_ref, ya_ref, mx_ref, wg_ref, bg_ref, wro_ref, wco_ref, wso_ref,
                  wpo_ref, wout_ref, rw_ref, rb_ref, x1_ref, h2_ref, ri_ref, rwt_ref):
    d = D_MODEL
    x = x_ref[...]
    hb = _norm_mod(x, n1_ref[...], mod_ref[:, 0:d], mod_ref[:, d:2 * d]).astype(BF16)
    cw = CONV_WIDTH
    branches = (
        (ya_ref[...], wro_ref),
        (mx_ref[:, 0:cw], wco_ref),
        (mx_ref[:, cw:cw + SGU_WIDTH], wso_ref),
        (mx_ref[:, cw + SGU_WIDTH:MX_OUT], wpo_ref),
    )
    merged = None
    for bi, (y_in, w_ref) in enumerate(branches):
        zg = _dot(hb, wg_ref[:, bi * d:(bi + 1) * d]) + bg_ref[:, bi * d:(bi + 1) * d]
        term = _sigmoid(zg) * _dot(y_in, w_ref[...])
        merged = term if merged is None else merged + term
    x1 = x + mod_ref[:, 2 * d:3 * d] * _dot(merged.astype(BF16), wout_ref[...])
    x1_ref[...] = x1
    h2 = _norm_mod(x1, n2_ref[...], mod_ref[:, 3 * d:4 * d], mod_ref[:, 4 * d:5 * d])
    h2_ref[...] = h2
    logits = _dot3(h2, rw_ref[...])
    ri_ref[...], rwt_ref[...] = _route(logits.T[0:N_EXPERTS] + rb_ref[...])


def _merge(x, mod_l, mod_row, n1, n2, ya, mx, wg, bg, wro, wco, wso, wpo, wout, rw_t, rb):
    n = x.shape[0]
    tm = TILE
    consts = (n1, n2)
    weights = (wg, bg, wro, wco, wso, wpo, wout, rw_t, rb)
    return pl.pallas_call(
        _merge_kernel,
        grid=(n // tm,),
        in_specs=[
            pl.BlockSpec((tm, D_MODEL), lambda i: (i, 0)),
            pl.BlockSpec((None, 1, 6 * D_MODEL), lambda i: (mod_row(i), 0, 0)),
            *[_const_spec(a.shape) for a in consts],
            pl.BlockSpec((tm, RWKV_WIDTH), lambda i: (i, 0)),
            pl.BlockSpec((tm, MX_OUT), lambda i: (i, 0)),
            *[_const_spec(a.shape) for a in weights],
        ],
        out_specs=[
            pl.BlockSpec((tm, D_MODEL), lambda i: (i, 0)),
            pl.BlockSpec((tm, D_MODEL), lambda i: (i, 0)),
            pl.BlockSpec((TOP_K, tm), lambda i: (0, i)),
            pl.BlockSpec((TOP_K, tm), lambda i: (0, i)),
        ],
        out_shape=[
            jax.ShapeDtypeStruct((n, D_MODEL), F32),
            jax.ShapeDtypeStruct((n, D_MODEL), F32),
            jax.ShapeDtypeStruct((TOP_K, n), jnp.int32),
            jax.ShapeDtypeStruct((TOP_K, n), F32),
        ],
        compiler_params=_cparams(("parallel",)),
        name="merge_router",
    )(x, mod_l, n1, n2, ya, mx, *weights)


PAIRS_PER_GROUP = EXPERTS_PER_GROUP * (EXPERTS_PER_GROUP - 1) // 2
N_PAIR_CLASSES = N_EXPERT_GROUPS * PAIRS_PER_GROUP
ROUTED_TILE = 256


def _dispatch_plan(route_i, route_w):
    n = route_i.shape[1]
    tm = ROUTED_TILE
    n_tiles = n // tm + N_PAIR_CLASSES
    i1, i2 = route_i[0], route_i[1]
    first_lower = i1 < i2
    lo, hi = jnp.minimum(i1, i2), jnp.maximum(i1, i2)
    w_lo = jnp.where(first_lower, route_w[0], route_w[1])
    w_hi = jnp.where(first_lower, route_w[1], route_w[0])
    a, b = lo % EXPERTS_PER_GROUP, hi % EXPERTS_PER_GROUP
    assert EXPERTS_PER_GROUP == 4
    pair = a * (2 * EXPERTS_PER_GROUP - 1 - a) // 2 + (b - a - 1)
    pair = jnp.where(pair == 3, 4, jnp.where(pair == 4, 3, pair))
    cls = (lo // EXPERTS_PER_GROUP) * PAIRS_PER_GROUP + pair
    onehot = (cls[:, None] == jnp.arange(N_PAIR_CLASSES)[None, :]).astype(jnp.int32)
    counts = jnp.sum(onehot, axis=0)
    rank = jnp.sum((jnp.cumsum(onehot, axis=0) - onehot) * onehot, axis=1)
    padded = (counts + tm - 1) // tm * tm
    ends = jnp.cumsum(padded)
    dest = (ends - padded)[cls] + rank
    tile_row0 = jnp.arange(n_tiles, dtype=jnp.int32) * tm
    tile_valid = (tile_row0 < ends[-1]).astype(jnp.int32)
    tile_cls = jnp.minimum(jnp.searchsorted(ends, tile_row0, side="right"), N_PAIR_CLASSES - 1)
    stride = 1 << 15
    assert n + tm <= stride and N_PAIR_CLASSES * stride < (1 << 30)
    unused = jnp.iinfo(jnp.int32).max
    j = jnp.arange(tm, dtype=jnp.int32)[None, :]
    pad_keys = jnp.where(j < (padded - counts)[:, None], jnp.arange(N_PAIR_CLASSES)[:, None] * stride + n + j, unused)
    keys = jnp.concatenate([cls * stride + jnp.arange(n, dtype=jnp.int32), pad_keys.reshape(-1).astype(jnp.int32)])
    no_weight = jnp.zeros((N_PAIR_CLASSES * tm,), F32)
    keys, w_lo_sorted, w_hi_sorted = lax.sort(
        (keys, jnp.concatenate([w_lo, no_weight]), jnp.concatenate([w_hi, no_weight])), num_keys=1)
    token = keys % stride
    src = jnp.where((token < n) & (keys != unused), token, 0)
    w_sorted = jnp.stack([w_lo_sorted, w_hi_sorted], axis=1)
    cls_ids = jnp.arange(N_PAIR_CLASSES)
    pa = jnp.array([0, 0, 0, 1, 1, 2], jnp.int32)
    pb = jnp.array([1, 2, 3, 3, 2, 3], jnp.int32)
    cls_lo = (cls_ids // PAIRS_PER_GROUP) * EXPERTS_PER_GROUP + pa[cls_ids % PAIRS_PER_GROUP]
    cls_hi = (cls_ids // PAIRS_PER_GROUP) * EXPERTS_PER_GROUP + pb[cls_ids % PAIRS_PER_GROUP]
    last_cls = tile_cls[jnp.maximum(ends[-1] // tm - 1, 0)]
    tile_cls = jnp.where(tile_valid == 1, tile_cls, last_cls)
    tiles_used = (ends[-1:] // tm).astype(jnp.int32)
    return (dest.astype(jnp.int32), src, w_sorted, cls_lo[tile_cls].astype(jnp.int32),
            cls_hi[tile_cls].astype(jnp.int32), tiles_used)


def _start_row_gather(idx_ref, idx0, rows_hbm, buf, sem, n_rows):
    for r in range(n_rows):
        pltpu.make_async_copy(rows_hbm.at[pl.ds(idx_ref[idx0 + r], 1)], buf.at[pl.ds(r, 1)], sem).start()


def _wait_row_gather(rows_hbm, buf, sem, n_rows):
    pltpu.make_async_copy(rows_hbm.at[pl.ds(0, n_rows)], buf, sem).wait()


def _pipelined_gather(idx_ref, rows_hbm, bufs, sems, n_rows):
    i = pl.program_id(0)
    slot = i % 2

    @pl.when(i == 0)
    def _():
        _start_row_gather(idx_ref, 0, rows_hbm, bufs.at[0], sems.at[0], n_rows)

    @pl.when(i + 1 < pl.num_programs(0))
    def _():
        _start_row_gather(idx_ref, (i + 1) * n_rows, rows_hbm, bufs.at[1 - slot], sems.at[1 - slot], n_rows)

    _wait_row_gather(rows_hbm, bufs.at[slot], sems.at[slot], n_rows)
    return slot


EXPERT_GATHER_DEPTH = 2


def _experts_kernel(lo_ref, hi_ref, used_ref, src_ref, w_ref, h_hbm, wg_lo, wu_lo, wd_lo, wg_hi, wu_hi, wd_hi,
                    o_ref, hbuf, sems, up_scr, down_scr):
    tm = ROUTED_TILE
    n_slots = EXPERT_GATHER_DEPTH + 1
    i = pl.program_id(0)
    n = used_ref[0]
    slot = i % n_slots

    @pl.when(i == 0)
    def _():
        for t in range(EXPERT_GATHER_DEPTH):
            @pl.when(t < n)
            def _():
                _start_row_gather(src_ref, t * tm, h_hbm, hbuf.at[t], sems.at[t], tm)

    def tile(prefetch):
        _wait_row_gather(h_hbm, hbuf.at[slot], sems.at[slot], tm)
        ahead = (i + EXPERT_GATHER_DEPTH) % n_slots
        rows = iter(range(tm))

        def request(count):
            if not prefetch:
                return
            for r in (next(rows) for _ in range(count)):
                row = src_ref[(i + EXPERT_GATHER_DEPTH) * tm + r]
                pltpu.make_async_copy(h_hbm.at[pl.ds(row, 1)], hbuf.at[ahead, pl.ds(r, 1)], sems.at[ahead]).start()

        hb = hbuf[slot].astype(BF16)
        per_dot = tm // 6

        def expert(which, expert_of_tile, wg, wu, wd):
            changed = (i == 0) | (expert_of_tile[i] != expert_of_tile[jnp.maximum(i - 1, 0)])

            @pl.when(changed)
            def _():
                up_scr[2 * which] = wg[...].astype(BF16)
                up_scr[2 * which + 1] = wu[...].astype(BF16)
                down_scr[which] = wd[...].astype(BF16)

            request(per_dot)
            gate = _dot(hb, up_scr[2 * which])
            request(per_dot)
            up = _dot(hb, up_scr[2 * which + 1])
            request(per_dot)
            return _dot((_silu(gate) * up).astype(BF16), down_scr[which])

        w = w_ref[...]
        out = w[:, 0:1] * expert(0, lo_ref, wg_lo, wu_lo, wd_lo) + w[:, 1:2] * expert(1, hi_ref, wg_hi, wu_hi, wd_hi)
        request(tm - 6 * per_dot)
        o_ref[...] = out

    @pl.when(i + EXPERT_GATHER_DEPTH < n)
    def _():
        tile(True)

    @pl.when((i < n) & (i + EXPERT_GATHER_DEPTH >= n))
    def _():
        tile(False)

    @pl.when(i >= n)
    def _():
        o_ref[...] = jnp.zeros(o_ref.shape, F32)


def _experts(h2, src, w_sorted, tile_lo, tile_hi, tiles_used, layer, wg, wu, wd):
    tm = ROUTED_TILE
    n_tiles = tile_lo.shape[0]
    n_slots = EXPERT_GATHER_DEPTH + 1
    up_spec = lambda sel: pl.BlockSpec((None, None, D_MODEL, D_EXPERT), lambda i, lo, hi, u, s: (layer, sel(lo, hi)[i], 0, 0))
    down_spec = lambda sel: pl.BlockSpec((None, None, D_EXPERT, D_MODEL), lambda i, lo, hi, u, s: (layer, sel(lo, hi)[i], 0, 0))
    pick_lo = lambda lo, hi: lo
    pick_hi = lambda lo, hi: hi
    return pl.pallas_call(
        _experts_kernel,
        grid_spec=pltpu.PrefetchScalarGridSpec(
            num_scalar_prefetch=4,
            grid=(n_tiles,),
            in_specs=[
                pl.BlockSpec((tm, TOP_K), lambda i, *_: (i, 0)),
                pl.BlockSpec(memory_space=pl.ANY),
                up_spec(pick_lo), up_spec(pick_lo), down_spec(pick_lo),
                up_spec(pick_hi), up_spec(pick_hi), down_spec(pick_hi),
            ],
            out_specs=pl.BlockSpec((tm, D_MODEL), lambda i, *_: (i, 0)),
            scratch_shapes=[
                pltpu.VMEM((n_slots, tm, D_MODEL), F32),
                pltpu.SemaphoreType.DMA((n_slots,)),
                pltpu.VMEM((2 * TOP_K, D_MODEL, D_EXPERT), BF16),
                pltpu.VMEM((TOP_K, D_EXPERT, D_MODEL), BF16),
            ],
        ),
        out_shape=jax.ShapeDtypeStruct((n_tiles * tm, D_MODEL), F32),
        compiler_params=_cparams(("arbitrary",)),
        name="experts",
    )(tile_lo, tile_hi, tiles_used, src, w_sorted, h2, wg, wu, wd, wg, wu, wd)


def _combine_kernel(dest_ref, x_ref, mod_ref, ff_hbm, o_ref, fbuf, sems):
    slot = _pipelined_gather(dest_ref, ff_hbm, fbuf, sems, ROUTED_TILE)
    o_ref[...] = x_ref[...] + mod_ref[:, 5 * D_MODEL:6 * D_MODEL] * fbuf[slot]


def _combine(ff_sorted, dest, x1, mod_l, mod_row):
    n = x1.shape[0]
    tm = ROUTED_TILE
    return pl.pallas_call(
        _combine_kernel,
        grid_spec=pltpu.PrefetchScalarGridSpec(
            num_scalar_prefetch=1,
            grid=(n // tm,),
            in_specs=[
                pl.BlockSpec((tm, D_MODEL), lambda i, d: (i, 0)),
                pl.BlockSpec((None, 1, 6 * D_MODEL), lambda i, d: (mod_row(i), 0, 0)),
                pl.BlockSpec(memory_space=pl.ANY),
            ],
            out_specs=pl.BlockSpec((tm, D_MODEL), lambda i, d: (i, 0)),
            scratch_shapes=[pltpu.VMEM((2, tm, D_MODEL), F32), pltpu.SemaphoreType.DMA((2,))],
        ),
        out_shape=jax.ShapeDtypeStruct((n, D_MODEL), F32),
        compiler_params=_cparams(("arbitrary",)),
        name="moe_combine",
    )(dest, x1, mod_l, ff_sorted)


def _final_norm_kernel(x_ref, g_ref, o_ref):
    x = x_ref[...]
    o_ref[...] = x * lax.rsqrt(jnp.mean(x * x, axis=-1, keepdims=True) + NORM_EPS) * g_ref[...]


def _final_norm(x, g, tile0, n_tiles):
    return pl.pallas_call(
        _final_norm_kernel,
        grid=(n_tiles,),
        in_specs=[pl.BlockSpec((TILE, D_MODEL), lambda i: (i + tile0, 0)), _const_spec((1, D_MODEL))],
        out_specs=pl.BlockSpec((TILE, D_MODEL), lambda i: (i, 0)),
        out_shape=jax.ShapeDtypeStruct((n_tiles * TILE, D_MODEL), F32),
        compiler_params=_cparams(("parallel",)),
        name="final_norm",
    )(x, g)


def _grid_pos_embed(n_tokens):
    rows = n_tokens // GRID_W
    quarter = D_MODEL // 4
    half = D_MODEL // 2
    omega = 1.0 / (10000.0 ** (jnp.arange(quarter, dtype=F32) / quarter))
    ang_r = jnp.arange(rows, dtype=F32)[:, None] * omega
    ang_c = jnp.arange(GRID_W, dtype=F32)[:, None] * omega
    emb_r = jnp.concatenate([jnp.sin(ang_r), jnp.cos(ang_r)], axis=-1)
    emb_c = jnp.concatenate([jnp.sin(ang_c), jnp.cos(ang_c)], axis=-1)
    emb = jnp.concatenate([jnp.broadcast_to(emb_r[:, None, :], (rows, GRID_W, half)),
                           jnp.broadcast_to(emb_c[None, :, :], (rows, GRID_W, half))], axis=-1)
    return emb.reshape(rows * GRID_W, D_MODEL)


def _pad_rows(w, rows, offset):
    out = jnp.zeros(w.shape[:-2] + (rows, w.shape[-1]), w.dtype)
    return lax.dynamic_update_slice_in_dim(out, w, offset, axis=-2)


def _hg_cols(w):
    return jnp.moveaxis(w.reshape(w.shape[:-1] + (N_HG, HG_LANES)), -2, 0)


def kernel(x_prompt, x_sample, state_rwkv, c, c_ctx, norm1_g, norm2_g, w_mod, b_mod, w_in, b_in, rwkv_w0, rwkv_w_up,
           rwkv_a0, rwkv_a_up, rwkv_g_up, rwkv_k_k, rwkv_k_a, rwkv_r_k, rwkv_gn_g, rwkv_gn_b, rwkv_w_o, conv_dw,
           conv_dw_b, conv_ln_g, conv_ln_b, conv_w_o, sgu_ln_g, sgu_ln_b, sgu_w_s, sgu_b_s, sgu_w_o, pool_w,
           pool_scale, pool_w_o, w_out, moe_w_gate, moe_w_up, moe_w_down, router_w, router_b, final_norm_g):
    n_ctx, t_ctx, d = x_prompt.shape
    n_lat, t_lat, _ = x_sample.shape
    n_layers = w_in.shape[0]
    assert d == D_MODEL and t_ctx % TILE == 0 and t_lat % TILE == 0 and n_lat < MOD_ROWS
    ctx_rows, lat_rows = n_ctx * t_ctx, n_lat * t_lat
    assert ctx_rows % t_lat == 0 and t_lat % PROJ_TILE == 0 and ROUTED_TILE == TILE
    ctx_tiles = ctx_rows // TILE
    seq_tiles_ctx, seq_tiles_lat = t_ctx // TILE, t_lat // TILE

    def mod_row_for(tile_rows):
        ctx_t = ctx_rows // tile_rows
        per_lat = t_lat // tile_rows
        return lambda i: jnp.where(i < ctx_t, n_lat, jnp.maximum(i - ctx_t, 0) // per_lat)

    mod_row = mod_row_for(TILE)
    mod_row_proj = mod_row_for(PROJ_TILE)

    cond = jnp.zeros((MOD_ROWS, d), F32).at[:n_lat].set(c).at[n_lat].set(c_ctx)
    mod = _modulation(cond, w_mod, b_mod).reshape(n_layers, MOD_ROWS, 1, 6 * d)

    x = _embed(x_prompt.reshape(ctx_rows, d), x_sample.reshape(lat_rows, d), _grid_pos_embed(t_lat), seq_tiles_lat)

    n_rw = RW_COLS
    n_mx = MX_COLS
    wa = w_in[:, :, :n_rw].astype(BF16)
    ba = b_in[:, None, :n_rw]
    wm = w_in[:, :, n_rw:n_rw + n_mx].astype(BF16)
    bm = b_in[:, None, n_rw:n_rw + n_mx]
    wgt = w_in[:, :, n_rw + n_mx:].astype(BF16)
    bgt = b_in[:, None, n_rw + n_mx:]

    par = jnp.zeros((n_layers, P_ROWS, RWKV_WIDTH), F32)
    par = par.at[:, P_W0:P_W0 + 2].set(rwkv_w0).at[:, P_A0:P_A0 + 2].set(rwkv_a0)
    par = par.at[:, P_KK].set(rwkv_k_k).at[:, P_KA].set(rwkv_k_a)
    par = par.at[:, P_RK].set(rwkv_r_k.reshape(n_layers, RWKV_WIDTH))
    par = par.at[:, P_GNG].set(rwkv_gn_g).at[:, P_GNB].set(rwkv_gn_b)
    par = jnp.moveaxis(_hg_cols(par), 0, 1)
    lr_k = RWKV_DECAY_RANK + RWKV_ICLR_RANK
    wup = jnp.moveaxis(_hg_cols(_pad_rows(rwkv_w_up, lr_k, 0)), 0, 2).astype(BF16)
    aup = jnp.moveaxis(_hg_cols(_pad_rows(rwkv_a_up, lr_k, RWKV_DECAY_RANK)), 0, 2).astype(BF16)
    gup = jnp.moveaxis(_hg_cols(rwkv_g_up), 0, 1).astype(BF16)
    rw_consts = _rwkv_consts()

    mix_vecs = jnp.zeros((n_layers, 8, CONV_WIDTH), F32)
    for row, vec in enumerate((conv_dw_b, conv_ln_g, conv_ln_b, sgu_ln_g, sgu_ln_b, pool_scale)):
        mix_vecs = mix_vecs.at[:, row].set(vec)
    sgu_wcat = jnp.transpose(sgu_w_s, (0, 2, 1, 3)).reshape(n_layers, SGU_CHUNK, SGU_GROUPS * SGU_CHUNK).astype(BF16)
    sgu_bs = jnp.repeat(jnp.swapaxes(sgu_b_s, 1, 2), SGU_WIDTH // SGU_GROUPS, axis=2)
    pc = POOL_WIDTH // POOL_GROUPS
    pool_bd = (pool_w[:, :, :, None, :] * jnp.eye(POOL_GROUPS, dtype=F32)[None, :, None, :, None])
    pool_bd = pool_bd.reshape(n_layers, POOL_WIDTH, POOL_WIDTH).astype(BF16)
    lane_grp = jnp.arange(SGU_WIDTH)[None, :] // (SGU_WIDTH // SGU_GROUPS)
    sgm = (jnp.arange(SGU_GROUPS * SGU_CHUNK)[:, None] // SGU_CHUNK == lane_grp).astype(BF16)
    wlen = TILE + 2 * HALO
    plm = (jnp.arange(POOL_GROUPS * wlen)[:, None] // wlen == jnp.arange(POOL_WIDTH)[None, :] // pc).astype(BF16)
    pool_band, pool_cnt = _pool_tables()

    wro, wco, wso, wpo, wout = (w.astype(BF16) for w in (rwkv_w_o, conv_w_o, sgu_w_o, pool_w_o, w_out))
    rw_t = jnp.pad(router_w, ((0, 0), (0, ROUTER_LANES - N_EXPERTS)))
    rb = router_b[:, None]

    ctx_s0 = jnp.zeros((n_ctx, N_HG, 2, RWKV_HEAD_DIM, HG_LANES), F32)
    ctx_states = []
    for l in range(n_layers):
        zr, zm = _in_projection(x, mod[l], mod_row_proj, norm1_g[l][None], wa[l], ba[l], wm[l], bm[l])
        rw_args = (par[l], wup[l], aup[l], gup[l], rw_consts)
        ya = jnp.zeros((x.shape[0], RWKV_WIDTH), BF16)
        ya, s_fin = _rwkv_mixer(zr, ctx_s0, n_ctx, t_ctx, 0, ya, *rw_args)
        ya, _ = _rwkv_mixer(zr, _states_to_kernel(state_rwkv[:, l]), n_lat, t_lat, ctx_rows // t_lat, ya, *rw_args)
        ctx_states.append(_states_from_kernel(s_fin))
        mx = _mixers(zm, ctx_tiles, seq_tiles_ctx, seq_tiles_lat, conv_dw[l], mix_vecs[l], sgu_wcat[l], sgu_bs[l],
                     pool_bd[l], sgm, plm, pool_band, pool_cnt)
        x1, h2, route_i, route_w = _merge(x, mod[l], mod_row, norm1_g[l][None], norm2_g[l][None], ya, mx, wgt[l], bgt[l],
                                 wro[l], wco[l], wso[l], wpo[l], wout[l], rw_t, rb)
        dest, src, w_sorted, tile_lo, tile_hi, tiles_used = _dispatch_plan(route_i, route_w)
        ff = _experts(h2, src, w_sorted, tile_lo, tile_hi, tiles_used, l, moe_w_gate, moe_w_up, moe_w_down)
        x = _combine(ff, dest, x1, mod[l], mod_row)

    g_fin = final_norm_g[None]
    y_prompt = _final_norm(x, g_fin, 0, ctx_tiles).reshape(n_ctx, t_ctx, d)
    y_sample = _final_norm(x, g_fin, ctx_tiles, lat_rows // TILE).reshape(n_lat, t_lat, d)
    new_state = jnp.stack(ctx_states, axis=1).astype(x_prompt.dtype)
    return (y_prompt, y_sample, new_state)
```

```python
import functools
import math

import jax
import jax.numpy as jnp
from jax import lax
from jax.experimental import pallas as pl
from jax.experimental.pallas import tpu as pltpu

F32 = jnp.float32
BF16 = jnp.bfloat16

D_MODEL = 1024
GRID_W = 64
RWKV_HEADS = 8
RWKV_HEAD_DIM = 64
RWKV_WIDTH = RWKV_HEADS * RWKV_HEAD_DIM
RWKV_DECAY_RANK = 64
RWKV_ICLR_RANK = 64
RWKV_GATE_RANK = 128
RWKV_GN_EPS = 64e-5
CONV_WIDTH = 256
CONV_KERNEL = 31
SGU_GROUPS = 4
SGU_WIDTH = 256
SGU_CHUNK = 128
POOL_GROUPS = 4
POOL_WIDTH = 256
POOL_WINDOWS = (2, 4, 8, 16)
N_BRANCHES = 4
N_EXPERTS = 16
TOP_K = 2
N_EXPERT_GROUPS = 4
EXPERTS_PER_GROUP = N_EXPERTS // N_EXPERT_GROUPS
D_EXPERT = 512
NORM_EPS = 1e-6
LN_EPS = 1e-5

RW_COLS = 3 * RWKV_WIDTH + RWKV_DECAY_RANK + RWKV_ICLR_RANK + RWKV_GATE_RANK
MX_COLS = 2 * CONV_WIDTH + 2 * SGU_WIDTH + POOL_WIDTH
MX_OUT = CONV_WIDTH + SGU_WIDTH + POOL_WIDTH

TILE = 256
PROJ_TILE = 512
IO_TILE = 1024
HALO = 16
SUBLANES = 8
SHIFTED_ROWS = TILE + 2 * HALO - SUBLANES
HG_LANES = 256
HEADS_PER_HG = HG_LANES // RWKV_HEAD_DIM
N_HG = RWKV_WIDTH // HG_LANES
SCAN_CHUNK = 64
assert SCAN_CHUNK == RWKV_HEAD_DIM
CHUNKS_PER_TRIP = 2
DECAY_SCALE = math.exp(-0.5)
MOD_ROWS = 16
ROUTER_LANES = 128

VMEM_LIMIT = 48 * 1024 * 1024
RWKV_VMEM_LIMIT = 56 * 1024 * 1024


def _cparams(sem, vmem_limit=VMEM_LIMIT):
    return pltpu.CompilerParams(dimension_semantics=sem, vmem_limit_bytes=vmem_limit)


def _dot(a, b):
    return jnp.dot(a, b, preferred_element_type=F32)


def _split2(x):
    hi = x.astype(BF16)
    lo = (x - hi.astype(F32)).astype(BF16)
    return hi, lo


def _dot3(a, b):
    a_hi, a_lo = _split2(a)
    b_hi, b_lo = _split2(b)
    return _dot(a_hi, b_hi) + (_dot(a_lo, b_hi) + _dot(a_hi, b_lo))


def _dot_exact_rhs(x, rhs):
    hi, lo = _split2(x)
    return _dot(hi, rhs) + _dot(lo, rhs)


def _dot_exact_lhs(lhs, x):
    hi, lo = _split2(x)
    return _dot(lhs, hi) + _dot(lhs, lo)


def _sigmoid(x):
    return 1.0 / (1.0 + jnp.exp(-x))


def _silu(x):
    return x * _sigmoid(x)


def _norm_mod(x, g, shift, scale):
    y = x * lax.rsqrt(jnp.mean(x * x, axis=-1, keepdims=True) + NORM_EPS) * g
    return y * (1.0 + scale) + shift


def _layer_norm(x, g, b):
    mu = jnp.mean(x, axis=-1, keepdims=True)
    xc = x - mu
    var = jnp.mean(xc * xc, axis=-1, keepdims=True)
    return xc * lax.rsqrt(var + LN_EPS) * g + b


def _const_spec(shape):
    nd = len(shape)
    return pl.BlockSpec(shape, lambda *_: (0,) * nd)


def _mod_kernel(c_ref, w_ref, b_ref, o_ref):
    o_ref[...] = _dot3(_silu(c_ref[...]), w_ref[...]) + b_ref[...]


def _modulation(cond, w_mod, b_mod):
    n_layers = w_mod.shape[0]
    tn = 1536
    return pl.pallas_call(
        _mod_kernel,
        grid=(n_layers, 6 * D_MODEL // tn),
        in_specs=[
            pl.BlockSpec((MOD_ROWS, D_MODEL), lambda l, j: (0, 0)),
            pl.BlockSpec((None, D_MODEL, tn), lambda l, j: (l, 0, j)),
            pl.BlockSpec((None, 1, tn), lambda l, j: (l, 0, j)),
        ],
        out_specs=pl.BlockSpec((None, MOD_ROWS, tn), lambda l, j: (l, 0, j)),
        out_shape=jax.ShapeDtypeStruct((n_layers, MOD_ROWS, 6 * D_MODEL), F32),
        compiler_params=_cparams(("parallel", "parallel")),
        name="modulation",
    )(cond, w_mod, b_mod.reshape(n_layers, 1, 6 * D_MODEL))


def _embed_kernel(n_ctx_tiles, xp_ref, xs_ref, pos_ref, o_ref):
    i = pl.program_id(0)

    @pl.when(i < n_ctx_tiles)
    def _():
        o_ref[...] = xp_ref[...]

    @pl.when(i >= n_ctx_tiles)
    def _():
        o_ref[...] = xs_ref[...] + pos_ref[...]


def _embed(xp, xs, pos, lat_tiles):
    tm = IO_TILE
    n_ctx_tiles = xp.shape[0] // tm
    n_lat_tiles = xs.shape[0] // tm
    n_tiles = n_ctx_tiles + n_lat_tiles
    return pl.pallas_call(
        functools.partial(_embed_kernel, n_ctx_tiles),
        grid=(n_tiles,),
        in_specs=[
            pl.BlockSpec((tm, D_MODEL), lambda i: (jnp.minimum(i, n_ctx_tiles - 1), 0)),
            pl.BlockSpec((tm, D_MODEL), lambda i: (jnp.maximum(i - n_ctx_tiles, 0), 0)),
            pl.BlockSpec((tm, D_MODEL), lambda i: (jnp.maximum(i - n_ctx_tiles, 0) % lat_tiles, 0)),
        ],
        out_specs=pl.BlockSpec((tm, D_MODEL), lambda i: (i, 0)),
        out_shape=jax.ShapeDtypeStruct((n_tiles * tm, D_MODEL), F32),
        compiler_params=_cparams(("parallel",)),
        name="embed",
    )(xp, xs, pos)


def _inproj_kernel(x_ref, mod_ref, g_ref, wa_ref, ba_ref, wm_ref, bm_ref, zr_ref, zm_ref):
    h = _norm_mod(x_ref[...], g_ref[...], mod_ref[:, 0:D_MODEL], mod_ref[:, D_MODEL:2 * D_MODEL])
    hb = h.astype(BF16)
    zr_ref[...] = _dot(hb, wa_ref[...]) + ba_ref[...]
    zm_ref[...] = (_dot(hb, wm_ref[...]) + bm_ref[...]).astype(BF16)


def _in_projection(x, mod_l, mod_row, norm_g, wa, ba, wm, bm):
    n = x.shape[0]
    tm = PROJ_TILE
    return pl.pallas_call(
        _inproj_kernel,
        grid=(n // tm,),
        in_specs=[
            pl.BlockSpec((tm, D_MODEL), lambda i: (i, 0)),
            pl.BlockSpec((None, 1, 6 * D_MODEL), lambda i: (mod_row(i), 0, 0)),
            _const_spec((1, D_MODEL)),
            _const_spec((D_MODEL, RW_COLS)),
            _const_spec((1, RW_COLS)),
            _const_spec((D_MODEL, MX_COLS)),
            _const_spec((1, MX_COLS)),
        ],
        out_specs=[
            pl.BlockSpec((tm, RW_COLS), lambda i: (i, 0)),
            pl.BlockSpec((tm, MX_COLS), lambda i: (i, 0)),
        ],
        out_shape=[
            jax.ShapeDtypeStruct((n, RW_COLS), F32),
            jax.ShapeDtypeStruct((n, MX_COLS), BF16),
        ],
        compiler_params=_cparams(("parallel",)),
        name="in_projection",
    )(x, mod_l, norm_g, wa, ba, wm, bm)


P_W0, P_A0, P_KK, P_KA, P_RK, P_GNG, P_GNB = 0, 2, 4, 5, 6, 7, 8
P_ROWS = 16


def _rep(xb, reps, mask):
    return jnp.concatenate([xb] * reps, axis=0) * mask


def _rwkv_kernel(t_len, y_all_ref, r_ref, k_ref, v_ref, lr_ref, s0_ref, par_ref, wup_ref, aup_ref, gup_ref,
                 bd_ref, cm_ref, tri_ref, eyec_ref, out_ref, sfin_ref, y_scr, st_scr):
    del y_all_ref
    c = SCAN_CHUNK
    n_chunks = t_len // c
    n_trips = n_chunks // CHUNKS_PER_TRIP
    lr_k = RWKV_DECAY_RANK + RWKV_ICLR_RANK
    probs = [(d, sl, g) for sl in range(CHUNKS_PER_TRIP) for d in (0, 1) for g in range(N_HG)]
    n_sq = int(math.log2(c))

    def lanes(g):
        return slice(g * HG_LANES, (g + 1) * HG_LANES)

    def rep(xb):
        return _rep(xb, HEADS_PER_HG, bd_ref[...])

    def seg_sums(xs, passes):
        n = xs[0].shape[0]
        if passes == 1:
            out = _dot(jnp.concatenate([x.astype(BF16) for x in xs], axis=0), bd_ref[...])
            return [out[i * n:(i + 1) * n] for i in range(len(xs))]
        parts = []
        for x in xs:
            parts.extend(_split2(x))
        out = _dot(jnp.concatenate(parts, axis=0), bd_ref[...])
        return [out[2 * i * n:(2 * i + 1) * n] + out[(2 * i + 1) * n:(2 * i + 2) * n] for i in range(len(xs))]

    def fold(x_bd):
        n = x_bd.shape[0] // HEADS_PER_HG
        return (x_bd[0:n] + x_bd[n:2 * n]) + (x_bd[2 * n:3 * n] + x_bd[3 * n:4 * n])

    def par(g, row):
        return par_ref[g, row:row + 1, :]

    st_scr[...] = s0_ref[...]

    def trip(j, first_touch):
        rows = {}
        for sl in range(CHUNKS_PER_TRIP):
            rows[0, sl] = pl.ds(pl.multiple_of((CHUNKS_PER_TRIP * j + sl) * c, c), c)
            rows[1, sl] = pl.ds(pl.multiple_of((n_chunks - 1 - CHUNKS_PER_TRIP * j - sl) * c, c), c)
        lr = {key: lr_ref[rw, 0:lr_k] for key, rw in rows.items()}
        th = {key: jnp.tanh(x).astype(BF16) for key, x in lr.items()}
        lrb = {key: x.astype(BF16) for key, x in lr.items()}
        ss = [dict(r=r_ref[rows[d, sl], lanes(g)], k=k_ref[rows[d, sl], lanes(g)], v=v_ref[rows[d, sl], lanes(g)])
              for d, sl, g in probs]

        for (d, sl, g), s in zip(probs, ss):
            s["w_pre"] = par(g, P_W0 + d) + _dot(th[d, sl], wup_ref[d, g])
            s["a_pre"] = par(g, P_A0 + d) + _dot(lrb[d, sl], aup_ref[d, g])
            s["kkr"] = s["k"] * par(g, P_KK)
        for s, ssq in zip(ss, seg_sums([s["kkr"] * s["kkr"] for s in ss], 1)):
            s["ssq"] = ssq
        for (d, sl, g), s in zip(probs, ss):
            s["lw"] = -DECAY_SCALE * _sigmoid(s["w_pre"])
            s["cum"] = _dot_exact_lhs(tri_ref[d], s["lw"])
        for (d, sl, g), s in zip(probs, ss):
            a = _sigmoid(s["a_pre"])
            kk = s["kkr"] / jnp.maximum(jnp.sqrt(s["ssq"]), 1e-12)
            kd = s["k"] * (1.0 + (a - 1.0) * par(g, P_KA))
            cum = s["cum"]
            e_incl = jnp.exp(cum)
            e_neg = jnp.exp(-cum)
            gam = e_incl[c - 1:c, :] if d == 0 else e_incl[0:1, :]
            s["at"] = (-kk * jnp.exp(cum - s["lw"])).astype(BF16)
            s["rt"] = s["r"] * e_incl
            bt = kk * a * e_neg
            kt = kd * e_neg
            s["vb"] = s["v"].astype(BF16)
            s["bonus"] = s["r"] * kd * par(g, P_RK)
            s["gam_rows"] = _dot_exact_rhs(eyec_ref[...] * gam, bd_ref[...])
            tr = jnp.concatenate([bt, kt], axis=0).T
            swapped = pltpu.roll(tr, c, axis=1)
            first_half = lax.broadcasted_iota(jnp.int32, tr.shape, 1) < c
            bt_t = jnp.where(first_half, tr, swapped).astype(BF16)
            kt_t = jnp.where(first_half, swapped, tr).astype(BF16)
            s["bt_bd"] = jnp.concatenate([bt_t, bt_t], axis=1) * bd_ref[...]
            s["kt_bd"] = jnp.concatenate([kt_t, kt_t], axis=1) * bd_ref[...]
            lhs2 = jnp.concatenate([s["at"], s["rt"].astype(BF16)], axis=0)
            s["m_b"] = _dot(lhs2, s["bt_bd"])
            s["m_k"] = _dot(lhs2, s["kt_bd"])
        for s, bsum in zip(ss, seg_sums([s["bonus"] for s in ss], 1)):
            s["bsum"] = bsum
        for (d, sl, g), s in zip(probs, ss):
            causal = cm_ref[d]
            m_b = s["m_b"] * causal
            s["x"] = m_b[0:c]
            s["a_rb"] = m_b[c:2 * c].astype(BF16)
            s["t"] = eyec_ref[...] + s["x"]
            s["bt_c"] = fold(s["bt_bd"])
            av = _dot(jnp.concatenate([(s["m_k"] * causal).astype(BF16), fold(s["kt_bd"])], axis=0), rep(s["vb"]))
            s["g1"] = av[0:c].astype(BF16)
            s["y0"] = av[c:2 * c] + s["bsum"] * s["v"]
            s["z"] = av[2 * c:2 * c + RWKV_HEAD_DIM]

        for step in range(1, n_sq):
            for s in ss:
                x_bd = rep(s["x"].astype(BF16))
                if step == 1:
                    s["x"] = _dot(s["x"].astype(BF16), x_bd)
                else:
                    both = _dot(jnp.concatenate([s["x"], s["t"]], axis=0).astype(BF16), x_bd)
                    s["x"], s["t"] = both[0:c], s["t"] + both[c:2 * c]
        for s in ss:
            s["t"] = (s["t"] + _dot(s["t"].astype(BF16), rep(s["x"].astype(BF16)))).astype(BF16)

        for s in ss:
            s["w"] = _dot(s["t"], rep(s["at"])).astype(BF16)
            s["u0"] = _dot(s["t"], rep(s["g1"])).astype(BF16)
        for s in ss:
            lhs = jnp.concatenate([s["a_rb"], s["bt_c"]], axis=0)
            with_w = _dot(lhs, rep(s["w"]))
            with_u0 = _dot(lhs, rep(s["u0"]))
            s["q"] = (s["rt"] + with_w[0:c]).astype(BF16)
            s["p"] = (eyec_ref[...] + with_w[c:c + RWKV_HEAD_DIM]).astype(BF16)
            s["y0"] = s["y0"] + with_u0[0:c]
            s["z"] = s["z"] + with_u0[c:c + RWKV_HEAD_DIM]

        for (d, sl, g), s in zip(probs, ss):
            both = _dot(jnp.concatenate([s["p"], s["q"]], axis=0), rep(st_scr[g, d].astype(BF16)))
            st_scr[g, d] = s["gam_rows"] * (both[0:RWKV_HEAD_DIM] + s["z"])
            y = both[RWKV_HEAD_DIM:RWKV_HEAD_DIM + c] + s["y0"]
            if first_touch:
                y_scr[rows[d, sl], lanes(g)] = y
            else:
                y_scr[rows[d, sl], lanes(g)] += y

    half = n_trips // 2
    lax.fori_loop(0, half, lambda j, carry: (trip(j, True), carry)[1], 0)
    lax.fori_loop(half, n_trips, lambda j, carry: (trip(j, False), carry)[1], 0)
    sfin_ref[...] = st_scr[...]

    inv_n = 1.0 / RWKV_HEAD_DIM

    def finish(j, carry):
        rows = pl.ds(pl.multiple_of(j * TILE, TILE), TILE)
        sg = _sigmoid(lr_ref[rows, lr_k:HG_LANES]).astype(BF16)
        ys = [y_scr[rows, lanes(g)] for g in range(N_HG)]
        mus = [m * inv_n for m in seg_sums(ys, 2)]
        ycs = [y - mu for y, mu in zip(ys, mus)]
        variances = [v * inv_n for v in seg_sums([yc * yc for yc in ycs], 2)]
        for g in range(N_HG):
            yn = ycs[g] * lax.rsqrt(variances[g] + RWKV_GN_EPS) * par(g, P_GNG) + par(g, P_GNB)
            out_ref[rows, lanes(g)] = (yn * _dot(sg, gup_ref[g])).astype(BF16)
        return carry

    lax.fori_loop(0, t_len // TILE, finish, 0)


def _rwkv_mixer(zr, s0, n_seq, t_len, row_off, y_all, par, wup, aup, gup, consts):
    lr_block = 3 * RWKV_WIDTH // HG_LANES
    state_spec = pl.BlockSpec((None, N_HG, 2, RWKV_HEAD_DIM, HG_LANES), lambda b: (b, 0, 0, 0, 0))
    weights = (par, wup, aup, gup)
    return pl.pallas_call(
        functools.partial(_rwkv_kernel, t_len),
        grid=(n_seq,),
        in_specs=[
            pl.BlockSpec(memory_space=pl.ANY),
            pl.BlockSpec((t_len, RWKV_WIDTH), lambda b: (b + row_off, 0)),
            pl.BlockSpec((t_len, RWKV_WIDTH), lambda b: (b + row_off, 1)),
            pl.BlockSpec((t_len, RWKV_WIDTH), lambda b: (b + row_off, 2)),
            pl.BlockSpec((t_len, HG_LANES), lambda b: (b + row_off, lr_block)),
            state_spec,
            *[_const_spec(a.shape) for a in weights],
            *[_const_spec(a.shape) for a in consts],
        ],
        out_specs=[pl.BlockSpec((t_len, RWKV_WIDTH), lambda b: (b + row_off, 0)), state_spec],
        out_shape=[
            jax.ShapeDtypeStruct((zr.shape[0], RWKV_WIDTH), BF16),
            jax.ShapeDtypeStruct((n_seq, N_HG, 2, RWKV_HEAD_DIM, HG_LANES), F32),
        ],
        scratch_shapes=[
            pltpu.VMEM((t_len, RWKV_WIDTH), F32),
            pltpu.VMEM((N_HG, 2, RWKV_HEAD_DIM, HG_LANES), F32),
        ],
        input_output_aliases={0: 0},
        compiler_params=_cparams(("parallel",), RWKV_VMEM_LIMIT),
        name="rwkv_mixer",
    )(y_all, zr, zr, zr, zr, s0, *weights, *consts)


def _rwkv_consts():
    c = SCAN_CHUNK
    ri = jnp.arange(HG_LANES)[:, None]
    ci = jnp.arange(HG_LANES)[None, :]
    bd = (ri // RWKV_HEAD_DIM == ci // RWKV_HEAD_DIM).astype(BF16)
    t = jnp.arange(c)[:, None]
    i = (jnp.arange(HEADS_PER_HG * c) % c)[None, :]
    cm = jnp.stack([jnp.concatenate([i < t, i <= t]), jnp.concatenate([i > t, i >= t])]).astype(F32)
    ti = jnp.arange(c)[None, :]
    tri = jnp.stack([ti <= t, ti >= t]).astype(BF16)
    eyec = (i == t).astype(F32)
    return bd, cm, tri, eyec


def _states_to_kernel(s):
    b = s.shape[0]
    st = s.reshape(b, 2, N_HG, HEADS_PER_HG, RWKV_HEAD_DIM, RWKV_HEAD_DIM)
    return jnp.transpose(st, (0, 2, 1, 5, 3, 4)).reshape(b, N_HG, 2, RWKV_HEAD_DIM, HG_LANES)


def _states_from_kernel(sk):
    b = sk.shape[0]
    st = sk.reshape(b, N_HG, 2, RWKV_HEAD_DIM, HEADS_PER_HG, RWKV_HEAD_DIM)
    return jnp.transpose(st, (0, 2, 1, 4, 5, 3)).reshape(b, 2, RWKV_HEADS, RWKV_HEAD_DIM, RWKV_HEAD_DIM)


def _pool_tables():
    wlen = TILE + 2 * HALO
    t_loc = jnp.arange(TILE)[:, None]
    s_rel = jnp.arange(wlen)[None, :] - HALO
    lane_g = jnp.arange(POOL_WIDTH)[None, :] // (POOL_WIDTH // POOL_GROUPS)
    bands, counts = [], []
    for first in (0, 1):
        for last in (0, 1):
            lo_seq = 0 if first else -HALO
            hi_seq = TILE if last else TILE + HALO
            band_groups = []
            cnt = jnp.zeros((TILE, POOL_WIDTH), F32)
            for gi, win in enumerate(POOL_WINDOWS):
                lo = jnp.maximum(t_loc - win // 2, lo_seq)
                hi = jnp.minimum(t_loc + win - win // 2, hi_seq)
                band_groups.append((s_rel >= lo) & (s_rel < hi))
                cnt = jnp.where(lane_g == gi, (hi - lo).astype(F32), cnt)
            bands.append(jnp.concatenate(band_groups, axis=1))
            counts.append(cnt)
    return jnp.stack(bands).astype(BF16), jnp.stack(counts)


def _mixers_kernel(ctx_tiles, seq_tiles_ctx, seq_tiles_lat, zc_ref, zp_ref, zn_ref, dw_ref, vec_ref, wcat_ref,
                   bs_ref, pw_ref, sgm_ref, plm_ref, band_ref, cnt_ref, out_ref, win_scr, shift_scr):
    i = pl.program_id(0)
    j_ctx = i % seq_tiles_ctx
    j_lat = jnp.maximum(i - ctx_tiles, 0) % seq_tiles_lat
    is_ctx = i < ctx_tiles
    first = jnp.where(is_ctx, j_ctx == 0, j_lat == 0)
    last = jnp.where(is_ctx, j_ctx == seq_tiles_ctx - 1, j_lat == seq_tiles_lat - 1)
    keep_prev = jnp.where(first, 0.0, 1.0)
    keep_next = jnp.where(last, 0.0, 1.0)
    case = 2 * first.astype(jnp.int32) + last.astype(jnp.int32)

    cw = CONV_WIDTH
    conv_b, cln_g, cln_b = vec_ref[0:1, :], vec_ref[1:2, :], vec_ref[2:3, :]
    sln_g, sln_b, pool_scale = vec_ref[3:4, :], vec_ref[4:5, :], vec_ref[5:6, :]

    glu = lambda z_ref: z_ref[:, 0:cw].astype(F32) * _sigmoid(z_ref[:, cw:2 * cw].astype(F32))
    win_scr[0:HALO, :] = glu(zp_ref) * keep_prev
    win_scr[HALO:HALO + TILE, :] = glu(zc_ref)
    win_scr[HALO + TILE:2 * HALO + TILE, :] = glu(zn_ref) * keep_next
    for r in range(1, SUBLANES):
        shift_scr[r] = win_scr[pl.ds(r, SHIFTED_ROWS), :]
    acc = jnp.zeros((TILE, cw), F32) + conv_b
    pad = CONV_KERNEL // 2
    for j in range(CONV_KERNEL):
        q, r = divmod(HALO - pad + j, SUBLANES)
        rows = pl.ds(q * SUBLANES, TILE)
        tap = win_scr[rows, :] if r == 0 else shift_scr[r, rows, :]
        acc = acc + tap * dw_ref[j:j + 1, :]
    out_ref[:, 0:cw] = _silu(_layer_norm(acc, cln_g, cln_b)).astype(BF16)

    su = zc_ref[:, 2 * cw:2 * cw + SGU_WIDTH].astype(F32)
    sv = zc_ref[:, 2 * cw + SGU_WIDTH:2 * cw + 2 * SGU_WIDTH].astype(F32)
    vn = _layer_norm(sv, sln_g, sln_b).astype(BF16)
    sgm = sgm_ref[...]
    for ch in range(TILE // SGU_CHUNK):
        rows = slice(ch * SGU_CHUNK, (ch + 1) * SGU_CHUNK)
        s = _dot(wcat_ref[...], _rep(vn[rows], SGU_GROUPS, sgm)) + bs_ref[...]
        out_ref[rows, cw:cw + SGU_WIDTH] = (su[rows] * s).astype(BF16)

    zoff = 2 * cw + 2 * SGU_WIDTH
    zcur = zc_ref[:, zoff:zoff + POOL_WIDTH]
    zw = jnp.concatenate([zp_ref[:, zoff:zoff + POOL_WIDTH], zcur, zn_ref[:, zoff:zoff + POOL_WIDTH]], axis=0)
    psum = _dot(band_ref[case], _rep(zw, POOL_GROUPS, plm_ref[...]))
    p = psum / cnt_ref[case] - zcur.astype(F32)
    pooled = _dot(p.astype(BF16), pw_ref[...]) * pool_scale
    out_ref[:, cw + SGU_WIDTH:cw + SGU_WIDTH + POOL_WIDTH] = pooled.astype(BF16)


def _mixers(zm, ctx_tiles, seq_tiles_ctx, seq_tiles_lat, *tables):
    n = zm.shape[0]
    n_tiles = n // TILE
    per = TILE // HALO
    n_halo = n // HALO
    return pl.pallas_call(
        functools.partial(_mixers_kernel, ctx_tiles, seq_tiles_ctx, seq_tiles_lat),
        grid=(n_tiles,),
        in_specs=[
            pl.BlockSpec((TILE, MX_COLS), lambda i: (i, 0)),
            pl.BlockSpec((HALO, MX_COLS), lambda i: (jnp.maximum(i * per - 1, 0), 0)),
            pl.BlockSpec((HALO, MX_COLS), lambda i: (jnp.minimum((i + 1) * per, n_halo - 1), 0)),
            *[_const_spec(a.shape) for a in tables],
        ],
        out_specs=pl.BlockSpec((TILE, MX_OUT), lambda i: (i, 0)),
        out_shape=jax.ShapeDtypeStruct((n, MX_OUT), BF16),
        scratch_shapes=[
            pltpu.VMEM((TILE + 2 * HALO, CONV_WIDTH), F32),
            pltpu.VMEM((SUBLANES, SHIFTED_ROWS, CONV_WIDTH), F32),
        ],
        compiler_params=_cparams(("parallel",)),
        name="mixers",
    )(zm, zm, zm, *tables)


def _route(logits_t):
    m = jnp.max(logits_t, axis=0, keepdims=True)
    e = jnp.exp(logits_t - m)
    p = e / jnp.sum(e, axis=0, keepdims=True)
    best_score = None
    best = None
    for g in range(N_EXPERT_GROUPS):
        rows = [p[g * EXPERTS_PER_GROUP + q:g * EXPERTS_PER_GROUP + q + 1] for q in range(EXPERTS_PER_GROUP)]
        score = None
        for a in range(EXPERTS_PER_GROUP):
            for b in range(a + 1, EXPERTS_PER_GROUP):
                pair = rows[a] + rows[b]
                score = pair if score is None else jnp.maximum(score, pair)
        if g == 0:
            best_score, best = score, jnp.zeros(score.shape, jnp.int32)
        else:
            upd = score > best_score
            best = jnp.where(upd, g, best)
            best_score = jnp.where(upd, score, best_score)
    eidx = lax.broadcasted_iota(jnp.int32, p.shape, 0)
    neg = -jnp.inf
    masked = jnp.where(eidx // EXPERTS_PER_GROUP == best, p, neg)
    m1 = jnp.max(masked, axis=0, keepdims=True)
    i1 = jnp.min(jnp.where(masked == m1, eidx, N_EXPERTS), axis=0, keepdims=True)
    masked2 = jnp.where(eidx == i1, neg, masked)
    m2 = jnp.max(masked2, axis=0, keepdims=True)
    i2 = jnp.min(jnp.where(masked2 == m2, eidx, N_EXPERTS), axis=0, keepdims=True)
    tot = m1 + m2
    return jnp.concatenate([i1, i2], axis=0), jnp.concatenate([m1 / tot, m2 / tot], axis=0)


def _merge_kernel(x_ref, mod_ref, n1_ref, n2_ref, ya_ref, mx_ref, wg_ref, bg_ref, wro_ref, wco_ref, wso_ref,
                  wpo_ref, wout_ref, rw_ref, rb_ref, x1_ref, h2_ref, ri_ref, rwt_ref):
    d = D_MODEL
    x = x_ref[...]
    hb = _norm_mod(x, n1_ref[...], mod_ref[:, 0:d], mod_ref[:, d:2 * d]).astype(BF16)
    cw = CONV_WIDTH
    branches = (
        (ya_ref[...], wro_ref),
        (mx_ref[:, 0:cw], wco_ref),
        (mx_ref[:, cw:cw + SGU_WIDTH], wso_ref),
        (mx_ref[:, cw + SGU_WIDTH:MX_OUT], wpo_ref),
    )
    merged = None
    for bi, (y_in, w_ref) in enumerate(branches):
        zg = _dot(hb, wg_ref[:, bi * d:(bi + 1) * d]) + bg_ref[:, bi * d:(bi + 1) * d]
        term = _sigmoid(zg) * _dot(y_in, w_ref[...])
        merged = term if merged is None else merged + term
    x1 = x + mod_ref[:, 2 * d:3 * d] * _dot(merged.astype(BF16), wout_ref[...])
    x1_ref[...] = x1
    h2 = _norm_mod(x1, n2_ref[...], mod_ref[:, 3 * d:4 * d], mod_ref[:, 4 * d:5 * d])
    h2_ref[...] = h2
    logits = _dot3(h2, rw_ref[...])
    ri_ref[...], rwt_ref[...] = _route(logits.T[0:N_EXPERTS] + rb_ref[...])


def _merge(x, mod_l, mod_row, n1, n2, ya, mx, wg, bg, wro, wco, wso, wpo, wout, rw_t, rb):
    n = x.shape[0]
    tm = TILE
    consts = (n1, n2)
    weights = (wg, bg, wro, wco, wso, wpo, wout, rw_t, rb)
    return pl.pallas_call(
        _merge_kernel,
        grid=(n // tm,),
        in_specs=[
            pl.BlockSpec((tm, D_MODEL), lambda i: (i, 0)),
            pl.BlockSpec((None, 1, 6 * D_MODEL), lambda i: (mod_row(i), 0, 0)),
            *[_const_spec(a.shape) for a in consts],
            pl.BlockSpec((tm, RWKV_WIDTH), lambda i: (i, 0)),
            pl.BlockSpec((tm, MX_OUT), lambda i: (i, 0)),
            *[_const_spec(a.shape) for a in weights],
        ],
        out_specs=[
            pl.BlockSpec((tm, D_MODEL), lambda i: (i, 0)),
            pl.BlockSpec((tm, D_MODEL), lambda i: (i, 0)),
            pl.BlockSpec((TOP_K, tm), lambda i: (0, i)),
            pl.BlockSpec((TOP_K, tm), lambda i: (0, i)),
        ],
        out_shape=[
            jax.ShapeDtypeStruct((n, D_MODEL), F32),
            jax.ShapeDtypeStruct((n, D_MODEL), F32),
            jax.ShapeDtypeStruct((TOP_K, n), jnp.int32),
            jax.ShapeDtypeStruct((TOP_K, n), F32),
        ],
        compiler_params=_cparams(("parallel",)),
        name="merge_router",
    )(x, mod_l, n1, n2, ya, mx, *weights)


PAIRS_PER_GROUP = EXPERTS_PER_GROUP * (EXPERTS_PER_GROUP - 1) // 2
N_PAIR_CLASSES = N_EXPERT_GROUPS * PAIRS_PER_GROUP
ROUTED_TILE = 256


def _dispatch_plan(route_i, route_w):
    n = route_i.shape[1]
    tm = ROUTED_TILE
    n_tiles = n // tm + N_PAIR_CLASSES
    i1, i2 = route_i[0], route_i[1]
    first_lower = i1 < i2
    lo, hi = jnp.minimum(i1, i2), jnp.maximum(i1, i2)
    w_lo = jnp.where(first_lower, route_w[0], route_w[1])
    w_hi = jnp.where(first_lower, route_w[1], route_w[0])
    a, b = lo % EXPERTS_PER_GROUP, hi % EXPERTS_PER_GROUP
    assert EXPERTS_PER_GROUP == 4
    pair = a * (2 * EXPERTS_PER_GROUP - 1 - a) // 2 + (b - a - 1)
    pair = jnp.where(pair == 3, 4, jnp.where(pair == 4, 3, pair))
    cls = (lo // EXPERTS_PER_GROUP) * PAIRS_PER_GROUP + pair
    onehot = (cls[:, None] == jnp.arange(N_PAIR_CLASSES)[None, :]).astype(jnp.int32)
    counts = jnp.sum(onehot, axis=0)
    rank = jnp.sum((jnp.cumsum(onehot, axis=0) - onehot) * onehot, axis=1)
    padded = (counts + tm - 1) // tm * tm
    ends = jnp.cumsum(padded)
    dest = (ends - padded)[cls] + rank
    tile_row0 = jnp.arange(n_tiles, dtype=jnp.int32) * tm
    tile_valid = (tile_row0 < ends[-1]).astype(jnp.int32)
    tile_cls = jnp.minimum(jnp.searchsorted(ends, tile_row0, side="right"), N_PAIR_CLASSES - 1)
    stride = 1 << 15
    assert n + tm <= stride and N_PAIR_CLASSES * stride < (1 << 30)
    unused = jnp.iinfo(jnp.int32).max
    j = jnp.arange(tm, dtype=jnp.int32)[None, :]
    pad_keys = jnp.where(j < (padded - counts)[:, None], jnp.arange(N_PAIR_CLASSES)[:, None] * stride + n + j, unused)
    keys = jnp.concatenate([cls * stride + jnp.arange(n, dtype=jnp.int32), pad_keys.reshape(-1).astype(jnp.int32)])
    no_weight = jnp.zeros((N_PAIR_CLASSES * tm,), F32)
    keys, w_lo_sorted, w_hi_sorted = lax.sort(
        (keys, jnp.concatenate([w_lo, no_weight]), jnp.concatenate([w_hi, no_weight])), num_keys=1)
    token = keys % stride
    src = jnp.where((token < n) & (keys != unused), token, 0)
    w_sorted = jnp.stack([w_lo_sorted, w_hi_sorted], axis=1)
    cls_ids = jnp.arange(N_PAIR_CLASSES)
    pa = jnp.array([0, 0, 0, 1, 1, 2], jnp.int32)
    pb = jnp.array([1, 2, 3, 3, 2, 3], jnp.int32)
    cls_lo = (cls_ids // PAIRS_PER_GROUP) * EXPERTS_PER_GROUP + pa[cls_ids % PAIRS_PER_GROUP]
    cls_hi = (cls_ids // PAIRS_PER_GROUP) * EXPERTS_PER_GROUP + pb[cls_ids % PAIRS_PER_GROUP]
    last_cls = tile_cls[jnp.maximum(ends[-1] // tm - 1, 0)]
    tile_cls = jnp.where(tile_valid == 1, tile_cls, last_cls)
    tiles_used = (ends[-1:] // tm).astype(jnp.int32)
    return (dest.astype(jnp.int32), src, w_sorted, cls_lo[tile_cls].astype(jnp.int32),
            cls_hi[tile_cls].astype(jnp.int32), tiles_used)


def _start_row_gather(idx_ref, idx0, rows_hbm, buf, sem, n_rows):
    for r in range(n_rows):
        pltpu.make_async_copy(rows_hbm.at[pl.ds(idx_ref[idx0 + r], 1)], buf.at[pl.ds(r, 1)], sem).start()


def _wait_row_gather(rows_hbm, buf, sem, n_rows):
    pltpu.make_async_copy(rows_hbm.at[pl.ds(0, n_rows)], buf, sem).wait()


def _pipelined_gather(idx_ref, rows_hbm, bufs, sems, n_rows):
    i = pl.program_id(0)
    slot = i % 2

    @pl.when(i == 0)
    def _():
        _start_row_gather(idx_ref, 0, rows_hbm, bufs.at[0], sems.at[0], n_rows)

    @pl.when(i + 1 < pl.num_programs(0))
    def _():
        _start_row_gather(idx_ref, (i + 1) * n_rows, rows_hbm, bufs.at[1 - slot], sems.at[1 - slot], n_rows)

    _wait_row_gather(rows_hbm, bufs.at[slot], sems.at[slot], n_rows)
    return slot


EXPERT_GATHER_DEPTH = 2


def _experts_kernel(lo_ref, hi_ref, used_ref, src_ref, w_ref, h_hbm, wg_lo, wu_lo, wd_lo, wg_hi, wu_hi, wd_hi,
                    o_ref, hbuf, sems, up_scr, down_scr):
    tm = ROUTED_TILE
    n_slots = EXPERT_GATHER_DEPTH + 1
    i = pl.program_id(0)
    n = used_ref[0]
    slot = i % n_slots

    @pl.when(i == 0)
    def _():
        for t in range(EXPERT_GATHER_DEPTH):
            @pl.when(t < n)
            def _():
                _start_row_gather(src_ref, t * tm, h_hbm, hbuf.at[t], sems.at[t], tm)

    def tile(prefetch):
        _wait_row_gather(h_hbm, hbuf.at[slot], sems.at[slot], tm)
        ahead = (i + EXPERT_GATHER_DEPTH) % n_slots
        rows = iter(range(tm))

        def request(count):
            if not prefetch:
                return
            for r in (next(rows) for _ in range(count)):
                row = src_ref[(i + EXPERT_GATHER_DEPTH) * tm + r]
                pltpu.make_async_copy(h_hbm.at[pl.ds(row, 1)], hbuf.at[ahead, pl.ds(r, 1)], sems.at[ahead]).start()

        hb = hbuf[slot].astype(BF16)
        per_dot = tm // 6

        def expert(which, expert_of_tile, wg, wu, wd):
            changed = (i == 0) | (expert_of_tile[i] != expert_of_tile[jnp.maximum(i - 1, 0)])

            @pl.when(changed)
            def _():
                up_scr[2 * which] = wg[...].astype(BF16)
                up_scr[2 * which + 1] = wu[...].astype(BF16)
                down_scr[which] = wd[...].astype(BF16)

            request(per_dot)
            gate = _dot(hb, up_scr[2 * which])
            request(per_dot)
            up = _dot(hb, up_scr[2 * which + 1])
            request(per_dot)
            return _dot((_silu(gate) * up).astype(BF16), down_scr[which])

        w = w_ref[...]
        out = w[:, 0:1] * expert(0, lo_ref, wg_lo, wu_lo, wd_lo) + w[:, 1:2] * expert(1, hi_ref, wg_hi, wu_hi, wd_hi)
        request(tm - 6 * per_dot)
        o_ref[...] = out

    @pl.when(i + EXPERT_GATHER_DEPTH < n)
    def _():
        tile(True)

    @pl.when((i < n) & (i + EXPERT_GATHER_DEPTH >= n))
    def _():
        tile(False)

    @pl.when(i >= n)
    def _():
        o_ref[...] = jnp.zeros(o_ref.shape, F32)


def _experts(h2, src, w_sorted, tile_lo, tile_hi, tiles_used, layer, wg, wu, wd):
    tm = ROUTED_TILE
    n_tiles = tile_lo.shape[0]
    n_slots = EXPERT_GATHER_DEPTH + 1
    up_spec = lambda sel: pl.BlockSpec((None, None, D_MODEL, D_EXPERT), lambda i, lo, hi, u, s: (layer, sel(lo, hi)[i], 0, 0))
    down_spec = lambda sel: pl.BlockSpec((None, None, D_EXPERT, D_MODEL), lambda i, lo, hi, u, s: (layer, sel(lo, hi)[i], 0, 0))
    pick_lo = lambda lo, hi: lo
    pick_hi = lambda lo, hi: hi
    return pl.pallas_call(
        _experts_kernel,
        grid_spec=pltpu.PrefetchScalarGridSpec(
            num_scalar_prefetch=4,
            grid=(n_tiles,),
            in_specs=[
                pl.BlockSpec((tm, TOP_K), lambda i, *_: (i, 0)),
                pl.BlockSpec(memory_space=pl.ANY),
                up_spec(pick_lo), up_spec(pick_lo), down_spec(pick_lo),
                up_spec(pick_hi), up_spec(pick_hi), down_spec(pick_hi),
            ],
            out_specs=pl.BlockSpec((tm, D_MODEL), lambda i, *_: (i, 0)),
            scratch_shapes=[
                pltpu.VMEM((n_slots, tm, D_MODEL), F32),
                pltpu.SemaphoreType.DMA((n_slots,)),
                pltpu.VMEM((2 * TOP_K, D_MODEL, D_EXPERT), BF16),
                pltpu.VMEM((TOP_K, D_EXPERT, D_MODEL), BF16),
            ],
        ),
        out_shape=jax.ShapeDtypeStruct((n_tiles * tm, D_MODEL), F32),
        compiler_params=_cparams(("arbitrary",)),
        name="experts",
    )(tile_lo, tile_hi, tiles_used, src, w_sorted, h2, wg, wu, wd, wg, wu, wd)


def _combine_kernel(dest_ref, x_ref, mod_ref, ff_hbm, o_ref, fbuf, sems):
    slot = _pipelined_gather(dest_ref, ff_hbm, fbuf, sems, ROUTED_TILE)
    o_ref[...] = x_ref[...] + mod_ref[:, 5 * D_MODEL:6 * D_MODEL] * fbuf[slot]


def _combine(ff_sorted, dest, x1, mod_l, mod_row):
    n = x1.shape[0]
    tm = ROUTED_TILE
    return pl.pallas_call(
        _combine_kernel,
        grid_spec=pltpu.PrefetchScalarGridSpec(
            num_scalar_prefetch=1,
            grid=(n // tm,),
            in_specs=[
                pl.BlockSpec((tm, D_MODEL), lambda i, d: (i, 0)),
                pl.BlockSpec((None, 1, 6 * D_MODEL), lambda i, d: (mod_row(i), 0, 0)),
                pl.BlockSpec(memory_space=pl.ANY),
            ],
            out_specs=pl.BlockSpec((tm, D_MODEL), lambda i, d: (i, 0)),
            scratch_shapes=[pltpu.VMEM((2, tm, D_MODEL), F32), pltpu.SemaphoreType.DMA((2,))],
        ),
        out_shape=jax.ShapeDtypeStruct((n, D_MODEL), F32),
        compiler_params=_cparams(("arbitrary",)),
        name="moe_combine",
    )(dest, x1, mod_l, ff_sorted)


def _final_norm_kernel(x_ref, g_ref, o_ref):
    x = x_ref[...]
    o_ref[...] = x * lax.rsqrt(jnp.mean(x * x, axis=-1, keepdims=True) + NORM_EPS) * g_ref[...]


def _final_norm(x, g, tile0, n_tiles):
    return pl.pallas_call(
        _final_norm_kernel,
        grid=(n_tiles,),
        in_specs=[pl.BlockSpec((IO_TILE, D_MODEL), lambda i: (i + tile0, 0)), _const_spec((1, D_MODEL))],
        out_specs=pl.BlockSpec((IO_TILE, D_MODEL), lambda i: (i, 0)),
        out_shape=jax.ShapeDtypeStruct((n_tiles * IO_TILE, D_MODEL), F32),
        compiler_params=_cparams(("parallel",)),
        name="final_norm",
    )(x, g)


def _grid_pos_embed(n_tokens):
    rows = n_tokens // GRID_W
    quarter = D_MODEL // 4
    half = D_MODEL // 2
    omega = 1.0 / (10000.0 ** (jnp.arange(quarter, dtype=F32) / quarter))
    ang_r = jnp.arange(rows, dtype=F32)[:, None] * omega
    ang_c = jnp.arange(GRID_W, dtype=F32)[:, None] * omega
    emb_r = jnp.concatenate([jnp.sin(ang_r), jnp.cos(ang_r)], axis=-1)
    emb_c = jnp.concatenate([jnp.sin(ang_c), jnp.cos(ang_c)], axis=-1)
    emb = jnp.concatenate([jnp.broadcast_to(emb_r[:, None, :], (rows, GRID_W, half)),
                           jnp.broadcast_to(emb_c[None, :, :], (rows, GRID_W, half))], axis=-1)
    return emb.reshape(rows * GRID_W, D_MODEL)


def _pad_rows(w, rows, offset):
    out = jnp.zeros(w.shape[:-2] + (rows, w.shape[-1]), w.dtype)
    return lax.dynamic_update_slice_in_dim(out, w, offset, axis=-2)


def _hg_cols(w):
    return jnp.moveaxis(w.reshape(w.shape[:-1] + (N_HG, HG_LANES)), -2, 0)


def kernel(x_prompt, x_sample, state_rwkv, c, c_ctx, norm1_g, norm2_g, w_mod, b_mod, w_in, b_in, rwkv_w0, rwkv_w_up,
           rwkv_a0, rwkv_a_up, rwkv_g_up, rwkv_k_k, rwkv_k_a, rwkv_r_k, rwkv_gn_g, rwkv_gn_b, rwkv_w_o, conv_dw,
           conv_dw_b, conv_ln_g, conv_ln_b, conv_w_o, sgu_ln_g, sgu_ln_b, sgu_w_s, sgu_b_s, sgu_w_o, pool_w,
           pool_scale, pool_w_o, w_out, moe_w_gate, moe_w_up, moe_w_down, router_w, router_b, final_norm_g):
    n_ctx, t_ctx, d = x_prompt.shape
    n_lat, t_lat, _ = x_sample.shape
    n_layers = w_in.shape[0]
    assert d == D_MODEL and t_ctx % TILE == 0 and t_lat % TILE == 0 and n_lat < MOD_ROWS
    ctx_rows, lat_rows = n_ctx * t_ctx, n_lat * t_lat
    assert ctx_rows % t_lat == 0 and t_lat % PROJ_TILE == 0 and ROUTED_TILE == TILE
    ctx_tiles = ctx_rows // TILE
    seq_tiles_ctx, seq_tiles_lat = t_ctx // TILE, t_lat // TILE

    def mod_row_for(tile_rows):
        ctx_t = ctx_rows // tile_rows
        per_lat = t_lat // tile_rows
        return lambda i: jnp.where(i < ctx_t, n_lat, jnp.maximum(i - ctx_t, 0) // per_lat)

    mod_row = mod_row_for(TILE)
    mod_row_proj = mod_row_for(PROJ_TILE)

    cond = jnp.zeros((MOD_ROWS, d), F32).at[:n_lat].set(c).at[n_lat].set(c_ctx)
    mod = _modulation(cond, w_mod, b_mod).reshape(n_layers, MOD_ROWS, 1, 6 * d)

    assert ctx_rows % IO_TILE == 0 and t_lat % IO_TILE == 0
    x = _embed(x_prompt.reshape(ctx_rows, d), x_sample.reshape(lat_rows, d), _grid_pos_embed(t_lat), t_lat // IO_TILE)

    n_rw = RW_COLS
    n_mx = MX_COLS
    wa = w_in[:, :, :n_rw].astype(BF16)
    ba = b_in[:, None, :n_rw]
    wm = w_in[:, :, n_rw:n_rw + n_mx].astype(BF16)
    bm = b_in[:, None, n_rw:n_rw + n_mx]
    wgt = w_in[:, :, n_rw + n_mx:].astype(BF16)
    bgt = b_in[:, None, n_rw + n_mx:]

    par = jnp.zeros((n_layers, P_ROWS, RWKV_WIDTH), F32)
    par = par.at[:, P_W0:P_W0 + 2].set(rwkv_w0).at[:, P_A0:P_A0 + 2].set(rwkv_a0)
    par = par.at[:, P_KK].set(rwkv_k_k).at[:, P_KA].set(rwkv_k_a)
    par = par.at[:, P_RK].set(rwkv_r_k.reshape(n_layers, RWKV_WIDTH))
    par = par.at[:, P_GNG].set(rwkv_gn_g).at[:, P_GNB].set(rwkv_gn_b)
    par = jnp.moveaxis(_hg_cols(par), 0, 1)
    lr_k = RWKV_DECAY_RANK + RWKV_ICLR_RANK
    wup = jnp.moveaxis(_hg_cols(_pad_rows(rwkv_w_up, lr_k, 0)), 0, 2).astype(BF16)
    aup = jnp.moveaxis(_hg_cols(_pad_rows(rwkv_a_up, lr_k, RWKV_DECAY_RANK)), 0, 2).astype(BF16)
    gup = jnp.moveaxis(_hg_cols(rwkv_g_up), 0, 1).astype(BF16)
    rw_consts = _rwkv_consts()

    mix_vecs = jnp.zeros((n_layers, 8, CONV_WIDTH), F32)
    for row, vec in enumerate((conv_dw_b, conv_ln_g, conv_ln_b, sgu_ln_g, sgu_ln_b, pool_scale)):
        mix_vecs = mix_vecs.at[:, row].set(vec)
    sgu_wcat = jnp.transpose(sgu_w_s, (0, 2, 1, 3)).reshape(n_layers, SGU_CHUNK, SGU_GROUPS * SGU_CHUNK).astype(BF16)
    sgu_bs = jnp.repeat(jnp.swapaxes(sgu_b_s, 1, 2), SGU_WIDTH // SGU_GROUPS, axis=2)
    pc = POOL_WIDTH // POOL_GROUPS
    pool_bd = (pool_w[:, :, :, None, :] * jnp.eye(POOL_GROUPS, dtype=F32)[None, :, None, :, None])
    pool_bd = pool_bd.reshape(n_layers, POOL_WIDTH, POOL_WIDTH).astype(BF16)
    lane_grp = jnp.arange(SGU_WIDTH)[None, :] // (SGU_WIDTH // SGU_GROUPS)
    sgm = (jnp.arange(SGU_GROUPS * SGU_CHUNK)[:, None] // SGU_CHUNK == lane_grp).astype(BF16)
    wlen = TILE + 2 * HALO
    plm = (jnp.arange(POOL_GROUPS * wlen)[:, None] // wlen == jnp.arange(POOL_WIDTH)[None, :] // pc).astype(BF16)
    pool_band, pool_cnt = _pool_tables()

    wro, wco, wso, wpo, wout = (w.astype(BF16) for w in (rwkv_w_o, conv_w_o, sgu_w_o, pool_w_o, w_out))
    rw_t = jnp.pad(router_w, ((0, 0), (0, ROUTER_LANES - N_EXPERTS)))
    rb = router_b[:, None]

    ctx_s0 = jnp.zeros((n_ctx, N_HG, 2, RWKV_HEAD_DIM, HG_LANES), F32)
    ctx_states = []
    for l in range(n_layers):
        zr, zm = _in_projection(x, mod[l], mod_row_proj, norm1_g[l][None], wa[l], ba[l], wm[l], bm[l])
        rw_args = (par[l], wup[l], aup[l], gup[l], rw_consts)
        ya = jnp.zeros((x.shape[0], RWKV_WIDTH), BF16)
        ya, s_fin = _rwkv_mixer(zr, ctx_s0, n_ctx, t_ctx, 0, ya, *rw_args)
        ya, _ = _rwkv_mixer(zr, _states_to_kernel(state_rwkv[:, l]), n_lat, t_lat, ctx_rows // t_lat, ya, *rw_args)
        ctx_states.append(_states_from_kernel(s_fin))
        mx = _mixers(zm, ctx_tiles, seq_tiles_ctx, seq_tiles_lat, conv_dw[l], mix_vecs[l], sgu_wcat[l], sgu_bs[l],
                     pool_bd[l], sgm, plm, pool_band, pool_cnt)
        x1, h2, route_i, route_w = _merge(x, mod[l], mod_row, norm1_g[l][None], norm2_g[l][None], ya, mx, wgt[l], bgt[l],
                                 wro[l], wco[l], wso[l], wpo[l], wout[l], rw_t, rb)
        dest, src, w_sorted, tile_lo, tile_hi, tiles_used = _dispatch_plan(route_i, route_w)
        ff = _experts(h2, src, w_sorted, tile_lo, tile_hi, tiles_used, l, moe_w_gate, moe_w_up, moe_w_down)
        x = _combine(ff, dest, x1, mod[l], mod_row)

    g_fin = final_norm_g[None]
    y_prompt = _final_norm(x, g_fin, 0, ctx_rows // IO_TILE).reshape(n_ctx, t_ctx, d)
    y_sample = _final_norm(x, g_fin, ctx_rows // IO_TILE, lat_rows // IO_TILE).reshape(n_lat, t_lat, d)
    new_state = jnp.stack(ctx_states, axis=1).astype(x_prompt.dtype)
    return (y_prompt, y_sample, new_state)
```

```python
import functools
import math

import jax
import jax.numpy as jnp
from jax import lax
from jax.experimental import pallas as pl
from jax.experimental.pallas import tpu as pltpu

F32 = jnp.float32
BF16 = jnp.bfloat16

D_MODEL = 1024
GRID_W = 64
RWKV_HEADS = 8
RWKV_HEAD_DIM = 64
RWKV_WIDTH = RWKV_HEADS * RWKV_HEAD_DIM
RWKV_DECAY_RANK = 64
RWKV_ICLR_RANK = 64
RWKV_GATE_RANK = 128
RWKV_GN_EPS = 64e-5
CONV_WIDTH = 256
CONV_KERNEL = 31
SGU_GROUPS = 4
SGU_WIDTH = 256
SGU_CHUNK = 128
POOL_GROUPS = 4
POOL_WIDTH = 256
POOL_WINDOWS = (2, 4, 8, 16)
N_BRANCHES = 4
N_EXPERTS = 16
TOP_K = 2
N_EXPERT_GROUPS = 4
EXPERTS_PER_GROUP = N_EXPERTS // N_EXPERT_GROUPS
D_EXPERT = 512
NORM_EPS = 1e-6
LN_EPS = 1e-5

RW_COLS = 3 * RWKV_WIDTH + RWKV_DECAY_RANK + RWKV_ICLR_RANK + RWKV_GATE_RANK
MX_COLS = 2 * CONV_WIDTH + 2 * SGU_WIDTH + POOL_WIDTH
MX_OUT = CONV_WIDTH + SGU_WIDTH + POOL_WIDTH

TILE = 256
PROJ_TILE = 512
IO_TILE = 1024
HALO = 16
SUBLANES = 8
SHIFTED_ROWS = TILE + 2 * HALO - SUBLANES
HG_LANES = 256
HEADS_PER_HG = HG_LANES // RWKV_HEAD_DIM
N_HG = RWKV_WIDTH // HG_LANES
SCAN_CHUNK = 64
assert SCAN_CHUNK == RWKV_HEAD_DIM
CHUNKS_PER_TRIP = 2
DECAY_SCALE = math.exp(-0.5)
MOD_ROWS = 16
ROUTER_LANES = 128

VMEM_LIMIT = 48 * 1024 * 1024
RWKV_VMEM_LIMIT = 56 * 1024 * 1024


def _cparams(sem, vmem_limit=VMEM_LIMIT):
    return pltpu.CompilerParams(dimension_semantics=sem, vmem_limit_bytes=vmem_limit)


def _dot(a, b):
    return jnp.dot(a, b, preferred_element_type=F32)


def _split2(x):
    hi = x.astype(BF16)
    lo = (x - hi.astype(F32)).astype(BF16)
    return hi, lo


def _dot3(a, b):
    a_hi, a_lo = _split2(a)
    b_hi, b_lo = _split2(b)
    return _dot(a_hi, b_hi) + (_dot(a_lo, b_hi) + _dot(a_hi, b_lo))


def _dot_exact_rhs(x, rhs):
    hi, lo = _split2(x)
    return _dot(hi, rhs) + _dot(lo, rhs)


def _dot_exact_lhs(lhs, x):
    hi, lo = _split2(x)
    return _dot(lhs, hi) + _dot(lhs, lo)


def _sigmoid(x):
    return 1.0 / (1.0 + jnp.exp(-x))


def _silu(x):
    return x * _sigmoid(x)


def _norm_mod(x, g, shift, scale):
    y = x * lax.rsqrt(jnp.mean(x * x, axis=-1, keepdims=True) + NORM_EPS) * g
    return y * (1.0 + scale) + shift


def _layer_norm(x, g, b):
    mu = jnp.mean(x, axis=-1, keepdims=True)
    xc = x - mu
    var = jnp.mean(xc * xc, axis=-1, keepdims=True)
    return xc * lax.rsqrt(var + LN_EPS) * g + b


def _const_spec(shape):
    nd = len(shape)
    return pl.BlockSpec(shape, lambda *_: (0,) * nd)


def _mod_kernel(c_ref, w_ref, b_ref, o_ref):
    o_ref[...] = _dot3(_silu(c_ref[...]), w_ref[...]) + b_ref[...]


def _modulation(cond, w_mod, b_mod):
    n_layers = w_mod.shape[0]
    tn = 1536
    return pl.pallas_call(
        _mod_kernel,
        grid=(n_layers, 6 * D_MODEL // tn),
        in_specs=[
            pl.BlockSpec((MOD_ROWS, D_MODEL), lambda l, j: (0, 0)),
            pl.BlockSpec((None, D_MODEL, tn), lambda l, j: (l, 0, j)),
            pl.BlockSpec((None, 1, tn), lambda l, j: (l, 0, j)),
        ],
        out_specs=pl.BlockSpec((None, MOD_ROWS, tn), lambda l, j: (l, 0, j)),
        out_shape=jax.ShapeDtypeStruct((n_layers, MOD_ROWS, 6 * D_MODEL), F32),
        compiler_params=_cparams(("parallel", "parallel")),
        name="modulation",
    )(cond, w_mod, b_mod.reshape(n_layers, 1, 6 * D_MODEL))


def _embed_kernel(n_ctx_tiles, xp_ref, xs_ref, pos_ref, o_ref):
    i = pl.program_id(0)

    @pl.when(i < n_ctx_tiles)
    def _():
        o_ref[...] = xp_ref[...]

    @pl.when(i >= n_ctx_tiles)
    def _():
        o_ref[...] = xs_ref[...] + pos_ref[...]


def _embed(xp, xs, pos, lat_tiles):
    tm = IO_TILE
    n_ctx_tiles = xp.shape[0] // tm
    n_lat_tiles = xs.shape[0] // tm
    n_tiles = n_ctx_tiles + n_lat_tiles
    return pl.pallas_call(
        functools.partial(_embed_kernel, n_ctx_tiles),
        grid=(n_tiles,),
        in_specs=[
            pl.BlockSpec((tm, D_MODEL), lambda i: (jnp.minimum(i, n_ctx_tiles - 1), 0)),
            pl.BlockSpec((tm, D_MODEL), lambda i: (jnp.maximum(i - n_ctx_tiles, 0), 0)),
            pl.BlockSpec((tm, D_MODEL), lambda i: (jnp.maximum(i - n_ctx_tiles, 0) % lat_tiles, 0)),
        ],
        out_specs=pl.BlockSpec((tm, D_MODEL), lambda i: (i, 0)),
        out_shape=jax.ShapeDtypeStruct((n_tiles * tm, D_MODEL), F32),
        compiler_params=_cparams(("parallel",)),
        name="embed",
    )(xp, xs, pos)


def _inproj_kernel(x_ref, mod_ref, g_ref, wa_ref, ba_ref, wm_ref, bm_ref, zr_ref, zm_ref):
    h = _norm_mod(x_ref[...], g_ref[...], mod_ref[:, 0:D_MODEL], mod_ref[:, D_MODEL:2 * D_MODEL])
    hb = h.astype(BF16)
    zr_ref[...] = _dot(hb, wa_ref[...]) + ba_ref[...]
    zm_ref[...] = (_dot(hb, wm_ref[...]) + bm_ref[...]).astype(BF16)


def _in_projection(x, mod_l, mod_row, norm_g, wa, ba, wm, bm):
    n = x.shape[0]
    tm = PROJ_TILE
    return pl.pallas_call(
        _inproj_kernel,
        grid=(n // tm,),
        in_specs=[
            pl.BlockSpec((tm, D_MODEL), lambda i: (i, 0)),
            pl.BlockSpec((None, 1, 6 * D_MODEL), lambda i: (mod_row(i), 0, 0)),
            _const_spec((1, D_MODEL)),
            _const_spec((D_MODEL, RW_COLS)),
            _const_spec((1, RW_COLS)),
            _const_spec((D_MODEL, MX_COLS)),
            _const_spec((1, MX_COLS)),
        ],
        out_specs=[
            pl.BlockSpec((tm, RW_COLS), lambda i: (i, 0)),
            pl.BlockSpec((tm, MX_COLS), lambda i: (i, 0)),
        ],
        out_shape=[
            jax.ShapeDtypeStruct((n, RW_COLS), F32),
            jax.ShapeDtypeStruct((n, MX_COLS), BF16),
        ],
        compiler_params=_cparams(("parallel",)),
        name="in_projection",
    )(x, mod_l, norm_g, wa, ba, wm, bm)


P_W0, P_A0, P_KK, P_KA, P_RK, P_GNG, P_GNB = 0, 2, 4, 5, 6, 7, 8
P_ROWS = 16


def _rep(xb, reps, mask):
    return jnp.concatenate([xb] * reps, axis=0) * mask


def _rwkv_kernel(t_len, y_all_ref, r_ref, k_ref, v_ref, lr_ref, s0_ref, par_ref, wup_ref, aup_ref, gup_ref,
                 bd_ref, cm_ref, tri_ref, eyec_ref, out_ref, sfin_ref, y_scr, st_scr):
    del y_all_ref
    c = SCAN_CHUNK
    n_chunks = t_len // c
    n_trips = n_chunks // CHUNKS_PER_TRIP
    lr_k = RWKV_DECAY_RANK + RWKV_ICLR_RANK
    probs = [(d, sl, g) for sl in range(CHUNKS_PER_TRIP) for d in (0, 1) for g in range(N_HG)]
    n_sq = int(math.log2(c))

    def lanes(g):
        return slice(g * HG_LANES, (g + 1) * HG_LANES)

    def rep(xb):
        return _rep(xb, HEADS_PER_HG, bd_ref[...])

    def seg_sums(xs, passes):
        n = xs[0].shape[0]
        if passes == 1:
            out = _dot(jnp.concatenate([x.astype(BF16) for x in xs], axis=0), bd_ref[...])
            return [out[i * n:(i + 1) * n] for i in range(len(xs))]
        parts = []
        for x in xs:
            parts.extend(_split2(x))
        out = _dot(jnp.concatenate(parts, axis=0), bd_ref[...])
        return [out[2 * i * n:(2 * i + 1) * n] + out[(2 * i + 1) * n:(2 * i + 2) * n] for i in range(len(xs))]

    def fold(x_bd):
        n = x_bd.shape[0] // HEADS_PER_HG
        return (x_bd[0:n] + x_bd[n:2 * n]) + (x_bd[2 * n:3 * n] + x_bd[3 * n:4 * n])

    def par(g, row):
        return par_ref[g, row:row + 1, :]

    st_scr[...] = s0_ref[...]

    def trip(j, first_touch):
        rows = {}
        for sl in range(CHUNKS_PER_TRIP):
            rows[0, sl] = pl.ds(pl.multiple_of((CHUNKS_PER_TRIP * j + sl) * c, c), c)
            rows[1, sl] = pl.ds(pl.multiple_of((n_chunks - 1 - CHUNKS_PER_TRIP * j - sl) * c, c), c)
        lr = {key: lr_ref[rw, 0:lr_k] for key, rw in rows.items()}
        th = {key: jnp.tanh(x).astype(BF16) for key, x in lr.items()}
        lrb = {key: x.astype(BF16) for key, x in lr.items()}
        ss = [dict(r=r_ref[rows[d, sl], lanes(g)], k=k_ref[rows[d, sl], lanes(g)], v=v_ref[rows[d, sl], lanes(g)])
              for d, sl, g in probs]

        for (d, sl, g), s in zip(probs, ss):
            s["w_pre"] = par(g, P_W0 + d) + _dot(th[d, sl], wup_ref[d, g])
            s["a_pre"] = par(g, P_A0 + d) + _dot(lrb[d, sl], aup_ref[d, g])
            s["kkr"] = s["k"] * par(g, P_KK)
        for s, ssq in zip(ss, seg_sums([s["kkr"] * s["kkr"] for s in ss], 1)):
            s["ssq"] = ssq
        for (d, sl, g), s in zip(probs, ss):
            s["lw"] = -DECAY_SCALE * _sigmoid(s["w_pre"])
            s["cum"] = _dot_exact_lhs(tri_ref[d], s["lw"])
        for (d, sl, g), s in zip(probs, ss):
            a = _sigmoid(s["a_pre"])
            kk = s["kkr"] / jnp.maximum(jnp.sqrt(s["ssq"]), 1e-12)
            kd = s["k"] * (1.0 + (a - 1.0) * par(g, P_KA))
            cum = s["cum"]
            e_incl = jnp.exp(cum)
            e_neg = jnp.exp(-cum)
            gam = e_incl[c - 1:c, :] if d == 0 else e_incl[0:1, :]
            s["at"] = (-kk * jnp.exp(cum - s["lw"])).astype(BF16)
            s["rt"] = s["r"] * e_incl
            bt = kk * a * e_neg
            kt = kd * e_neg
            s["vb"] = s["v"].astype(BF16)
            s["bonus"] = s["r"] * kd * par(g, P_RK)
            s["gam_rows"] = _dot_exact_rhs(eyec_ref[...] * gam, bd_ref[...])
            tr = jnp.concatenate([bt, kt], axis=0).T
            swapped = pltpu.roll(tr, c, axis=1)
            first_half = lax.broadcasted_iota(jnp.int32, tr.shape, 1) < c
            bt_t = jnp.where(first_half, tr, swapped).astype(BF16)
            kt_t = jnp.where(first_half, swapped, tr).astype(BF16)
            s["bt_bd"] = jnp.concatenate([bt_t, bt_t], axis=1) * bd_ref[...]
            s["kt_bd"] = jnp.concatenate([kt_t, kt_t], axis=1) * bd_ref[...]
            lhs2 = jnp.concatenate([s["at"], s["rt"].astype(BF16)], axis=0)
            s["m_b"] = _dot(lhs2, s["bt_bd"])
            s["m_k"] = _dot(lhs2, s["kt_bd"])
        for s, bsum in zip(ss, seg_sums([s["bonus"] for s in ss], 1)):
            s["bsum"] = bsum
        for (d, sl, g), s in zip(probs, ss):
            causal = cm_ref[d]
            m_b = s["m_b"] * causal
            s["x"] = m_b[0:c]
            s["a_rb"] = m_b[c:2 * c].astype(BF16)
            s["t"] = eyec_ref[...] + s["x"]
            s["bt_c"] = fold(s["bt_bd"])
            av = _dot(jnp.concatenate([(s["m_k"] * causal).astype(BF16), fold(s["kt_bd"])], axis=0), rep(s["vb"]))
            s["g1"] = av[0:c].astype(BF16)
            s["y0"] = av[c:2 * c] + s["bsum"] * s["v"]
            s["z"] = av[2 * c:2 * c + RWKV_HEAD_DIM]

        for step in range(1, n_sq):
            for s in ss:
                x_bd = rep(s["x"].astype(BF16))
                if step == 1:
                    s["x"] = _dot(s["x"].astype(BF16), x_bd)
                else:
                    both = _dot(jnp.concatenate([s["x"], s["t"]], axis=0).astype(BF16), x_bd)
                    s["x"], s["t"] = both[0:c], s["t"] + both[c:2 * c]
        for s in ss:
            s["t"] = (s["t"] + _dot(s["t"].astype(BF16), rep(s["x"].astype(BF16)))).astype(BF16)

        for s in ss:
            s["w"] = _dot(s["t"], rep(s["at"])).astype(BF16)
            s["u0"] = _dot(s["t"], rep(s["g1"])).astype(BF16)
        for s in ss:
            lhs = jnp.concatenate([s["a_rb"], s["bt_c"]], axis=0)
            with_w = _dot(lhs, rep(s["w"]))
            with_u0 = _dot(lhs, rep(s["u0"]))
            s["q"] = (s["rt"] + with_w[0:c]).astype(BF16)
            s["p"] = (eyec_ref[...] + with_w[c:c + RWKV_HEAD_DIM]).astype(BF16)
            s["y0"] = s["y0"] + with_u0[0:c]
            s["z"] = s["z"] + with_u0[c:c + RWKV_HEAD_DIM]

        for (d, sl, g), s in zip(probs, ss):
            both = _dot(jnp.concatenate([s["p"], s["q"]], axis=0), rep(st_scr[g, d].astype(BF16)))
            st_scr[g, d] = s["gam_rows"] * (both[0:RWKV_HEAD_DIM] + s["z"])
            y = both[RWKV_HEAD_DIM:RWKV_HEAD_DIM + c] + s["y0"]
            if first_touch:
                y_scr[rows[d, sl], lanes(g)] = y
            else:
                y_scr[rows[d, sl], lanes(g)] += y

    half = n_trips // 2
    lax.fori_loop(0, half, lambda j, carry: (trip(j, True), carry)[1], 0)
    lax.fori_loop(half, n_trips, lambda j, carry: (trip(j, False), carry)[1], 0)
    sfin_ref[...] = st_scr[...]

    inv_n = 1.0 / RWKV_HEAD_DIM

    def finish(j, carry):
        rows = pl.ds(pl.multiple_of(j * TILE, TILE), TILE)
        sg = _sigmoid(lr_ref[rows, lr_k:HG_LANES]).astype(BF16)
        ys = [y_scr[rows, lanes(g)] for g in range(N_HG)]
        mus = [m * inv_n for m in seg_sums(ys, 2)]
        ycs = [y - mu for y, mu in zip(ys, mus)]
        variances = [v * inv_n for v in seg_sums([yc * yc for yc in ycs], 2)]
        for g in range(N_HG):
            yn = ycs[g] * lax.rsqrt(variances[g] + RWKV_GN_EPS) * par(g, P_GNG) + par(g, P_GNB)
            out_ref[rows, lanes(g)] = (yn * _dot(sg, gup_ref[g])).astype(BF16)
        return carry

    lax.fori_loop(0, t_len // TILE, finish, 0)


def _rwkv_mixer(zr, s0, n_seq, t_len, row_off, y_all, par, wup, aup, gup, consts):
    lr_block = 3 * RWKV_WIDTH // HG_LANES
    state_spec = pl.BlockSpec((None, N_HG, 2, RWKV_HEAD_DIM, HG_LANES), lambda b: (b, 0, 0, 0, 0))
    weights = (par, wup, aup, gup)
    return pl.pallas_call(
        functools.partial(_rwkv_kernel, t_len),
        grid=(n_seq,),
        in_specs=[
            pl.BlockSpec(memory_space=pl.ANY),
            pl.BlockSpec((t_len, RWKV_WIDTH), lambda b: (b + row_off, 0)),
            pl.BlockSpec((t_len, RWKV_WIDTH), lambda b: (b + row_off, 1)),
            pl.BlockSpec((t_len, RWKV_WIDTH), lambda b: (b + row_off, 2)),
            pl.BlockSpec((t_len, HG_LANES), lambda b: (b + row_off, lr_block)),
            state_spec,
            *[_const_spec(a.shape) for a in weights],
            *[_const_spec(a.shape) for a in consts],
        ],
        out_specs=[pl.BlockSpec((t_len, RWKV_WIDTH), lambda b: (b + row_off, 0)), state_spec],
        out_shape=[
            jax.ShapeDtypeStruct((zr.shape[0], RWKV_WIDTH), BF16),
            jax.ShapeDtypeStruct((n_seq, N_HG, 2, RWKV_HEAD_DIM, HG_LANES), F32),
        ],
        scratch_shapes=[
            pltpu.VMEM((t_len, RWKV_WIDTH), F32),
            pltpu.VMEM((N_HG, 2, RWKV_HEAD_DIM, HG_LANES), F32),
        ],
        input_output_aliases={0: 0},
        compiler_params=_cparams(("parallel",), RWKV_VMEM_LIMIT),
        name="rwkv_mixer",
    )(y_all, zr, zr, zr, zr, s0, *weights, *consts)


def _rwkv_consts():
    c = SCAN_CHUNK
    ri = jnp.arange(HG_LANES)[:, None]
    ci = jnp.arange(HG_LANES)[None, :]
    bd = (ri // RWKV_HEAD_DIM == ci // RWKV_HEAD_DIM).astype(BF16)
    t = jnp.arange(c)[:, None]
    i = (jnp.arange(HEADS_PER_HG * c) % c)[None, :]
    cm = jnp.stack([jnp.concatenate([i < t, i <= t]), jnp.concatenate([i > t, i >= t])]).astype(F32)
    ti = jnp.arange(c)[None, :]
    tri = jnp.stack([ti <= t, ti >= t]).astype(BF16)
    eyec = (i == t).astype(F32)
    return bd, cm, tri, eyec


def _states_to_kernel(s):
    b = s.shape[0]
    st = s.reshape(b, 2, N_HG, HEADS_PER_HG, RWKV_HEAD_DIM, RWKV_HEAD_DIM)
    return jnp.transpose(st, (0, 2, 1, 5, 3, 4)).reshape(b, N_HG, 2, RWKV_HEAD_DIM, HG_LANES)


def _states_from_kernel(sk):
    b = sk.shape[0]
    st = sk.reshape(b, N_HG, 2, RWKV_HEAD_DIM, HEADS_PER_HG, RWKV_HEAD_DIM)
    return jnp.transpose(st, (0, 2, 1, 4, 5, 3)).reshape(b, 2, RWKV_HEADS, RWKV_HEAD_DIM, RWKV_HEAD_DIM)


def _pool_tables():
    wlen = TILE + 2 * HALO
    t_loc = jnp.arange(TILE)[:, None]
    s_rel = jnp.arange(wlen)[None, :] - HALO
    lane_g = jnp.arange(POOL_WIDTH)[None, :] // (POOL_WIDTH // POOL_GROUPS)
    bands, counts = [], []
    for first in (0, 1):
        for last in (0, 1):
            lo_seq = 0 if first else -HALO
            hi_seq = TILE if last else TILE + HALO
            band_groups = []
            cnt = jnp.zeros((TILE, POOL_WIDTH), F32)
            for gi, win in enumerate(POOL_WINDOWS):
                lo = jnp.maximum(t_loc - win // 2, lo_seq)
                hi = jnp.minimum(t_loc + win - win // 2, hi_seq)
                band_groups.append((s_rel >= lo) & (s_rel < hi))
                cnt = jnp.where(lane_g == gi, (hi - lo).astype(F32), cnt)
            bands.append(jnp.concatenate(band_groups, axis=1))
            counts.append(cnt)
    return jnp.stack(bands).astype(BF16), jnp.stack(counts)


def _mixers_kernel(ctx_tiles, seq_tiles_ctx, seq_tiles_lat, zc_ref, zp_ref, zn_ref, dw_ref, vec_ref, wcat_ref,
                   bs_ref, pw_ref, sgm_ref, plm_ref, band_ref, cnt_ref, out_ref, win_scr, shift_scr):
    i = pl.program_id(0)
    j_ctx = i % seq_tiles_ctx
    j_lat = jnp.maximum(i - ctx_tiles, 0) % seq_tiles_lat
    is_ctx = i < ctx_tiles
    first = jnp.where(is_ctx, j_ctx == 0, j_lat == 0)
    last = jnp.where(is_ctx, j_ctx == seq_tiles_ctx - 1, j_lat == seq_tiles_lat - 1)
    keep_prev = jnp.where(first, 0.0, 1.0)
    keep_next = jnp.where(last, 0.0, 1.0)
    case = 2 * first.astype(jnp.int32) + last.astype(jnp.int32)

    cw = CONV_WIDTH
    conv_b, cln_g, cln_b = vec_ref[0:1, :], vec_ref[1:2, :], vec_ref[2:3, :]
    sln_g, sln_b, pool_scale = vec_ref[3:4, :], vec_ref[4:5, :], vec_ref[5:6, :]

    glu = lambda z_ref: z_ref[:, 0:cw].astype(F32) * _sigmoid(z_ref[:, cw:2 * cw].astype(F32))
    win_scr[0:HALO, :] = glu(zp_ref) * keep_prev
    win_scr[HALO:HALO + TILE, :] = glu(zc_ref)
    win_scr[HALO + TILE:2 * HALO + TILE, :] = glu(zn_ref) * keep_next
    for r in range(1, SUBLANES):
        shift_scr[r] = win_scr[pl.ds(r, SHIFTED_ROWS), :]
    acc = jnp.zeros((TILE, cw), F32) + conv_b
    pad = CONV_KERNEL // 2
    for j in range(CONV_KERNEL):
        q, r = divmod(HALO - pad + j, SUBLANES)
        rows = pl.ds(q * SUBLANES, TILE)
        tap = win_scr[rows, :] if r == 0 else shift_scr[r, rows, :]
        acc = acc + tap * dw_ref[j:j + 1, :]
    out_ref[:, 0:cw] = _silu(_layer_norm(acc, cln_g, cln_b)).astype(BF16)

    su = zc_ref[:, 2 * cw:2 * cw + SGU_WIDTH].astype(F32)
    sv = zc_ref[:, 2 * cw + SGU_WIDTH:2 * cw + 2 * SGU_WIDTH].astype(F32)
    vn = _layer_norm(sv, sln_g, sln_b).astype(BF16)
    sgm = sgm_ref[...]
    for ch in range(TILE // SGU_CHUNK):
        rows = slice(ch * SGU_CHUNK, (ch + 1) * SGU_CHUNK)
        s = _dot(wcat_ref[...], _rep(vn[rows], SGU_GROUPS, sgm)) + bs_ref[...]
        out_ref[rows, cw:cw + SGU_WIDTH] = (su[rows] * s).astype(BF16)

    zoff = 2 * cw + 2 * SGU_WIDTH
    zcur = zc_ref[:, zoff:zoff + POOL_WIDTH]
    zw = jnp.concatenate([zp_ref[:, zoff:zoff + POOL_WIDTH], zcur, zn_ref[:, zoff:zoff + POOL_WIDTH]], axis=0)
    psum = _dot(band_ref[case], _rep(zw, POOL_GROUPS, plm_ref[...]))
    p = psum / cnt_ref[case] - zcur.astype(F32)
    pooled = _dot(p.astype(BF16), pw_ref[...]) * pool_scale
    out_ref[:, cw + SGU_WIDTH:cw + SGU_WIDTH + POOL_WIDTH] = pooled.astype(BF16)


def _mixers(zm, ctx_tiles, seq_tiles_ctx, seq_tiles_lat, *tables):
    n = zm.shape[0]
    n_tiles = n // TILE
    per = TILE // HALO
    n_halo = n // HALO
    return pl.pallas_call(
        functools.partial(_mixers_kernel, ctx_tiles, seq_tiles_ctx, seq_tiles_lat),
        grid=(n_tiles,),
        in_specs=[
            pl.BlockSpec((TILE, MX_COLS), lambda i: (i, 0)),
            pl.BlockSpec((HALO, MX_COLS), lambda i: (jnp.maximum(i * per - 1, 0), 0)),
            pl.BlockSpec((HALO, MX_COLS), lambda i: (jnp.minimum((i + 1) * per, n_halo - 1), 0)),
            *[_const_spec(a.shape) for a in tables],
        ],
        out_specs=pl.BlockSpec((TILE, MX_OUT), lambda i: (i, 0)),
        out_shape=jax.ShapeDtypeStruct((n, MX_OUT), BF16),
        scratch_shapes=[
            pltpu.VMEM((TILE + 2 * HALO, CONV_WIDTH), F32),
            pltpu.VMEM((SUBLANES, SHIFTED_ROWS, CONV_WIDTH), F32),
        ],
        compiler_params=_cparams(("parallel",)),
        name="mixers",
    )(zm, zm, zm, *tables)


def _route(logits_t):
    m = jnp.max(logits_t, axis=0, keepdims=True)
    e = jnp.exp(logits_t - m)
    p = e / jnp.sum(e, axis=0, keepdims=True)
    best_score = None
    best = None
    for g in range(N_EXPERT_GROUPS):
        rows = [p[g * EXPERTS_PER_GROUP + q:g * EXPERTS_PER_GROUP + q + 1] for q in range(EXPERTS_PER_GROUP)]
        score = None
        for a in range(EXPERTS_PER_GROUP):
            for b in range(a + 1, EXPERTS_PER_GROUP):
                pair = rows[a] + rows[b]
                score = pair if score is None else jnp.maximum(score, pair)
        if g == 0:
            best_score, best = score, jnp.zeros(score.shape, jnp.int32)
        else:
            upd = score > best_score
            best = jnp.where(upd, g, best)
            best_score = jnp.where(upd, score, best_score)
    eidx = lax.broadcasted_iota(jnp.int32, p.shape, 0)
    neg = -jnp.inf
    masked = jnp.where(eidx // EXPERTS_PER_GROUP == best, p, neg)
    m1 = jnp.max(masked, axis=0, keepdims=True)
    i1 = jnp.min(jnp.where(masked == m1, eidx, N_EXPERTS), axis=0, keepdims=True)
    masked2 = jnp.where(eidx == i1, neg, masked)
    m2 = jnp.max(masked2, axis=0, keepdims=True)
    i2 = jnp.min(jnp.where(masked2 == m2, eidx, N_EXPERTS), axis=0, keepdims=True)
    tot = m1 + m2
    return jnp.concatenate([i1, i2], axis=0), jnp.concatenate([m1 / tot, m2 / tot], axis=0)


def _merge_kernel(x_ref, mod_ref, n1_ref, n2_ref, ya_ref, mx_ref, wg_ref, bg_ref, wro_ref, wco_ref, wso_ref,
                  wpo_ref, wout_ref, rw_ref, rb_ref, x1_ref, h2_ref, ri_ref, rwt_ref):
    d = D_MODEL
    x = x_ref[...]
    hb = _norm_mod(x, n1_ref[...], mod_ref[:, 0:d], mod_ref[:, d:2 * d]).astype(BF16)
    cw = CONV_WIDTH
    branches = (
        (ya_ref[...], wro_ref),
        (mx_ref[:, 0:cw], wco_ref),
        (mx_ref[:, cw:cw + SGU_WIDTH], wso_ref),
        (mx_ref[:, cw + SGU_WIDTH:MX_OUT], wpo_ref),
    )
    merged = None
    for bi, (y_in, w_ref) in enumerate(branches):
        zg = _dot(hb, wg_ref[:, bi * d:(bi + 1) * d]) + bg_ref[:, bi * d:(bi + 1) * d]
        term = _sigmoid(zg) * _dot(y_in, w_ref[...])
        merged = term if merged is None else merged + term
    x1 = x + mod_ref[:, 2 * d:3 * d] * _dot(merged.astype(BF16), wout_ref[...])
    x1_ref[...] = x1
    h2 = _norm_mod(x1, n2_ref[...], mod_ref[:, 3 * d:4 * d], mod_ref[:, 4 * d:5 * d])
    h2_ref[...] = h2
    logits = _dot3(h2, rw_ref[...])
    ri_ref[...], rwt_ref[...] = _route(logits.T[0:N_EXPERTS] + rb_ref[...])


def _merge(x, mod_l, mod_row, n1, n2, ya, mx, wg, bg, wro, wco, wso, wpo, wout, rw_t, rb):
    n = x.shape[0]
    tm = TILE
    consts = (n1, n2)
    weights = (wg, bg, wro, wco, wso, wpo, wout, rw_t, rb)
    return pl.pallas_call(
        _merge_kernel,
        grid=(n // tm,),
        in_specs=[
            pl.BlockSpec((tm, D_MODEL), lambda i: (i, 0)),
            pl.BlockSpec((None, 1, 6 * D_MODEL), lambda i: (mod_row(i), 0, 0)),
            *[_const_spec(a.shape) for a in consts],
            pl.BlockSpec((tm, RWKV_WIDTH), lambda i: (i, 0)),
            pl.BlockSpec((tm, MX_OUT), lambda i: (i, 0)),
            *[_const_spec(a.shape) for a in weights],
        ],
        out_specs=[
            pl.BlockSpec((tm, D_MODEL), lambda i: (i, 0)),
            pl.BlockSpec((tm, D_MODEL), lambda i: (i, 0)),
            pl.BlockSpec((TOP_K, tm), lambda i: (0, i)),
            pl.BlockSpec((TOP_K, tm), lambda i: (0, i)),
        ],
        out_shape=[
            jax.ShapeDtypeStruct((n, D_MODEL), F32),
            jax.ShapeDtypeStruct((n, D_MODEL), F32),
            jax.ShapeDtypeStruct((TOP_K, n), jnp.int32),
            jax.ShapeDtypeStruct((TOP_K, n), F32),
        ],
        compiler_params=_cparams(("parallel",)),
        name="merge_router",
    )(x, mod_l, n1, n2, ya, mx, *weights)


PAIRS_PER_GROUP = EXPERTS_PER_GROUP * (EXPERTS_PER_GROUP - 1) // 2
N_PAIR_CLASSES = N_EXPERT_GROUPS * PAIRS_PER_GROUP
ROUTED_TILE = 256


def _dispatch_plan(route_i, route_w):
    n = route_i.shape[1]
    tm = ROUTED_TILE
    n_tiles = n // tm + N_PAIR_CLASSES
    i1, i2 = route_i[0], route_i[1]
    first_lower = i1 < i2
    lo, hi = jnp.minimum(i1, i2), jnp.maximum(i1, i2)
    w_lo = jnp.where(first_lower, route_w[0], route_w[1])
    w_hi = jnp.where(first_lower, route_w[1], route_w[0])
    a, b = lo % EXPERTS_PER_GROUP, hi % EXPERTS_PER_GROUP
    assert EXPERTS_PER_GROUP == 4
    pair = a * (2 * EXPERTS_PER_GROUP - 1 - a) // 2 + (b - a - 1)
    pair = jnp.where(pair == 3, 4, jnp.where(pair == 4, 3, pair))
    cls = (lo // EXPERTS_PER_GROUP) * PAIRS_PER_GROUP + pair
    onehot = (cls[:, None] == jnp.arange(N_PAIR_CLASSES)[None, :]).astype(jnp.int32)
    counts = jnp.sum(onehot, axis=0)
    rank = jnp.sum((jnp.cumsum(onehot, axis=0) - onehot) * onehot, axis=1)
    padded = (counts + tm - 1) // tm * tm
    ends = jnp.cumsum(padded)
    dest = (ends - padded)[cls] + rank
    tile_row0 = jnp.arange(n_tiles, dtype=jnp.int32) * tm
    tile_valid = (tile_row0 < ends[-1]).astype(jnp.int32)
    tile_cls = jnp.minimum(jnp.searchsorted(ends, tile_row0, side="right"), N_PAIR_CLASSES - 1)
    stride = 1 << 15
    assert n + tm <= stride and N_PAIR_CLASSES * stride < (1 << 30)
    unused = jnp.iinfo(jnp.int32).max
    j = jnp.arange(tm, dtype=jnp.int32)[None, :]
    pad_keys = jnp.where(j < (padded - counts)[:, None], jnp.arange(N_PAIR_CLASSES)[:, None] * stride + n + j, unused)
    keys = jnp.concatenate([cls * stride + jnp.arange(n, dtype=jnp.int32), pad_keys.reshape(-1).astype(jnp.int32)])
    no_weight = jnp.zeros((N_PAIR_CLASSES * tm,), F32)
    keys, w_lo_sorted, w_hi_sorted = lax.sort(
        (keys, jnp.concatenate([w_lo, no_weight]), jnp.concatenate([w_hi, no_weight])), num_keys=1)
    token = keys % stride
    src = jnp.where((token < n) & (keys != unused), token, 0)
    w_sorted = jnp.stack([w_lo_sorted, w_hi_sorted], axis=1)
    cls_ids = jnp.arange(N_PAIR_CLASSES)
    pa = jnp.array([0, 0, 0, 1, 1, 2], jnp.int32)
    pb = jnp.array([1, 2, 3, 3, 2, 3], jnp.int32)
    cls_lo = (cls_ids // PAIRS_PER_GROUP) * EXPERTS_PER_GROUP + pa[cls_ids % PAIRS_PER_GROUP]
    cls_hi = (cls_ids // PAIRS_PER_GROUP) * EXPERTS_PER_GROUP + pb[cls_ids % PAIRS_PER_GROUP]
    last_cls = tile_cls[jnp.maximum(ends[-1] // tm - 1, 0)]
    tile_cls = jnp.where(tile_valid == 1, tile_cls, last_cls)
    tiles_used = (ends[-1:] // tm).astype(jnp.int32)
    return (dest.astype(jnp.int32), src, w_sorted, cls_lo[tile_cls].astype(jnp.int32),
            cls_hi[tile_cls].astype(jnp.int32), tiles_used)


def _start_row_gather(idx_ref, idx0, rows_hbm, buf, sem, n_rows):
    for r in range(n_rows):
        copy = pltpu.make_async_copy(rows_hbm.at[pl.ds(idx_ref[idx0 + r], 1)], buf.at[pl.ds(r, 1)], sem)
        copy.start(priority=r % 2)


def _wait_row_gather(rows_hbm, buf, sem, n_rows):
    pltpu.make_async_copy(rows_hbm.at[pl.ds(0, n_rows)], buf, sem).wait()


def _pipelined_gather(idx_ref, rows_hbm, bufs, sems, n_rows):
    i = pl.program_id(0)
    slot = i % 2

    @pl.when(i == 0)
    def _():
        _start_row_gather(idx_ref, 0, rows_hbm, bufs.at[0], sems.at[0], n_rows)

    @pl.when(i + 1 < pl.num_programs(0))
    def _():
        _start_row_gather(idx_ref, (i + 1) * n_rows, rows_hbm, bufs.at[1 - slot], sems.at[1 - slot], n_rows)

    _wait_row_gather(rows_hbm, bufs.at[slot], sems.at[slot], n_rows)
    return slot


EXPERT_GATHER_DEPTH = 2


def _experts_kernel(lo_ref, hi_ref, used_ref, src_ref, w_ref, h_hbm, wg_lo, wu_lo, wd_lo, wg_hi, wu_hi, wd_hi,
                    o_ref, hbuf, sems, up_scr, down_scr):
    tm = ROUTED_TILE
    n_slots = EXPERT_GATHER_DEPTH + 1
    i = pl.program_id(0)
    n = used_ref[0]
    slot = i % n_slots

    @pl.when(i == 0)
    def _():
        for t in range(EXPERT_GATHER_DEPTH):
            @pl.when(t < n)
            def _():
                _start_row_gather(src_ref, t * tm, h_hbm, hbuf.at[t], sems.at[t], tm)

    def tile(prefetch):
        _wait_row_gather(h_hbm, hbuf.at[slot], sems.at[slot], tm)
        ahead = (i + EXPERT_GATHER_DEPTH) % n_slots
        rows = iter(range(tm))

        def request(count):
            if not prefetch:
                return
            for r in (next(rows) for _ in range(count)):
                row = src_ref[(i + EXPERT_GATHER_DEPTH) * tm + r]
                pltpu.make_async_copy(h_hbm.at[pl.ds(row, 1)], hbuf.at[ahead, pl.ds(r, 1)], sems.at[ahead]).start()

        hb = hbuf[slot].astype(BF16)
        per_dot = tm // 6

        def expert(which, expert_of_tile, wg, wu, wd):
            changed = (i == 0) | (expert_of_tile[i] != expert_of_tile[jnp.maximum(i - 1, 0)])

            @pl.when(changed)
            def _():
                up_scr[2 * which] = wg[...].astype(BF16)
                up_scr[2 * which + 1] = wu[...].astype(BF16)
                down_scr[which] = wd[...].astype(BF16)

            request(per_dot)
            gate = _dot(hb, up_scr[2 * which])
            request(per_dot)
            up = _dot(hb, up_scr[2 * which + 1])
            request(per_dot)
            return _dot((_silu(gate) * up).astype(BF16), down_scr[which])

        w = w_ref[...]
        out = w[:, 0:1] * expert(0, lo_ref, wg_lo, wu_lo, wd_lo) + w[:, 1:2] * expert(1, hi_ref, wg_hi, wu_hi, wd_hi)
        request(tm - 6 * per_dot)
        o_ref[...] = out

    @pl.when(i + EXPERT_GATHER_DEPTH < n)
    def _():
        tile(True)

    @pl.when((i < n) & (i + EXPERT_GATHER_DEPTH >= n))
    def _():
        tile(False)

    @pl.when(i >= n)
    def _():
        o_ref[...] = jnp.zeros(o_ref.shape, F32)


def _experts(h2, src, w_sorted, tile_lo, tile_hi, tiles_used, layer, wg, wu, wd):
    tm = ROUTED_TILE
    n_tiles = tile_lo.shape[0]
    n_slots = EXPERT_GATHER_DEPTH + 1
    up_spec = lambda sel: pl.BlockSpec((None, None, D_MODEL, D_EXPERT), lambda i, lo, hi, u, s: (layer, sel(lo, hi)[i], 0, 0))
    down_spec = lambda sel: pl.BlockSpec((None, None, D_EXPERT, D_MODEL), lambda i, lo, hi, u, s: (layer, sel(lo, hi)[i], 0, 0))
    pick_lo = lambda lo, hi: lo
    pick_hi = lambda lo, hi: hi
    return pl.pallas_call(
        _experts_kernel,
        grid_spec=pltpu.PrefetchScalarGridSpec(
            num_scalar_prefetch=4,
            grid=(n_tiles,),
            in_specs=[
                pl.BlockSpec((tm, TOP_K), lambda i, *_: (i, 0)),
                pl.BlockSpec(memory_space=pl.ANY),
                up_spec(pick_lo), up_spec(pick_lo), down_spec(pick_lo),
                up_spec(pick_hi), up_spec(pick_hi), down_spec(pick_hi),
            ],
            out_specs=pl.BlockSpec((tm, D_MODEL), lambda i, *_: (i, 0)),
            scratch_shapes=[
                pltpu.VMEM((n_slots, tm, D_MODEL), F32),
                pltpu.SemaphoreType.DMA((n_slots,)),
                pltpu.VMEM((2 * TOP_K, D_MODEL, D_EXPERT), BF16),
                pltpu.VMEM((TOP_K, D_EXPERT, D_MODEL), BF16),
            ],
        ),
        out_shape=jax.ShapeDtypeStruct((n_tiles * tm, D_MODEL), F32),
        compiler_params=_cparams(("arbitrary",)),
        name="experts",
    )(tile_lo, tile_hi, tiles_used, src, w_sorted, h2, wg, wu, wd, wg, wu, wd)


def _combine_kernel(dest_ref, x_ref, mod_ref, ff_hbm, o_ref, fbuf, sems):
    slot = _pipelined_gather(dest_ref, ff_hbm, fbuf, sems, ROUTED_TILE)
    o_ref[...] = x_ref[...] + mod_ref[:, 5 * D_MODEL:6 * D_MODEL] * fbuf[slot]


def _combine(ff_sorted, dest, x1, mod_l, mod_row):
    n = x1.shape[0]
    tm = ROUTED_TILE
    return pl.pallas_call(
        _combine_kernel,
        grid_spec=pltpu.PrefetchScalarGridSpec(
            num_scalar_prefetch=1,
            grid=(n // tm,),
            in_specs=[
                pl.BlockSpec((tm, D_MODEL), lambda i, d: (i, 0)),
                pl.BlockSpec((None, 1, 6 * D_MODEL), lambda i, d: (mod_row(i), 0, 0)),
                pl.BlockSpec(memory_space=pl.ANY),
            ],
            out_specs=pl.BlockSpec((tm, D_MODEL), lambda i, d: (i, 0)),
            scratch_shapes=[pltpu.VMEM((2, tm, D_MODEL), F32), pltpu.SemaphoreType.DMA((2,))],
        ),
        out_shape=jax.ShapeDtypeStruct((n, D_MODEL), F32),
        compiler_params=_cparams(("arbitrary",)),
        name="moe_combine",
    )(dest, x1, mod_l, ff_sorted)


def _final_norm_kernel(x_ref, g_ref, o_ref):
    x = x_ref[...]
    o_ref[...] = x * lax.rsqrt(jnp.mean(x * x, axis=-1, keepdims=True) + NORM_EPS) * g_ref[...]


def _final_norm(x, g, tile0, n_tiles):
    return pl.pallas_call(
        _final_norm_kernel,
        grid=(n_tiles,),
        in_specs=[pl.BlockSpec((IO_TILE, D_MODEL), lambda i: (i + tile0, 0)), _const_spec((1, D_MODEL))],
        out_specs=pl.BlockSpec((IO_TILE, D_MODEL), lambda i: (i, 0)),
        out_shape=jax.ShapeDtypeStruct((n_tiles * IO_TILE, D_MODEL), F32),
        compiler_params=_cparams(("parallel",)),
        name="final_norm",
    )(x, g)


def _grid_pos_embed(n_tokens):
    rows = n_tokens // GRID_W
    quarter = D_MODEL // 4
    half = D_MODEL // 2
    omega = 1.0 / (10000.0 ** (jnp.arange(quarter, dtype=F32) / quarter))
    ang_r = jnp.arange(rows, dtype=F32)[:, None] * omega
    ang_c = jnp.arange(GRID_W, dtype=F32)[:, None] * omega
    emb_r = jnp.concatenate([jnp.sin(ang_r), jnp.cos(ang_r)], axis=-1)
    emb_c = jnp.concatenate([jnp.sin(ang_c), jnp.cos(ang_c)], axis=-1)
    emb = jnp.concatenate([jnp.broadcast_to(emb_r[:, None, :], (rows, GRID_W, half)),
                           jnp.broadcast_to(emb_c[None, :, :], (rows, GRID_W, half))], axis=-1)
    return emb.reshape(rows * GRID_W, D_MODEL)


def _pad_rows(w, rows, offset):
    out = jnp.zeros(w.shape[:-2] + (rows, w.shape[-1]), w.dtype)
    return lax.dynamic_update_slice_in_dim(out, w, offset, axis=-2)


def _hg_cols(w):
    return jnp.moveaxis(w.reshape(w.shape[:-1] + (N_HG, HG_LANES)), -2, 0)


def kernel(x_prompt, x_sample, state_rwkv, c, c_ctx, norm1_g, norm2_g, w_mod, b_mod, w_in, b_in, rwkv_w0, rwkv_w_up,
           rwkv_a0, rwkv_a_up, rwkv_g_up, rwkv_k_k, rwkv_k_a, rwkv_r_k, rwkv_gn_g, rwkv_gn_b, rwkv_w_o, conv_dw,
           conv_dw_b, conv_ln_g, conv_ln_b, conv_w_o, sgu_ln_g, sgu_ln_b, sgu_w_s, sgu_b_s, sgu_w_o, pool_w,
           pool_scale, pool_w_o, w_out, moe_w_gate, moe_w_up, moe_w_down, router_w, router_b, final_norm_g):
    n_ctx, t_ctx, d = x_prompt.shape
    n_lat, t_lat, _ = x_sample.shape
    n_layers = w_in.shape[0]
    assert d == D_MODEL and t_ctx % TILE == 0 and t_lat % TILE == 0 and n_lat < MOD_ROWS
    ctx_rows, lat_rows = n_ctx * t_ctx, n_lat * t_lat
    assert ctx_rows % t_lat == 0 and t_lat % PROJ_TILE == 0 and ROUTED_TILE == TILE
    ctx_tiles = ctx_rows // TILE
    seq_tiles_ctx, seq_tiles_lat = t_ctx // TILE, t_lat // TILE

    def mod_row_for(tile_rows):
        ctx_t = ctx_rows // tile_rows
        per_lat = t_lat // tile_rows
        return lambda i: jnp.where(i < ctx_t, n_lat, jnp.maximum(i - ctx_t, 0) // per_lat)

    mod_row = mod_row_for(TILE)
    mod_row_proj = mod_row_for(PROJ_TILE)

    cond = jnp.zeros((MOD_ROWS, d), F32).at[:n_lat].set(c).at[n_lat].set(c_ctx)
    mod = _modulation(cond, w_mod, b_mod).reshape(n_layers, MOD_ROWS, 1, 6 * d)

    assert ctx_rows % IO_TILE == 0 and t_lat % IO_TILE == 0
    x = _embed(x_prompt.reshape(ctx_rows, d), x_sample.reshape(lat_rows, d), _grid_pos_embed(t_lat), t_lat // IO_TILE)

    n_rw = RW_COLS
    n_mx = MX_COLS
    wa = w_in[:, :, :n_rw].astype(BF16)
    ba = b_in[:, None, :n_rw]
    wm = w_in[:, :, n_rw:n_rw + n_mx].astype(BF16)
    bm = b_in[:, None, n_rw:n_rw + n_mx]
    wgt = w_in[:, :, n_rw + n_mx:].astype(BF16)
    bgt = b_in[:, None, n_rw + n_mx:]

    par = jnp.zeros((n_layers, P_ROWS, RWKV_WIDTH), F32)
    par = par.at[:, P_W0:P_W0 + 2].set(rwkv_w0).at[:, P_A0:P_A0 + 2].set(rwkv_a0)
    par = par.at[:, P_KK].set(rwkv_k_k).at[:, P_KA].set(rwkv_k_a)
    par = par.at[:, P_RK].set(rwkv_r_k.reshape(n_layers, RWKV_WIDTH))
    par = par.at[:, P_GNG].set(rwkv_gn_g).at[:, P_GNB].set(rwkv_gn_b)
    par = jnp.moveaxis(_hg_cols(par), 0, 1)
    lr_k = RWKV_DECAY_RANK + RWKV_ICLR_RANK
    wup = jnp.moveaxis(_hg_cols(_pad_rows(rwkv_w_up, lr_k, 0)), 0, 2).astype(BF16)
    aup = jnp.moveaxis(_hg_cols(_pad_rows(rwkv_a_up, lr_k, RWKV_DECAY_RANK)), 0, 2).astype(BF16)
    gup = jnp.moveaxis(_hg_cols(rwkv_g_up), 0, 1).astype(BF16)
    rw_consts = _rwkv_consts()

    mix_vecs = jnp.zeros((n_layers, 8, CONV_WIDTH), F32)
    for row, vec in enumerate((conv_dw_b, conv_ln_g, conv_ln_b, sgu_ln_g, sgu_ln_b, pool_scale)):
        mix_vecs = mix_vecs.at[:, row].set(vec)
    sgu_wcat = jnp.transpose(sgu_w_s, (0, 2, 1, 3)).reshape(n_layers, SGU_CHUNK, SGU_GROUPS * SGU_CHUNK).astype(BF16)
    sgu_bs = jnp.repeat(jnp.swapaxes(sgu_b_s, 1, 2), SGU_WIDTH // SGU_GROUPS, axis=2)
    pc = POOL_WIDTH // POOL_GROUPS
    pool_bd = (pool_w[:, :, :, None, :] * jnp.eye(POOL_GROUPS, dtype=F32)[None, :, None, :, None])
    pool_bd = pool_bd.reshape(n_layers, POOL_WIDTH, POOL_WIDTH).astype(BF16)
    lane_grp = jnp.arange(SGU_WIDTH)[None, :] // (SGU_WIDTH // SGU_GROUPS)
    sgm = (jnp.arange(SGU_GROUPS * SGU_CHUNK)[:, None] // SGU_CHUNK == lane_grp).astype(BF16)
    wlen = TILE + 2 * HALO
    plm = (jnp.arange(POOL_GROUPS * wlen)[:, None] // wlen == jnp.arange(POOL_WIDTH)[None, :] // pc).astype(BF16)
    pool_band, pool_cnt = _pool_tables()

    wro, wco, wso, wpo, wout = (w.astype(BF16) for w in (rwkv_w_o, conv_w_o, sgu_w_o, pool_w_o, w_out))
    rw_t = jnp.pad(router_w, ((0, 0), (0, ROUTER_LANES - N_EXPERTS)))
    rb = router_b[:, None]

    ctx_s0 = jnp.zeros((n_ctx, N_HG, 2, RWKV_HEAD_DIM, HG_LANES), F32)
    ctx_states = []
    for l in range(n_layers):
        zr, zm = _in_projection(x, mod[l], mod_row_proj, norm1_g[l][None], wa[l], ba[l], wm[l], bm[l])
        rw_args = (par[l], wup[l], aup[l], gup[l], rw_consts)
        ya = jnp.zeros((x.shape[0], RWKV_WIDTH), BF16)
        ya, s_fin = _rwkv_mixer(zr, ctx_s0, n_ctx, t_ctx, 0, ya, *rw_args)
        ya, _ = _rwkv_mixer(zr, _states_to_kernel(state_rwkv[:, l]), n_lat, t_lat, ctx_rows // t_lat, ya, *rw_args)
        ctx_states.append(_states_from_kernel(s_fin))
        mx = _mixers(zm, ctx_tiles, seq_tiles_ctx, seq_tiles_lat, conv_dw[l], mix_vecs[l], sgu_wcat[l], sgu_bs[l],
                     pool_bd[l], sgm, plm, pool_band, pool_cnt)
        x1, h2, route_i, route_w = _merge(x, mod[l], mod_row, norm1_g[l][None], norm2_g[l][None], ya, mx, wgt[l], bgt[l],
                                 wro[l], wco[l], wso[l], wpo[l], wout[l], rw_t, rb)
        dest, src, w_sorted, tile_lo, tile_hi, tiles_used = _dispatch_plan(route_i, route_w)
        ff = _experts(h2, src, w_sorted, tile_lo, tile_hi, tiles_used, l, moe_w_gate, moe_w_up, moe_w_down)
        x = _combine(ff, dest, x1, mod[l], mod_row)

    g_fin = final_norm_g[None]
    y_prompt = _final_norm(x, g_fin, 0, ctx_rows // IO_TILE).reshape(n_ctx, t_ctx, d)
    y_sample = _final_norm(x, g_fin, ctx_rows // IO_TILE, lat_rows // IO_TILE).reshape(n_lat, t_lat, d)
    new_state = jnp.stack(ctx_states, axis=1).astype(x_prompt.dtype)
    return (y_prompt, y_sample, new_state)
```
